```python
import math
import jax, jax.numpy as jnp
from jax import lax
import numpy as np

D_MODEL = 1024
BATCH = 32
SEQ = 256
DEPTH = 4
DEC_BATCH = 8
DEC_SEQ = 1024
PAST_LEN = 512

F32 = jnp.float32
GRID_W = 64
N_EVEN = (DEPTH + 1) // 2
N_ODD = DEPTH // 2
HEAD_DIM = 64
ROPE_BASE = 10000.0
Q_BLOCK = 128
MASK_VALUE = -1e30
F_FLOOR = 1e-30
A_HEADS = 8
A_KV = 2
WINDOW = 128
B_HEADS = 4
B_DK = 128
B_DV = 128
B_CHUNK = 16
C_HEADS = 8
C_Q_LORA = 384
C_KV_LORA = 256
C_NOPE = 64
C_ROPE = 32
C_V = 64
D_HEADS = 8
D_KV = 4
MIX = A_HEADS * HEAD_DIM + B_HEADS * B_DV
D_FF = -(-8 * D_MODEL // (3 * 256)) * 256
EVEN_SIZES = (A_HEADS * HEAD_DIM, A_KV * HEAD_DIM, A_KV * HEAD_DIM,
              B_HEADS * B_DK, B_HEADS * B_DV, B_HEADS * B_DK, B_HEADS * B_DK, B_HEADS * B_DV)
ODD_SIZES = (C_Q_LORA, C_KV_LORA, C_ROPE, D_HEADS * HEAD_DIM, D_KV * HEAD_DIM, D_KV * HEAD_DIM)
EVEN_SPLIT = tuple(int(i) for i in np.cumsum(EVEN_SIZES)[:-1])
ODD_SPLIT = tuple(int(i) for i in np.cumsum(ODD_SIZES)[:-1])
EVEN_IN = int(sum(EVEN_SIZES))
ODD_IN = int(sum(ODD_SIZES))
ALPHA = (2 * DEPTH) ** 0.25
BETA = (8 * DEPTH) ** -0.25

kernel_name = 'hybrid_flow_backbone_step'


def rms_norm(x, g, eps=1e-6):
    xf = x.astype(F32)
    y = xf * lax.rsqrt(jnp.mean(xf * xf, axis=-1, keepdims=True) + eps)
    return (y * g.astype(F32)).astype(x.dtype)


def layer_norm(x, g, b, eps=1e-5):
    xf = x.astype(F32)
    mu = jnp.mean(xf, axis=-1, keepdims=True)
    xc = xf - mu
    var = jnp.mean(xc * xc, axis=-1, keepdims=True)
    return (xc * lax.rsqrt(var + eps) * g.astype(F32) + b.astype(F32)).astype(x.dtype)


def axial_rope_tables(n_tokens, rot_dim):
    n_rows = n_tokens // GRID_W
    row = jnp.broadcast_to(jnp.arange(n_rows, dtype=F32)[:, None], (n_rows, GRID_W)).reshape(-1)
    col = jnp.broadcast_to(jnp.arange(GRID_W, dtype=F32)[None, :], (n_rows, GRID_W)).reshape(-1)
    quarter = rot_dim // 4
    inv = ROPE_BASE ** (-jnp.arange(quarter, dtype=F32) / quarter)
    ar = row[:, None] * inv
    ac = col[:, None] * inv
    return (jnp.cos(ar), jnp.sin(ar), jnp.cos(ac), jnp.sin(ac))


def apply_axial_rope(x, tables):
    cos_r, sin_r, cos_c, sin_c = tables
    half = x.shape[-1] // 2
    xf = x.astype(F32)

    def rot(t, cos, sin):
        t1, t2 = jnp.split(t, 2, axis=-1)
        cos = cos[:, None, :]
        sin = sin[:, None, :]
        return jnp.concatenate([t1 * cos - t2 * sin, t2 * cos + t1 * sin], axis=-1)

    out = jnp.concatenate([rot(xf[..., :half], cos_r, sin_r), rot(xf[..., half:], cos_c, sin_c)], axis=-1)
    return out.astype(x.dtype)


def dense_gqa(q, k, v, sink=None):
    b, lq, h, dq = q.shape
    kv = k.shape[2]
    g = h // kv
    dv = v.shape[-1]
    scale = dq ** -0.5
    qb = q.reshape(b, lq // Q_BLOCK, Q_BLOCK, kv, g, dq).transpose(1, 0, 2, 3, 4, 5)

    def one_block(qblk):
        s = jnp.einsum('bqkgd,bskd->bkgqs', qblk, k).astype(F32) * scale
        if sink is not None:
            sk = jnp.broadcast_to(sink.astype(F32).reshape(1, kv, g, 1, 1), s.shape[:-1] + (1,))
            p = jax.nn.softmax(jnp.concatenate([s, sk], axis=-1), axis=-1)[..., :-1]
        else:
            p = jax.nn.softmax(s, axis=-1)
        return jnp.einsum('bkgqs,bskd->bqkgd', p.astype(v.dtype), v)

    o = lax.map(one_block, qb)
    return o.transpose(1, 0, 2, 3, 4, 5).reshape(b, lq, h, dv)


def banded_gqa_with_context(q, k, v, k_ctx, v_ctx, sink):
    b, l, h, d = q.shape
    kv = k.shape[2]
    g = h // kv
    nb = l // Q_BLOCK
    p_len = k_ctx.shape[1]
    scale = d ** -0.5
    qb = q.reshape(b, nb, Q_BLOCK, kv, g, d)

    def windows(t):
        tp = jnp.pad(t, ((0, 0), (Q_BLOCK, Q_BLOCK), (0, 0), (0, 0))).reshape(b, nb + 2, Q_BLOCK, kv, d)
        return jnp.concatenate([tp[:, :-2], tp[:, 1:-1], tp[:, 2:]], axis=2)

    kw, vw = windows(k), windows(v)
    qpos = jnp.arange(l).reshape(nb, Q_BLOCK)
    kpos = (jnp.arange(nb)[:, None] - 1) * Q_BLOCK + jnp.arange(3 * Q_BLOCK)[None, :]
    kp = kpos[:, None, :]
    valid = (kp >= 0) & (kp < l) & (jnp.abs(kp - qpos[:, :, None]) <= WINDOW)
    s_loc = jnp.einsum('bnqkgd,bnskd->bkgnqs', qb, kw).astype(F32) * scale
    s_loc = jnp.where(valid, s_loc, MASK_VALUE)
    s_ctx = jnp.einsum('bnqkgd,bpkd->bkgnqp', qb, k_ctx).astype(F32) * scale
    sk = jnp.broadcast_to(sink.astype(F32).reshape(1, kv, g, 1, 1, 1), s_loc.shape[:-1] + (1,))
    p = jax.nn.softmax(jnp.concatenate([s_loc, s_ctx, sk], axis=-1), axis=-1)
    p_loc = p[..., :3 * Q_BLOCK].astype(v.dtype)
    p_ctx = p[..., 3 * Q_BLOCK:3 * Q_BLOCK + p_len].astype(v.dtype)
    o = jnp.einsum('bkgnqs,bnskd->bnqkgd', p_loc, vw) + jnp.einsum('bkgnqp,bpkd->bnqkgd', p_ctx, v_ctx)
    return o.reshape(b, l, h, d)


def hgrn2_gate(z, lb):
    f = lb + (1.0 - lb) * jax.nn.sigmoid(z.astype(F32))
    return jnp.log(jnp.maximum(f, F_FLOOR)), 1.0 - f


def hgrn2_scan(q, k, v, logf, s0):
    b, l, h, dk = q.shape
    dv = v.shape[-1]
    n = l // B_CHUNK

    def chunks(t):
        return t.astype(F32).reshape(b, n, B_CHUNK, h, t.shape[-1]).transpose(0, 3, 1, 2, 4)

    qc, kc, vc, gc = chunks(q) * dk ** -0.5, chunks(k), chunks(v), chunks(logf)
    cum = jnp.cumsum(gc, axis=3)
    tri = jnp.tril(jnp.ones((B_CHUNK, B_CHUNK), bool))[:, :, None]
    diff = cum[..., :, None, :] - cum[..., None, :, :]
    decay = jnp.where(tri, jnp.exp(jnp.where(tri, diff, 0.0)), 0.0)
    attn = jnp.einsum('bhntk,bhntsk,bhnsk->bhnts', qc, decay, kc)
    o_intra = jnp.einsum('bhnts,bhnsv->bhntv', attn, vc)
    last = cum[..., -1:, :]
    upd = jnp.einsum('bhnsk,bhnsv->bhnkv', kc * jnp.exp(last - cum), vc)
    dec = jnp.exp(last[..., 0, :])

    def step(s, inp):
        d_n, u_n = inp
        return d_n[..., None] * s + u_n, s

    s_fin, s_start = lax.scan(step, s0.astype(F32), (dec.transpose(2, 0, 1, 3), upd.transpose(2, 0, 1, 3, 4)))
    s_start = s_start.transpose(1, 2, 0, 3, 4)
    o_inter = jnp.einsum('bhntk,bhnkv->bhntv', qc * jnp.exp(cum), s_start)
    o = (o_intra + o_inter).transpose(0, 2, 3, 1, 4).reshape(b, l, h, dv)
    return o.astype(v.dtype), s_fin


def hgrn2_bidir(q, k_f, logf_f, k_b, logf_b, v, s0_f, s0_b):
    o_f, s_f = hgrn2_scan(q, k_f, v, logf_f, s0_f)
    flip = lambda t: jnp.flip(t, axis=1)
    o_b, s_b = hgrn2_scan(flip(q), flip(k_b), flip(v), flip(logf_b), s0_b)
    return o_f + flip(o_b), s_f, s_b


def mixer_ab(h, w_in, w_out, sink, lb, gnorm, rope=None, ctx=None):
    bsz, l, _ = h.shape
    qa, ka, va, qb, ib, ff, fb, gb = jnp.split(h @ w_in, EVEN_SPLIT, axis=-1)
    qa = qa.reshape(bsz, l, A_HEADS, HEAD_DIM)
    ka = ka.reshape(bsz, l, A_KV, HEAD_DIM)
    va = va.reshape(bsz, l, A_KV, HEAD_DIM)
    qb = jax.nn.silu(qb).reshape(bsz, l, B_HEADS, B_DK)
    ib = ib.reshape(bsz, l, B_HEADS, B_DV)
    logf_f, kf = hgrn2_gate(ff, lb[0])
    logf_b, kb = hgrn2_gate(fb, lb[1])
    hs = (bsz, l, B_HEADS, B_DK)
    logf_f, kf, logf_b, kb = logf_f.reshape(hs), kf.reshape(hs), logf_b.reshape(hs), kb.reshape(hs)
    if ctx is None:
        oa = dense_gqa(qa, ka, va, sink)
        zeros = jnp.zeros((bsz, B_HEADS, B_DK, B_DV), F32)
        ob, s_f, s_b = hgrn2_bidir(qb, kf, logf_f, kb, logf_b, ib, zeros, zeros)
        new = (ka, va, jnp.stack([s_f, s_b], axis=1))
    else:
        k_ctx, v_ctx, s_ctx = ctx
        oa = banded_gqa_with_context(apply_axial_rope(qa, rope), apply_axial_rope(ka, rope), va, k_ctx, v_ctx, sink)
        ob, _, _ = hgrn2_bidir(qb, kf, logf_f, kb, logf_b, ib, s_ctx[:, 0], s_ctx[:, 1])
        new = None
    ob = rms_norm(ob, gnorm) * jax.nn.silu(gb.reshape(bsz, l, B_HEADS, B_DV))
    y = jnp.concatenate([oa.reshape(bsz, l, -1), ob.reshape(bsz, l, -1)], axis=-1) @ w_out
    return y, new


def mixer_cd(h, w_in, w_out, g_cq, g_ckv, w_q_up, w_kv_up, g_dq, g_dk, rope_c=None, rope_d=None, ctx=None):
    bsz, l, _ = h.shape
    cq, ckv, kpe, qd, kd, vd = jnp.split(h @ w_in, ODD_SPLIT, axis=-1)
    qc = (rms_norm(cq, g_cq) @ w_q_up).reshape(bsz, l, C_HEADS, C_NOPE + C_ROPE)
    ckv = rms_norm(ckv, g_ckv)
    kpe = kpe[:, :, None, :]
    qd = rms_norm(qd.reshape(bsz, l, D_HEADS, HEAD_DIM), g_dq)
    kd = rms_norm(kd.reshape(bsz, l, D_KV, HEAD_DIM), g_dk)
    vd = vd.reshape(bsz, l, D_KV, HEAD_DIM)

    def mla_kv(lat, pe):
        n = lat.shape[1]
        kvu = (lat @ w_kv_up).reshape(bsz, n, C_HEADS, C_NOPE + C_V)
        k = jnp.concatenate([kvu[..., :C_NOPE], jnp.broadcast_to(pe, (bsz, n, C_HEADS, C_ROPE))], axis=-1)
        return k, kvu[..., C_NOPE:]

    if ctx is None:
        kc, vc = mla_kv(ckv, kpe)
        oc = dense_gqa(qc, kc, vc)
        od = dense_gqa(qd, kd, vd)
        new = (ckv, kpe[:, :, 0], kd, vd)
    else:
        c_ctx, pe_ctx, kd_ctx, vd_ctx = ctx
        qc = jnp.concatenate([qc[..., :C_NOPE], apply_axial_rope(qc[..., C_NOPE:], rope_c)], axis=-1)
        kpe = apply_axial_rope(kpe, rope_c)
        kc, vc = mla_kv(jnp.concatenate([ckv, c_ctx], axis=1), jnp.concatenate([kpe, pe_ctx[:, :, None, :]], axis=1))
        oc = dense_gqa(qc, kc, vc)
        qd, kd = apply_axial_rope(qd, rope_d), apply_axial_rope(kd, rope_d)
        od = dense_gqa(qd, jnp.concatenate([kd, kd_ctx], axis=1), jnp.concatenate([vd, vd_ctx], axis=1))
        new = None
    y = jnp.concatenate([oc.reshape(bsz, l, -1), od.reshape(bsz, l, -1)], axis=-1) @ w_out
    return y, new


def swiglu(h, w_gate, w_up, w_down):
    return (jax.nn.silu(h @ w_gate) * (h @ w_up)) @ w_down


def modulation(cond, w, b):
    return jnp.split((jax.nn.silu(cond) @ w + b)[:, None, :], 6, axis=-1)


def setup_inputs(seed: int = 0) -> dict:
    key = jax.random.key(seed)
    ks = list(jax.random.split(key, 40))

    def nrm(shape, scale=1.0):
        return jax.random.normal(ks.pop(), shape, F32) * scale

    def gain(shape):
        return 1.0 + nrm(shape, 0.02)

    d = D_MODEL
    return {
        'x_prompt': nrm((BATCH, SEQ, d)),
        'x_sample': nrm((DEC_BATCH, DEC_SEQ, d)),
        'cache_a_k': nrm((DEC_BATCH, N_EVEN, PAST_LEN, A_KV, HEAD_DIM)),
        'cache_a_v': nrm((DEC_BATCH, N_EVEN, PAST_LEN, A_KV, HEAD_DIM)),
        'state_b': nrm((DEC_BATCH, N_EVEN, 2, B_HEADS, B_DK, B_DV), 0.5),
        'cache_c_kv': nrm((DEC_BATCH, N_ODD, PAST_LEN, C_KV_LORA)),
        'cache_c_pe': nrm((DEC_BATCH, N_ODD, PAST_LEN, C_ROPE)),
        'cache_d_k': nrm((DEC_BATCH, N_ODD, PAST_LEN, D_KV, HEAD_DIM)),
        'cache_d_v': nrm((DEC_BATCH, N_ODD, PAST_LEN, D_KV, HEAD_DIM)),
        'c': nrm((DEC_BATCH, d)),
        'c_ctx': nrm((d,)),
        'w_ada': nrm((DEPTH, d, 6 * d), 0.5 * d ** -0.5),
        'b_ada': nrm((DEPTH, 6 * d), 0.02),
        'ln_g': gain((DEPTH, 2, d)),
        'ln_b': nrm((DEPTH, 2, d), 0.02),
        'w_in_ab': nrm((N_EVEN, d, EVEN_IN), d ** -0.5),
        'a_sink': nrm((N_EVEN, A_HEADS), 0.5),
        'b_lb': nrm((N_EVEN, 2, B_HEADS * B_DK), 0.5),
        'b_gnorm': gain((N_EVEN, B_DV)),
        'w_in_cd': nrm((N_ODD, d, ODD_IN), d ** -0.5),
        'c_q_norm': gain((N_ODD, C_Q_LORA)),
        'c_kv_norm': gain((N_ODD, C_KV_LORA)),
        'c_w_q_up': nrm((N_ODD, C_Q_LORA, C_HEADS * (C_NOPE + C_ROPE)), C_Q_LORA ** -0.5),
        'c_w_kv_up': nrm((N_ODD, C_KV_LORA, C_HEADS * (C_NOPE + C_V)), C_KV_LORA ** -0.5),
        'd_q_norm': gain((N_ODD, HEAD_DIM)),
        'd_k_norm': gain((N_ODD, HEAD_DIM)),
        'w_out': nrm((DEPTH, MIX, d), BETA * MIX ** -0.5),
        'w_ffn_gate': nrm((DEPTH, d, D_FF), d ** -0.5),
        'w_ffn_up': nrm((DEPTH, d, D_FF), d ** -0.5),
        'w_ffn_down': nrm((DEPTH, D_FF, d), BETA * D_FF ** -0.5),
    }


def reference(x_prompt, x_sample, cache_a_k, cache_a_v, state_b, cache_c_kv, cache_c_pe, cache_d_k, cache_d_v,
              c, c_ctx, w_ada, b_ada, ln_g, ln_b, w_in_ab, a_sink, b_lb, b_gnorm, w_in_cd, c_q_norm, c_kv_norm,
              c_w_q_up, c_w_kv_up, d_q_norm, d_k_norm, w_out, w_ffn_gate, w_ffn_up, w_ffn_down):
    n_lat = x_sample.shape[1]
    rope_hd = axial_rope_tables(n_lat, HEAD_DIM)
    rope_c = axial_rope_tables(n_lat, C_ROPE)
    lb_w = jax.nn.softmax(b_lb.astype(F32), axis=0)
    lb_all = jnp.cumsum(lb_w, axis=0) - lb_w[:1]
    xp, xs = x_prompt, x_sample
    ak, av, sb, ckv, cpe, dk, dv = [], [], [], [], [], [], []
    for l in range(DEPTH):
        mp = modulation(c_ctx[None, :], w_ada[l], b_ada[l])
        ms = modulation(c, w_ada[l], b_ada[l])
        hp = xp * (1 + mp[1]) + mp[0]
        hs = xs * (1 + ms[1]) + ms[0]
        if l % 2 == 0:
            e = l // 2
            wts = (w_in_ab[e], w_out[l], a_sink[e], lb_all[e], b_gnorm[e])
            yp, (k_new, v_new, s_new) = mixer_ab(hp, *wts)
            ys, _ = mixer_ab(hs, *wts, rope=rope_hd, ctx=(cache_a_k[:, e], cache_a_v[:, e], state_b[:, e]))
            ak.append(k_new)
            av.append(v_new)
            sb.append(s_new)
        else:
            o = l // 2
            wts = (w_in_cd[o], w_out[l], c_q_norm[o], c_kv_norm[o], c_w_q_up[o], c_w_kv_up[o], d_q_norm[o], d_k_norm[o])
            yp, (c_new, pe_new, k_new, v_new) = mixer_cd(hp, *wts)
            ys, _ = mixer_cd(hs, *wts, rope_c=rope_c, rope_d=rope_hd,
                             ctx=(cache_c_kv[:, o], cache_c_pe[:, o], cache_d_k[:, o], cache_d_v[:, o]))
            ckv.append(c_new)
            cpe.append(pe_new)
            dk.append(k_new)
            dv.append(v_new)
        xp = layer_norm(ALPHA * xp + mp[2] * yp, ln_g[l, 0], ln_b[l, 0])
        xs = layer_norm(ALPHA * xs + ms[2] * ys, ln_g[l, 0], ln_b[l, 0])
        hp = xp * (1 + mp[4]) + mp[3]
        hs = xs * (1 + ms[4]) + ms[3]
        xp = layer_norm(ALPHA * xp + mp[5] * swiglu(hp, w_ffn_gate[l], w_ffn_up[l], w_ffn_down[l]), ln_g[l, 1], ln_b[l, 1])
        xs = layer_norm(ALPHA * xs + ms[5] * swiglu(hs, w_ffn_gate[l], w_ffn_up[l], w_ffn_down[l]), ln_g[l, 1], ln_b[l, 1])
    new_cache_a_k = jnp.stack(ak, axis=1)
    new_cache_a_v = jnp.stack(av, axis=1)
    new_state_b = jnp.stack(sb, axis=1).astype(x_prompt.dtype)
    new_cache_c_kv = jnp.stack(ckv, axis=1)
    new_cache_c_pe = jnp.stack(cpe, axis=1)
    new_cache_d_k = jnp.stack(dk, axis=1)
    new_cache_d_v = jnp.stack(dv, axis=1)
    return (xp, xs, new_cache_a_k, new_cache_a_v, new_state_b, new_cache_c_kv, new_cache_c_pe, new_cache_d_k, new_cache_d_v)
```

```python
import functools

import numpy as np
import jax
import jax.numpy as jnp
from jax import lax
from jax.experimental import pallas as pl
from jax.experimental.pallas import tpu as pltpu

F32 = jnp.float32
BF16 = jnp.bfloat16
HIGHEST = lax.Precision.HIGHEST

D_MODEL = 1024
DEPTH = 4
GRID_W = 64
HEAD_DIM = 64
ROPE_BASE = 10000.0
MASK_VALUE = -1e30
F_FLOOR = 1e-30
A_HEADS = 8
A_KV = 2
WINDOW = 128
B_HEADS = 4
B_DK = 128
B_DV = 128
C_HEADS = 8
C_Q_LORA = 384
C_KV_LORA = 256
C_NOPE = 64
C_ROPE = 32
C_V = 64
D_HEADS = 8
D_KV = 4
D_FF = 2816
ALPHA = (2 * DEPTH) ** 0.25

LANES = 128
HGRN_CHUNK = 16
HGRN_GROUP = 128
VMEM_LIMIT = 56 * 1024 * 1024
NT_DIMS = (((1,), (1,)), ((), ()))
TN_DIMS = (((0,), (0,)), ((), ()))


def _cparams(n_axes):
    return pltpu.CompilerParams(dimension_semantics=("arbitrary",) * n_axes,
                                vmem_limit_bytes=VMEM_LIMIT)


def _silu(x):
    return x * jax.nn.sigmoid(x)


def _dot(a, b):
    return jnp.dot(a, b, preferred_element_type=F32)


def _layer_norm(x, g, b):
    mu = jnp.mean(x, axis=-1, keepdims=True)
    xc = x - mu
    var = jnp.mean(xc * xc, axis=-1, keepdims=True)
    return xc * lax.rsqrt(var + 1e-5) * g + b


def _rms(x, g):
    return x * lax.rsqrt(jnp.mean(x * x, axis=-1, keepdims=True) + 1e-6) * g


def _rms_heads64(x, g):
    outs = []
    for j in range(x.shape[1] // LANES):
        xg = x[:, j * LANES:(j + 1) * LANES]
        x2 = xg * xg
        lo = lax.broadcasted_iota(jnp.int32, xg.shape, 1) < HEAD_DIM
        s_lo = jnp.sum(jnp.where(lo, x2, 0.0), axis=-1, keepdims=True)
        s_hi = jnp.sum(jnp.where(lo, 0.0, x2), axis=-1, keepdims=True)
        ms = jnp.where(lo, s_lo, s_hi) * (1.0 / HEAD_DIM)
        outs.append(xg * lax.rsqrt(ms + 1e-6))
    return jnp.concatenate(outs, axis=1) * g


def _rope(x, cos, sin, half):
    outs = []
    for j in range(x.shape[1] // LANES):
        xg = x[:, j * LANES:(j + 1) * LANES]
        lane = lax.broadcasted_iota(jnp.int32, xg.shape, 1)
        up = pltpu.roll(xg, LANES - half, 1)
        down = pltpu.roll(xg, half, 1)
        rot = jnp.where((lane & (2 * half - 1)) < half, up, down)
        outs.append(xg * cos + rot * sin)
    return outs[0] if len(outs) == 1 else jnp.concatenate(outs, axis=1)


def _rope_tables(n_tokens, rot_dim):
    n_rows = n_tokens // GRID_W
    row = jnp.broadcast_to(jnp.arange(n_rows, dtype=F32)[:, None], (n_rows, GRID_W)).reshape(-1)
    col = jnp.broadcast_to(jnp.arange(GRID_W, dtype=F32)[None, :], (n_rows, GRID_W)).reshape(-1)
    quarter = rot_dim // 4
    inv = ROPE_BASE ** (-jnp.arange(quarter, dtype=F32) / quarter)
    ar = row[:, None] * inv
    ac = col[:, None] * inv
    cos = jnp.concatenate([jnp.cos(ar), jnp.cos(ar), jnp.cos(ac), jnp.cos(ac)], axis=-1)
    sin = jnp.concatenate([-jnp.sin(ar), jnp.sin(ar), -jnp.sin(ac), jnp.sin(ac)], axis=-1)
    reps = LANES // rot_dim
    return jnp.tile(cos, (1, reps)), jnp.tile(sin, (1, reps))


def _mod_kernel(c_ref, w_ref, b_ref, o_ref):
    a = _silu(c_ref[...]).astype(BF16)
    o_ref[0] = _dot(a, w_ref[0].astype(BF16)) + b_ref[0]


def _modulation(c, c_ctx, w_ada, b_ada):
    nb = c.shape[0]
    cond = jnp.zeros((16, D_MODEL), F32).at[0].set(c_ctx).at[1:1 + nb].set(c)
    tn = 1536
    out = pl.pallas_call(
        _mod_kernel,
        grid=(DEPTH, 6 * D_MODEL // tn),
        in_specs=[pl.BlockSpec((16, D_MODEL), lambda l, j: (0, 0)),
                  pl.BlockSpec((1, D_MODEL, tn), lambda l, j: (l, 0, j)),
                  pl.BlockSpec((1, 1, tn), lambda l, j: (l, 0, j))],
        out_specs=pl.BlockSpec((1, 16, tn), lambda l, j: (l, 0, j)),
        out_shape=jax.ShapeDtypeStruct((DEPTH, 16, 6 * D_MODEL), F32),
        compiler_params=_cparams(2),
        name="modulation",
    )(cond, w_ada, b_ada.reshape(DEPTH, 1, 6 * D_MODEL))
    return out.reshape(DEPTH, 16, 6, D_MODEL)


def _modulate(x_ref, mod_ref, shift_row):
    m = mod_ref[0]
    return (x_ref[...] * (1.0 + m[shift_row + 1:shift_row + 2]) + m[shift_row:shift_row + 1]).astype(BF16)


def _inproj_even_kernel(rope, *refs):
    if rope:
        x_ref, mod_ref, w_ref, lb_ref, cos_ref, sin_ref = refs[:6]
        outs = refs[6:]
    else:
        x_ref, mod_ref, w_ref, lb_ref = refs[:4]
        outs = refs[4:]
    qa_ref, ka_ref, va_ref, qb_ref, ib_ref, kf_ref, lf_ref, kb_ref, lbw_ref, gb_ref = outs
    h = _modulate(x_ref, mod_ref, 0)

    def proj(a, b):
        return _dot(h, w_ref[:, a:b])

    qa = proj(0, 512)
    ka = proj(512, 640)
    if rope:
        cos = cos_ref[...]
        sin = sin_ref[...]
        qa = _rope(qa, cos, sin, HEAD_DIM // 4)
        ka = _rope(ka, cos, sin, HEAD_DIM // 4)
    qa_ref[...] = qa
    ka_ref[...] = ka
    va_ref[...] = proj(640, 768)
    qb_ref[...] = _silu(proj(768, 1280))
    ib_ref[...] = proj(1280, 1792)
    for d, (k_ref, l_ref) in enumerate(((kf_ref, lf_ref), (kb_ref, lbw_ref))):
        lb = lb_ref[d:d + 1, :]
        f = lb + (1.0 - lb) * jax.nn.sigmoid(proj(1792 + 512 * d, 2304 + 512 * d))
        l_ref[...] = jnp.log(jnp.maximum(f, F_FLOOR))
        k_ref[...] = 1.0 - f
    gb_ref[...] = _silu(proj(2816, 3328))


def _inproj_even(x, mod, w_bf, lb, rope_tabs, rows_per_cond, tm=256):
    m_rows = x.shape[0]
    rope = rope_tabs is not None
    widths = (512, 128, 128, 512, 512, 512, 512, 512, 512, 512)
    in_specs = [pl.BlockSpec((tm, D_MODEL), lambda i: (i, 0)),
                pl.BlockSpec((1, 6, D_MODEL), lambda i: (i * tm // rows_per_cond, 0, 0)),
                pl.BlockSpec(w_bf.shape, lambda i: (0, 0)),
                pl.BlockSpec(lb.shape, lambda i: (0, 0))]
    args = [x, mod, w_bf, lb]
    if rope:
        nblk = rope_tabs[0].shape[0] // tm
        in_specs += [pl.BlockSpec((tm, LANES), lambda i: (i % nblk, 0))] * 2
        args += list(rope_tabs)
    return pl.pallas_call(
        functools.partial(_inproj_even_kernel, rope),
        grid=(m_rows // tm,),
        in_specs=in_specs,
        out_specs=[pl.BlockSpec((tm, w), lambda i: (i, 0)) for w in widths],
        out_shape=[jax.ShapeDtypeStruct((m_rows, w), F32) for w in widths],
        compiler_params=_cparams(1),
        name="inproj_even",
    )(*args)


def _inproj_odd_kernel(rope, *refs):
    if rope:
        (x_ref, mod_ref, w_ref, gcq_ref, gckv_ref, wq_ref, wkv_ref, gdq_ref, gdk_ref,
         cosd_ref, sind_ref, cosc_ref, sinc_ref) = refs[:13]
        outs = refs[13:]
    else:
        x_ref, mod_ref, w_ref, gcq_ref, gckv_ref, wq_ref, wkv_ref, gdq_ref, gdk_ref = refs[:9]
        outs = refs[9:]
    qn_ref, qr_ref, ckv_ref, kn_ref, vc_ref, pe_ref, qd_ref, kd_ref, vd_ref = outs
    h = _modulate(x_ref, mod_ref, 0)

    def proj(a, b):
        return _dot(h, w_ref[:, a:b])

    cq = _rms(proj(0, 384), gcq_ref[...]).astype(BF16)
    qn_ref[...] = _dot(cq, wq_ref[:, 0:512])
    qr = _dot(cq, wq_ref[:, 512:768])
    ckv = _rms(proj(384, 640), gckv_ref[...])
    ckv_ref[...] = ckv
    ckv_bf = ckv.astype(BF16)
    kn_ref[...] = _dot(ckv_bf, wkv_ref[:, 0:512])
    vc_ref[...] = _dot(ckv_bf, wkv_ref[:, 512:1024])
    pe = proj(640, 768)
    qd = _rms_heads64(proj(768, 1280), gdq_ref[...])
    kd = _rms_heads64(proj(1280, 1536), gdk_ref[...])
    if rope:
        qr = _rope(qr, cosc_ref[...], sinc_ref[...], C_ROPE // 4)
        pe = _rope(pe, cosc_ref[...], sinc_ref[...], C_ROPE // 4)
        qd = _rope(qd, cosd_ref[...], sind_ref[...], HEAD_DIM // 4)
        kd = _rope(kd, cosd_ref[...], sind_ref[...], HEAD_DIM // 4)
    qr_ref[...] = qr
    pe_ref[...] = pe
    qd_ref[...] = qd
    kd_ref[...] = kd
    vd_ref[...] = proj(1536, 1792)


def _inproj_odd(x, mod, w_bf, gcq, gckv, wq_bf, wkv_bf, gdq, gdk, rope_tabs, rows_per_cond, tm=256):
    m_rows = x.shape[0]
    rope = rope_tabs is not None
    widths = (512, 256, 256, 512, 512, 128, 512, 256, 256)
    consts = [w_bf, gcq, gckv, wq_bf, wkv_bf, gdq, gdk]
    in_specs = [pl.BlockSpec((tm, D_MODEL), lambda i: (i, 0)),
                pl.BlockSpec((1, 6, D_MODEL), lambda i: (i * tm // rows_per_cond, 0, 0))]
    in_specs += [pl.BlockSpec(a.shape, lambda i: (0, 0)) for a in consts]
    args = [x, mod] + consts
    if rope:
        nblk = rope_tabs[0].shape[0] // tm
        in_specs += [pl.BlockSpec((tm, LANES), lambda i: (i % nblk, 0))] * 4
        args += list(rope_tabs)
    return pl.pallas_call(
        functools.partial(_inproj_odd_kernel, rope),
        grid=(m_rows // tm,),
        in_specs=in_specs,
        out_specs=[pl.BlockSpec((tm, w), lambda i: (i, 0)) for w in widths],
        out_shape=[jax.ShapeDtypeStruct((m_rows, w), F32) for w in widths],
        compiler_params=_cparams(1),
        name="inproj_odd",
    )(*args)


def _kvup_kernel(x_ref, w_ref, kn_ref, vc_ref):
    x = x_ref[...].astype(BF16)
    kn_ref[...] = _dot(x, w_ref[:, 0:512])
    vc_ref[...] = _dot(x, w_ref[:, 512:1024])


def _kvup(lat, wkv_bf, tm=512):
    m_rows = lat.shape[0]
    return pl.pallas_call(
        _kvup_kernel,
        grid=(m_rows // tm,),
        in_specs=[pl.BlockSpec((tm, C_KV_LORA), lambda i: (i, 0)),
                  pl.BlockSpec(wkv_bf.shape, lambda i: (0, 0))],
        out_specs=[pl.BlockSpec((tm, 512), lambda i: (i, 0))] * 2,
        out_shape=[jax.ShapeDtypeStruct((m_rows, 512), F32)] * 2,
        compiler_params=_cparams(1),
        name="kv_up_ctx",
    )(lat, wkv_bf)


def _gqa_kernel(tq, n_heads, n_kv, seq_k, band, has_ctx, has_sink, *refs):
    q_ref, k_ref, v_ref = refs[:3]
    idx = 3
    if has_ctx:
        kc_ref, vc_ref = refs[idx:idx + 2]
        idx += 2
    if has_sink:
        sink_ref = refs[idx]
        idx += 1
    o_ref = refs[idx]
    d = HEAD_DIM
    group = n_heads // n_kv
    scale = d ** -0.5
    if band:
        qi = pl.program_id(1)
        kb = tq + 2 * WINDOW
        start = pl.multiple_of(jnp.clip(qi * tq - WINDOW, 0, seq_k - kb), LANES)
        krows = pl.ds(start, kb)
        diff = ((qi * tq - start) + lax.broadcasted_iota(jnp.int32, (tq, kb), 0)
                - lax.broadcasted_iota(jnp.int32, (tq, kb), 1))
        bmask = jnp.abs(diff) <= WINDOW
    else:
        krows = slice(None)
    for g in range(n_kv):
        cols = slice(g * d, (g + 1) * d)
        kg = k_ref[krows, cols].astype(BF16)
        vg = v_ref[krows, cols].astype(BF16)
        if has_ctx:
            kcg = kc_ref[:, cols].astype(BF16)
            vcg = vc_ref[:, cols].astype(BF16)
        for hh in range(group):
            h = g * group + hh
            hcols = slice(h * d, (h + 1) * d)
            qh = (q_ref[:, hcols] * scale).astype(BF16)
            s = lax.dot_general(qh, kg, NT_DIMS, preferred_element_type=F32)
            if band:
                s = jnp.where(bmask, s, MASK_VALUE)
            m = jnp.max(s, axis=-1, keepdims=True)
            if has_ctx:
                sc = lax.dot_general(qh, kcg, NT_DIMS, preferred_element_type=F32)
                m = jnp.maximum(m, jnp.max(sc, axis=-1, keepdims=True))
            if has_sink:
                sk = sink_ref[:, h:h + 1]
                m = jnp.maximum(m, sk)
            p = jnp.exp(s - m)
            den = jnp.sum(p, axis=-1, keepdims=True)
            o = _dot(p.astype(BF16), vg)
            if has_ctx:
                pc = jnp.exp(sc - m)
                den = den + jnp.sum(pc, axis=-1, keepdims=True)
                o = o + _dot(pc.astype(BF16), vcg)
            if has_sink:
                den = den + jnp.exp(sk - m)
            o_ref[:, hcols] = o / den


def _gqa(q, k, v, n_batch, n_heads, n_kv, tq, band=False, ctx=None, sink=None):
    seq = q.shape[0] // n_batch
    nq = seq // tq
    wq, wk = n_heads * HEAD_DIM, n_kv * HEAD_DIM
    in_specs = [pl.BlockSpec((tq, wq), lambda b, i: (b * nq + i, 0)),
                pl.BlockSpec((seq, wk), lambda b, i: (b, 0)),
                pl.BlockSpec((seq, wk), lambda b, i: (b, 0))]
    args = [q, k, v]
    if ctx is not None:
        kc, vc, layer = ctx
        spec = pl.BlockSpec((None, None, kc.shape[2], wk), lambda b, i: (b, layer, 0, 0))
        in_specs += [spec, spec]
        args += [kc, vc]
    if sink is not None:
        in_specs.append(pl.BlockSpec((1, n_heads), lambda b, i: (0, 0)))
        args.append(sink.reshape(1, n_heads))
    return pl.pallas_call(
        functools.partial(_gqa_kernel, tq, n_heads, n_kv, seq, band, ctx is not None, sink is not None),
        grid=(n_batch, nq),
        in_specs=in_specs,
        out_specs=pl.BlockSpec((tq, wq), lambda b, i: (b * nq + i, 0)),
        out_shape=jax.ShapeDtypeStruct((q.shape[0], wq), F32),
        compiler_params=_cparams(2),
        name="gqa_attention",
    )(*args)


def _mla_kernel(has_ctx, *refs):
    qn_ref, qr_ref, kn_ref, vc_ref, pe_ref = refs[:5]
    idx = 5
    if has_ctx:
        knc_ref, vcc_ref, pec_ref = refs[idx:idx + 3]
        idx += 3
    o_ref = refs[idx]
    scale = (C_NOPE + C_ROPE) ** -0.5
    pe = pe_ref[:, 0:C_ROPE].astype(BF16)
    if has_ctx:
        pec = pec_ref[...].astype(BF16)
    for h in range(C_HEADS):
        ncols = slice(h * C_NOPE, (h + 1) * C_NOPE)
        qn = qn_ref[:, ncols].astype(BF16)
        qr = qr_ref[:, h * C_ROPE:(h + 1) * C_ROPE].astype(BF16)
        s = (lax.dot_general(qn, kn_ref[:, ncols].astype(BF16), NT_DIMS, preferred_element_type=F32)
             + lax.dot_general(qr, pe, NT_DIMS, preferred_element_type=F32)) * scale
        m = jnp.max(s, axis=-1, keepdims=True)
        if has_ctx:
            sc = (lax.dot_general(qn, knc_ref[:, ncols].astype(BF16), NT_DIMS, preferred_element_type=F32)
                  + lax.dot_general(qr, pec, NT_DIMS, preferred_element_type=F32)) * scale
            m = jnp.maximum(m, jnp.max(sc, axis=-1, keepdims=True))
        p = jnp.exp(s - m)
        den = jnp.sum(p, axis=-1, keepdims=True)
        o = _dot(p.astype(BF16), vc_ref[:, ncols].astype(BF16))
        if has_ctx:
            pc = jnp.exp(sc - m)
            den = den + jnp.sum(pc, axis=-1, keepdims=True)
            o = o + _dot(pc.astype(BF16), vcc_ref[:, ncols].astype(BF16))
        o_ref[:, ncols] = o / den


def _mla(qn, qr, kn, vc, pe, n_batch, tq, ctx=None):
    seq = qn.shape[0] // n_batch
    nq = seq // tq
    in_specs = [pl.BlockSpec((tq, 512), lambda b, i: (b * nq + i, 0)),
                pl.BlockSpec((tq, 256), lambda b, i: (b * nq + i, 0)),
                pl.BlockSpec((seq, 512), lambda b, i: (b, 0)),
                pl.BlockSpec((seq, 512), lambda b, i: (b, 0)),
                pl.BlockSpec((seq, LANES), lambda b, i: (b, 0))]
    args = [qn, qr, kn, vc, pe]
    if ctx is not None:
        knc, vcc, pec, layer = ctx
        past = pec.shape[2]
        in_specs += [pl.BlockSpec((past, 512), lambda b, i: (b, 0)),
                     pl.BlockSpec((past, 512), lambda b, i: (b, 0)),
                     pl.BlockSpec((None, None, past, C_ROPE), lambda b, i: (b, layer, 0, 0))]
        args += [knc, vcc, pec]
    return pl.pallas_call(
        functools.partial(_mla_kernel, ctx is not None),
        grid=(n_batch, nq),
        in_specs=in_specs,
        out_specs=pl.BlockSpec((tq, 512), lambda b, i: (b * nq + i, 0)),
        out_shape=jax.ShapeDtypeStruct((qn.shape[0], 512), F32),
        compiler_params=_cparams(2),
        name="mla_attention",
    )(*args)


def _hgrn_consts():
    i = np.arange(HGRN_GROUP)
    same = (i[:, None] // HGRN_CHUNK) == (i[None, :] // HGRN_CHUNK)
    tri_f = same & (i[None, :] <= i[:, None])
    tri_b = same & (i[None, :] >= i[:, None])
    return (jnp.asarray(tri_f, F32), jnp.asarray(tri_b, F32), jnp.asarray(same, F32))


def _hgrn_kernel(seq, has_s0, want_state, *refs):
    (q_ref, kf_ref, lf_ref, kb_ref, lbw_ref, v_ref, gb_ref, gn_ref,
     trif_ref, trib_ref, blk_ref) = refs[:11]
    idx = 11
    if has_s0:
        s0_ref = refs[idx]
        idx += 1
    o_ref = refs[idx]
    idx += 1
    if want_state:
        sfin_ref = refs[idx]
        idx += 1
    oacc, qt, ut, dec, st = refs[idx:idx + 5]
    n_groups = seq // HGRN_GROUP
    n_chunks = seq // HGRN_CHUNK
    per_group = HGRN_GROUP // HGRN_CHUNK
    scale = B_DK ** -0.5
    blk = blk_ref[...]
    tpos = lax.broadcasted_iota(jnp.int32, (HGRN_CHUNK, LANES), 0)

    for direction, (k_ref, l_ref, tri_ref) in enumerate(((kf_ref, lf_ref, trif_ref),
                                                        (kb_ref, lbw_ref, trib_ref))):
        rev = direction == 1
        tri = tri_ref[...]

        def intra(g, carry, k_ref=k_ref, l_ref=l_ref, tri=tri, rev=rev, direction=direction):
            rows = pl.ds(pl.multiple_of(g * HGRN_GROUP, HGRN_GROUP), HGRN_GROUP)
            q = q_ref[rows, :] * scale
            k = k_ref[rows, :]
            logf = l_ref[rows, :]
            v = v_ref[rows, :]
            cum = jnp.dot(tri, logf, precision=HIGHEST, preferred_element_type=F32)
            tot = jnp.dot(blk, logf, precision=HIGHEST, preferred_element_type=F32)
            qt[rows, :] = q * jnp.exp(cum)
            kt = (k * jnp.exp(tot - cum)).astype(BF16)
            o_chunks = []
            for c in range(per_group):
                sl = slice(c * HGRN_CHUNK, (c + 1) * HGRN_CHUNK)
                qc, kc, vc, cc = q[sl], k[sl], v[sl], cum[sl]
                oc = jnp.zeros((HGRN_CHUNK, LANES), F32)
                for s in range(HGRN_CHUNK):
                    live = (tpos <= s) if rev else (tpos >= s)
                    e = jnp.exp(jnp.where(live, cc - cc[s:s + 1], MASK_VALUE))
                    a = jnp.sum((qc * kc[s:s + 1]) * e, axis=-1, keepdims=True)
                    oc = oc + a * vc[s:s + 1]
                o_chunks.append(oc)
                n = g * per_group + c
                ut[n] = lax.dot_general(vc.astype(BF16), kt[sl], TN_DIMS, preferred_element_type=F32)
                dec[pl.ds(n, 1), :] = jnp.exp(tot[c * HGRN_CHUNK:c * HGRN_CHUNK + 1])
            o_new = jnp.concatenate(o_chunks, axis=0)
            if direction == 0:
                oacc[rows, :] = o_new
            else:
                oacc[rows, :] += o_new
            return carry

        lax.fori_loop(0, n_groups, intra, 0)

        if has_s0:
            st[...] = s0_ref[direction].T
        else:
            st[...] = jnp.zeros((B_DV, B_DK), F32)

        def inter(i, carry, rev=rev):
            n = (n_chunks - 1 - i) if rev else i
            rows = pl.ds(pl.multiple_of(n * HGRN_CHUNK, HGRN_CHUNK), HGRN_CHUNK)
            s_t = st[...]
            oacc[rows, :] += lax.dot_general(qt[rows, :].astype(BF16), s_t.astype(BF16), NT_DIMS,
                                             preferred_element_type=F32)
            st[...] = s_t * dec[pl.ds(n, 1), :] + ut[n]
            return carry

        lax.fori_loop(0, n_chunks, inter, 0)
        if want_state:
            sfin_ref[direction] = st[...].T

    o = oacc[...]
    o_ref[...] = _rms(o, gn_ref[...]) * gb_ref[...]


def _hgrn(q, kf, lf, kb, lbw, v, gb, gnorm, n_batch, s0=None, want_state=False):
    seq = q.shape[0] // n_batch
    n_chunks = seq // HGRN_CHUNK
    tok = pl.BlockSpec((seq, LANES), lambda b, h: (b, h))
    const = pl.BlockSpec((HGRN_GROUP, HGRN_GROUP), lambda b, h: (0, 0))
    in_specs = [tok] * 7 + [pl.BlockSpec((1, LANES), lambda b, h: (0, 0))] + [const] * 3
    args = [q, kf, lf, kb, lbw, v, gb, gnorm.reshape(1, LANES)] + list(_hgrn_consts())
    if s0 is not None:
        state, layer = s0
        in_specs.append(pl.BlockSpec((None, None, 2, None, B_DK, B_DV), lambda b, h: (b, layer, 0, h, 0, 0)))
        args.append(state)
    out_specs = [tok]
    out_shape = [jax.ShapeDtypeStruct(q.shape, F32)]
    if want_state:
        out_specs.append(pl.BlockSpec((None, 2, None, B_DK, B_DV), lambda b, h: (b, 0, h, 0, 0)))
        out_shape.append(jax.ShapeDtypeStruct((n_batch, 2, B_HEADS, B_DK, B_DV), F32))
    res = pl.pallas_call(
        functools.partial(_hgrn_kernel, seq, s0 is not None, want_state),
        grid=(n_batch, B_HEADS),
        in_specs=in_specs,
        out_specs=out_specs,
        out_shape=out_shape,
        scratch_shapes=[pltpu.VMEM((seq, LANES), F32),
                        pltpu.VMEM((seq, LANES), F32),
                        pltpu.VMEM((n_chunks, B_DV, B_DK), F32),
                        pltpu.VMEM((n_chunks, LANES), F32),
                        pltpu.VMEM((B_DV, B_DK), F32)],
        compiler_params=_cparams(2),
        name="hgrn2",
    )(*args)
    return res if want_state else (res[0], None)


def _outffn_kernel(o1_ref, o2_ref, x_ref, mod_ref, wo_ref, ln_ref, wg_ref, wu_ref, wd_ref, out_ref):
    m = mod_ref[0]
    half = o1_ref.shape[1]
    y = (_dot(o1_ref[...].astype(BF16), wo_ref[0:half, :])
         + _dot(o2_ref[...].astype(BF16), wo_ref[half:2 * half, :]))
    x1 = _layer_norm(ALPHA * x_ref[...] + m[2:3] * y, ln_ref[0:1, :], ln_ref[1:2, :])
    h = (x1 * (1.0 + m[4:5]) + m[3:4]).astype(BF16)
    acc = None
    for a in range(0, D_FF, 512):
        b = min(a + 512, D_FF)
        act = (_silu(_dot(h, wg_ref[:, a:b])) * _dot(h, wu_ref[:, a:b])).astype(BF16)
        part = _dot(act, wd_ref[a:b, :])
        acc = part if acc is None else acc + part
    out_ref[...] = _layer_norm(ALPHA * x1 + m[5:6] * acc, ln_ref[2:3, :], ln_ref[3:4, :])


def _outffn(o1, o2, x, mod, wo_bf, ln, wg_bf, wu_bf, wd_bf, rows_per_cond, tm=512):
    m_rows = x.shape[0]
    half = o1.shape[1]
    consts = [wo_bf, ln, wg_bf, wu_bf, wd_bf]
    in_specs = [pl.BlockSpec((tm, half), lambda i: (i, 0)),
                pl.BlockSpec((tm, half), lambda i: (i, 0)),
                pl.BlockSpec((tm, D_MODEL), lambda i: (i, 0)),
                pl.BlockSpec((1, 6, D_MODEL), lambda i: (i * tm // rows_per_cond, 0, 0))]
    in_specs += [pl.BlockSpec(a.shape, lambda i: (0, 0)) for a in consts]
    return pl.pallas_call(
        _outffn_kernel,
        grid=(m_rows // tm,),
        in_specs=in_specs,
        out_specs=pl.BlockSpec((tm, D_MODEL), lambda i: (i, 0)),
        out_shape=jax.ShapeDtypeStruct((m_rows, D_MODEL), F32),
        compiler_params=_cparams(1),
        name="outproj_ffn",
    )(o1, o2, x, mod, *consts)


def _even_layer(xp, xs, modp, mods, w_in_bf, lb, sink, gnorm, tail_w, rope_hd,
                cache_k, cache_v, state, e, nbp, nbs):
    seq_p = xp.shape[0] // nbp
    seq_s = xs.shape[0] // nbs
    pp = _inproj_even(xp, modp, w_in_bf, lb, None, xp.shape[0])
    ps = _inproj_even(xs, mods, w_in_bf, lb, rope_hd, seq_s)
    qa, ka, va, qb, ib, kf, lf, kb, lbw, gb = pp
    oa_p = _gqa(qa, ka, va, nbp, A_HEADS, A_KV, seq_p, sink=sink)
    ob_p, s_new = _hgrn(qb, kf, lf, kb, lbw, ib, gb, gnorm, nbp, want_state=True)
    xp = _outffn(oa_p, ob_p, xp, modp, *tail_w, xp.shape[0])
    new = (ka.reshape(nbp, seq_p, A_KV, HEAD_DIM), va.reshape(nbp, seq_p, A_KV, HEAD_DIM), s_new)
    qa, ka, va, qb, ib, kf, lf, kb, lbw, gb = ps
    oa_s = _gqa(qa, ka, va, nbs, A_HEADS, A_KV, 256, band=True, ctx=(cache_k, cache_v, e), sink=sink)
    ob_s, _ = _hgrn(qb, kf, lf, kb, lbw, ib, gb, gnorm, nbs, s0=(state, e))
    xs = _outffn(oa_s, ob_s, xs, mods, *tail_w, seq_s)
    return xp, xs, new


def _odd_layer(xp, xs, modp, mods, in_w, tail_w, rope_hd, rope_c,
               cache_ckv, cache_pe, cache_k, cache_v, o, nbp, nbs):
    seq_p = xp.shape[0] // nbp
    seq_s = xs.shape[0] // nbs
    wkv_bf = in_w[4]
    qn, qr, ckv, kn, vc, pe, qd, kd, vd = _inproj_odd(xp, modp, *in_w, None, xp.shape[0])
    oc_p = _mla(qn, qr, kn, vc, pe, nbp, seq_p)
    od_p = _gqa(qd, kd, vd, nbp, D_HEADS, D_KV, seq_p)
    xp = _outffn(oc_p, od_p, xp, modp, *tail_w, xp.shape[0])
    new = (ckv.reshape(nbp, seq_p, C_KV_LORA), pe[:, :C_ROPE].reshape(nbp, seq_p, C_ROPE),
           kd.reshape(nbp, seq_p, D_KV, HEAD_DIM), vd.reshape(nbp, seq_p, D_KV, HEAD_DIM))
    qn, qr, ckv, kn, vc, pe, qd, kd, vd = _inproj_odd(xs, mods, *in_w, rope_hd + rope_c, seq_s)
    past = cache_ckv.shape[2]
    knc, vcc = _kvup(cache_ckv[:, o].reshape(nbs * past, C_KV_LORA), wkv_bf)
    oc_s = _mla(qn, qr, kn, vc, pe, nbs, 256, ctx=(knc, vcc, cache_pe, o))
    od_s = _gqa(qd, kd, vd, nbs, D_HEADS, D_KV, 256, ctx=(cache_k, cache_v, o))
    xs = _outffn(oc_s, od_s, xs, mods, *tail_w, seq_s)
    return xp, xs, new


def kernel(x_prompt, x_sample, cache_a_k, cache_a_v, state_b, cache_c_kv, cache_c_pe, cache_d_k, cache_d_v, c, c_ctx, w_ada, b_ada, ln_g, ln_b, w_in_ab, a_sink, b_lb, b_gnorm, w_in_cd, c_q_norm, c_kv_norm, c_w_q_up, c_w_kv_up, d_q_norm, d_k_norm, w_out, w_ffn_gate, w_ffn_up, w_ffn_down):
    nbp, seq_p, _ = x_prompt.shape
    nbs, seq_s, _ = x_sample.shape
    past = cache_a_k.shape[2]
    xp = x_prompt.reshape(nbp * seq_p, D_MODEL)
    xs = x_sample.reshape(nbs * seq_s, D_MODEL)
    rope_hd = _rope_tables(seq_s, HEAD_DIM)
    rope_c = _rope_tables(seq_s, C_ROPE)
    lb_w = jax.nn.softmax(b_lb.astype(F32), axis=0)
    lb_all = jnp.cumsum(lb_w, axis=0) - lb_w[:1]
    mod_all = _modulation(c, c_ctx, w_ada, b_ada)
    n_even = cache_a_k.shape[1]
    n_odd = cache_c_kv.shape[1]
    ca_k = cache_a_k.reshape(nbs, n_even, past, A_KV * HEAD_DIM)
    ca_v = cache_a_v.reshape(nbs, n_even, past, A_KV * HEAD_DIM)
    cd_k = cache_d_k.reshape(nbs, n_odd, past, D_KV * HEAD_DIM)
    cd_v = cache_d_v.reshape(nbs, n_odd, past, D_KV * HEAD_DIM)
    ak, av, sb, ckv, cpe, dk, dv = [], [], [], [], [], [], []
    for l in range(DEPTH):
        modp = mod_all[l, 0:1]
        mods = mod_all[l, 1:1 + nbs]
        ln = jnp.concatenate([ln_g[l, 0:1], ln_b[l, 0:1], ln_g[l, 1:2], ln_b[l, 1:2]], axis=0)
        tail_w = (w_out[l].astype(BF16), ln, w_ffn_gate[l].astype(BF16), w_ffn_up[l].astype(BF16),
                  w_ffn_down[l].astype(BF16))
        if l % 2 == 0:
            e = l // 2
            xp, xs, (k_new, v_new, s_new) = _even_layer(
                xp, xs, modp, mods, w_in_ab[e].astype(BF16), lb_all[e], a_sink[e], b_gnorm[e], tail_w,
                rope_hd, ca_k, ca_v, state_b, e, nbp, nbs)
            ak.append(k_new)
            av.append(v_new)
            sb.append(s_new)
        else:
            o = l // 2
            w_in = w_in_cd[o]
            split = C_Q_LORA + C_KV_LORA + C_ROPE
            w_in = jnp.concatenate([w_in[:, :split], jnp.zeros((D_MODEL, LANES - C_ROPE), F32), w_in[:, split:]],
                                   axis=1).astype(BF16)
            wq = c_w_q_up[o].reshape(C_Q_LORA, C_HEADS, C_NOPE + C_ROPE)
            wq = jnp.concatenate([wq[:, :, :C_NOPE].reshape(C_Q_LORA, -1), wq[:, :, C_NOPE:].reshape(C_Q_LORA, -1)],
                                 axis=1).astype(BF16)
            wkv = c_w_kv_up[o].reshape(C_KV_LORA, C_HEADS, C_NOPE + C_V)
            wkv = jnp.concatenate([wkv[:, :, :C_NOPE].reshape(C_KV_LORA, -1), wkv[:, :, C_NOPE:].reshape(C_KV_LORA, -1)],
                                  axis=1).astype(BF16)
            in_w = (w_in, c_q_norm[o].reshape(1, -1), c_kv_norm[o].reshape(1, -1), wq, wkv,
                    jnp.tile(d_q_norm[o], D_HEADS).reshape(1, -1), jnp.tile(d_k_norm[o], D_KV).reshape(1, -1))
            xp, xs, (c_new, pe_new, k_new, v_new) = _odd_layer(
                xp, xs, modp, mods, in_w, tail_w, rope_hd, rope_c, cache_c_kv, cache_c_pe, cd_k, cd_v, o, nbp, nbs)
            ckv.append(c_new)
            cpe.append(pe_new)
            dk.append(k_new)
            dv.append(v_new)
    return (xp.reshape(nbp, seq_p, D_MODEL), xs.reshape(nbs, seq_s, D_MODEL),
            jnp.stack(ak, axis=1), jnp.stack(av, axis=1), jnp.stack(sb, axis=1),
            jnp.stack(ckv, axis=1), jnp.stack(cpe, axis=1), jnp.stack(dk, axis=1), jnp.stack(dv, axis=1))
```

```python
import functools

import numpy as np
import jax
import jax.numpy as jnp
from jax import lax
from jax.experimental import pallas as pl
from jax.experimental.pallas import tpu as pltpu

F32 = jnp.float32
BF16 = jnp.bfloat16
HIGHEST = lax.Precision.HIGHEST

D_MODEL = 1024
DEPTH = 4
GRID_W = 64
HEAD_DIM = 64
ROPE_BASE = 10000.0
MASK_VALUE = -1e30
F_FLOOR = 1e-30
A_HEADS = 8
A_KV = 2
WINDOW = 128
B_HEADS = 4
B_DK = 128
B_DV = 128
C_HEADS = 8
C_Q_LORA = 384
C_KV_LORA = 256
C_NOPE = 64
C_ROPE = 32
C_V = 64
D_HEADS = 8
D_KV = 4
D_FF = 2816
ALPHA = (2 * DEPTH) ** 0.25

LANES = 128
HGRN_GROUP = 128
LOG2E = 1.4426950408889634
VMEM_LIMIT = 56 * 1024 * 1024
NT_DIMS = (((1,), (1,)), ((), ()))
TN_DIMS = (((0,), (0,)), ((), ()))


def _cparams(n_axes):
    return pltpu.CompilerParams(dimension_semantics=("arbitrary",) * n_axes,
                                vmem_limit_bytes=VMEM_LIMIT)


def _silu(x):
    return x * jax.nn.sigmoid(x)


def _dot(a, b):
    return jnp.dot(a, b, preferred_element_type=F32)


def _layer_norm(x, g, b):
    mu = jnp.mean(x, axis=-1, keepdims=True)
    xc = x - mu
    var = jnp.mean(xc * xc, axis=-1, keepdims=True)
    return xc * lax.rsqrt(var + 1e-5) * g + b


def _rms(x, g):
    return x * lax.rsqrt(jnp.mean(x * x, axis=-1, keepdims=True) + 1e-6) * g


def _rms_heads64(x, g):
    outs = []
    for j in range(x.shape[1] // LANES):
        xg = x[:, j * LANES:(j + 1) * LANES]
        x2 = xg * xg
        lo = lax.broadcasted_iota(jnp.int32, xg.shape, 1) < HEAD_DIM
        s_lo = jnp.sum(jnp.where(lo, x2, 0.0), axis=-1, keepdims=True)
        s_hi = jnp.sum(jnp.where(lo, 0.0, x2), axis=-1, keepdims=True)
        ms = jnp.where(lo, s_lo, s_hi) * (1.0 / HEAD_DIM)
        outs.append(xg * lax.rsqrt(ms + 1e-6))
    return jnp.concatenate(outs, axis=1) * g


def _rope(x, cos, sin, half):
    outs = []
    for j in range(x.shape[1] // LANES):
        xg = x[:, j * LANES:(j + 1) * LANES]
        lane = lax.broadcasted_iota(jnp.int32, xg.shape, 1)
        up = pltpu.roll(xg, LANES - half, 1)
        down = pltpu.roll(xg, half, 1)
        rot = jnp.where((lane & (2 * half - 1)) < half, up, down)
        outs.append(xg * cos + rot * sin)
    return outs[0] if len(outs) == 1 else jnp.concatenate(outs, axis=1)


def _rope_tables(n_tokens, rot_dim):
    n_rows = n_tokens // GRID_W
    row = jnp.broadcast_to(jnp.arange(n_rows, dtype=F32)[:, None], (n_rows, GRID_W)).reshape(-1)
    col = jnp.broadcast_to(jnp.arange(GRID_W, dtype=F32)[None, :], (n_rows, GRID_W)).reshape(-1)
    quarter = rot_dim // 4
    inv = ROPE_BASE ** (-jnp.arange(quarter, dtype=F32) / quarter)
    ar = row[:, None] * inv
    ac = col[:, None] * inv
    cos = jnp.concatenate([jnp.cos(ar), jnp.cos(ar), jnp.cos(ac), jnp.cos(ac)], axis=-1)
    sin = jnp.concatenate([-jnp.sin(ar), jnp.sin(ar), -jnp.sin(ac), jnp.sin(ac)], axis=-1)
    reps = LANES // rot_dim
    return jnp.tile(cos, (1, reps)), jnp.tile(sin, (1, reps))


def _mod_kernel(c_ref, w_ref, b_ref, o_ref):
    a = _silu(c_ref[...]).astype(BF16)
    o_ref[0] = _dot(a, w_ref[0].astype(BF16)) + b_ref[0]


def _modulation(c, c_ctx, w_ada, b_ada):
    nb = c.shape[0]
    cond = jnp.zeros((16, D_MODEL), F32).at[0].set(c_ctx).at[1:1 + nb].set(c)
    tn = 1536
    out = pl.pallas_call(
        _mod_kernel,
        grid=(DEPTH, 6 * D_MODEL // tn),
        in_specs=[pl.BlockSpec((16, D_MODEL), lambda l, j: (0, 0)),
                  pl.BlockSpec((1, D_MODEL, tn), lambda l, j: (l, 0, j)),
                  pl.BlockSpec((1, 1, tn), lambda l, j: (l, 0, j))],
        out_specs=pl.BlockSpec((1, 16, tn), lambda l, j: (l, 0, j)),
        out_shape=jax.ShapeDtypeStruct((DEPTH, 16, 6 * D_MODEL), F32),
        compiler_params=_cparams(2),
        name="modulation",
    )(cond, w_ada, b_ada.reshape(DEPTH, 1, 6 * D_MODEL))
    return out.reshape(DEPTH, 16, 6, D_MODEL)


def _modulate(x_ref, mod_ref, shift_row):
    m = mod_ref[0]
    return (x_ref[...] * (1.0 + m[shift_row + 1:shift_row + 2]) + m[shift_row:shift_row + 1]).astype(BF16)


def _inproj_even_kernel(rope, *refs):
    if rope:
        x_ref, mod_ref, w_ref, lb_ref, cos_ref, sin_ref = refs[:6]
        outs = refs[6:]
    else:
        x_ref, mod_ref, w_ref, lb_ref = refs[:4]
        outs = refs[4:]
    qa_ref, ka_ref, va_ref, qb_ref, ib_ref, kf_ref, lf_ref, kb_ref, lbw_ref, gb_ref = outs
    h = _modulate(x_ref, mod_ref, 0)

    def proj(a, b):
        return _dot(h, w_ref[:, a:b])

    qa = proj(0, 512)
    ka = proj(512, 640)
    if rope:
        cos = cos_ref[...]
        sin = sin_ref[...]
        qa = _rope(qa, cos, sin, HEAD_DIM // 4)
        ka = _rope(ka, cos, sin, HEAD_DIM // 4)
    qa_ref[...] = qa
    ka_ref[...] = ka
    va_ref[...] = proj(640, 768)
    qb_ref[...] = _silu(proj(768, 1280))
    ib_ref[...] = proj(1280, 1792)
    for d, (k_ref, l_ref) in enumerate(((kf_ref, lf_ref), (kb_ref, lbw_ref))):
        lb = lb_ref[d:d + 1, :]
        f = lb + (1.0 - lb) * jax.nn.sigmoid(proj(1792 + 512 * d, 2304 + 512 * d))
        l_ref[...] = jnp.log(jnp.maximum(f, F_FLOOR))
        k_ref[...] = 1.0 - f
    gb_ref[...] = _silu(proj(2816, 3328))


def _inproj_even(x, mod, w_bf, lb, rope_tabs, rows_per_cond, tm=256):
    m_rows = x.shape[0]
    rope = rope_tabs is not None
    widths = (512, 128, 128, 512, 512, 512, 512, 512, 512, 512)
    in_specs = [pl.BlockSpec((tm, D_MODEL), lambda i: (i, 0)),
                pl.BlockSpec((1, 6, D_MODEL), lambda i: (i * tm // rows_per_cond, 0, 0)),
                pl.BlockSpec(w_bf.shape, lambda i: (0, 0)),
                pl.BlockSpec(lb.shape, lambda i: (0, 0))]
    args = [x, mod, w_bf, lb]
    if rope:
        nblk = rope_tabs[0].shape[0] // tm
        in_specs += [pl.BlockSpec((tm, LANES), lambda i: (i % nblk, 0))] * 2
        args += list(rope_tabs)
    return pl.pallas_call(
        functools.partial(_inproj_even_kernel, rope),
        grid=(m_rows // tm,),
        in_specs=in_specs,
        out_specs=[pl.BlockSpec((tm, w), lambda i: (i, 0)) for w in widths],
        out_shape=[jax.ShapeDtypeStruct((m_rows, w), F32) for w in widths],
        compiler_params=_cparams(1),
        name="inproj_even",
    )(*args)


def _inproj_odd_kernel(rope, *refs):
    if rope:
        (x_ref, mod_ref, w_ref, gcq_ref, gckv_ref, wq_ref, wkv_ref, gdq_ref, gdk_ref,
         cosd_ref, sind_ref, cosc_ref, sinc_ref) = refs[:13]
        outs = refs[13:]
    else:
        x_ref, mod_ref, w_ref, gcq_ref, gckv_ref, wq_ref, wkv_ref, gdq_ref, gdk_ref = refs[:9]
        outs = refs[9:]
    qn_ref, qr_ref, ckv_ref, kn_ref, vc_ref, pe_ref, qd_ref, kd_ref, vd_ref = outs
    h = _modulate(x_ref, mod_ref, 0)

    def proj(a, b):
        return _dot(h, w_ref[:, a:b])

    cq = _rms(proj(0, 384), gcq_ref[...]).astype(BF16)
    qn_ref[...] = _dot(cq, wq_ref[:, 0:512])
    qr = _dot(cq, wq_ref[:, 512:768])
    ckv = _rms(proj(384, 640), gckv_ref[...])
    ckv_ref[...] = ckv
    ckv_bf = ckv.astype(BF16)
    kn_ref[...] = _dot(ckv_bf, wkv_ref[:, 0:512])
    vc_ref[...] = _dot(ckv_bf, wkv_ref[:, 512:1024])
    pe = proj(640, 768)
    qd = _rms_heads64(proj(768, 1280), gdq_ref[...])
    kd = _rms_heads64(proj(1280, 1536), gdk_ref[...])
    if rope:
        qr = _rope(qr, cosc_ref[...], sinc_ref[...], C_ROPE // 4)
        pe = _rope(pe, cosc_ref[...], sinc_ref[...], C_ROPE // 4)
        qd = _rope(qd, cosd_ref[...], sind_ref[...], HEAD_DIM // 4)
        kd = _rope(kd, cosd_ref[...], sind_ref[...], HEAD_DIM // 4)
    qr_ref[...] = qr
    pe_ref[...] = pe
    qd_ref[...] = qd
    kd_ref[...] = kd
    vd_ref[...] = proj(1536, 1792)


def _inproj_odd(x, mod, w_bf, gcq, gckv, wq_bf, wkv_bf, gdq, gdk, rope_tabs, rows_per_cond, tm=256):
    m_rows = x.shape[0]
    rope = rope_tabs is not None
    widths = (512, 256, 256, 512, 512, 128, 512, 256, 256)
    consts = [w_bf, gcq, gckv, wq_bf, wkv_bf, gdq, gdk]
    in_specs = [pl.BlockSpec((tm, D_MODEL), lambda i: (i, 0)),
                pl.BlockSpec((1, 6, D_MODEL), lambda i: (i * tm // rows_per_cond, 0, 0))]
    in_specs += [pl.BlockSpec(a.shape, lambda i: (0, 0)) for a in consts]
    args = [x, mod] + consts
    if rope:
        nblk = rope_tabs[0].shape[0] // tm
        in_specs += [pl.BlockSpec((tm, LANES), lambda i: (i % nblk, 0))] * 4
        args += list(rope_tabs)
    return pl.pallas_call(
        functools.partial(_inproj_odd_kernel, rope),
        grid=(m_rows // tm,),
        in_specs=in_specs,
        out_specs=[pl.BlockSpec((tm, w), lambda i: (i, 0)) for w in widths],
        out_shape=[jax.ShapeDtypeStruct((m_rows, w), F32) for w in widths],
        compiler_params=_cparams(1),
        name="inproj_odd",
    )(*args)


def _kvup_kernel(x_ref, w_ref, kn_ref, vc_ref):
    x = x_ref[...].astype(BF16)
    kn_ref[...] = _dot(x, w_ref[:, 0:512])
    vc_ref[...] = _dot(x, w_ref[:, 512:1024])


def _kvup(lat, wkv_bf, tm=512):
    m_rows = lat.shape[0]
    return pl.pallas_call(
        _kvup_kernel,
        grid=(m_rows // tm,),
        in_specs=[pl.BlockSpec((tm, C_KV_LORA), lambda i: (i, 0)),
                  pl.BlockSpec(wkv_bf.shape, lambda i: (0, 0))],
        out_specs=[pl.BlockSpec((tm, 512), lambda i: (i, 0))] * 2,
        out_shape=[jax.ShapeDtypeStruct((m_rows, 512), F32)] * 2,
        compiler_params=_cparams(1),
        name="kv_up_ctx",
    )(lat, wkv_bf)


def _gqa_kernel(tq, n_heads, n_kv, seq_k, band, has_ctx, has_sink, *refs):
    q_ref, k_ref, v_ref = refs[:3]
    idx = 3
    if has_ctx:
        kc_ref, vc_ref = refs[idx:idx + 2]
        idx += 2
    if has_sink:
        sink_ref = refs[idx]
        idx += 1
    o_ref = refs[idx]
    d = HEAD_DIM
    group = n_heads // n_kv
    scale = d ** -0.5
    if band:
        qi = pl.program_id(1)
        kb = tq + 2 * WINDOW
        start = pl.multiple_of(jnp.clip(qi * tq - WINDOW, 0, seq_k - kb), LANES)
        krows = pl.ds(start, kb)
        diff = ((qi * tq - start) + lax.broadcasted_iota(jnp.int32, (tq, kb), 0)
                - lax.broadcasted_iota(jnp.int32, (tq, kb), 1))
        bmask = jnp.abs(diff) <= WINDOW
    else:
        krows = slice(None)
    for g in range(n_kv):
        cols = slice(g * d, (g + 1) * d)
        kg = k_ref[krows, cols].astype(BF16)
        vg = v_ref[krows, cols].astype(BF16)
        if has_ctx:
            kcg = kc_ref[:, cols].astype(BF16)
            vcg = vc_ref[:, cols].astype(BF16)
        for hh in range(group):
            h = g * group + hh
            hcols = slice(h * d, (h + 1) * d)
            qh = (q_ref[:, hcols] * scale).astype(BF16)
            s = lax.dot_general(qh, kg, NT_DIMS, preferred_element_type=F32)
            if band:
                s = jnp.where(bmask, s, MASK_VALUE)
            m = jnp.max(s, axis=-1, keepdims=True)
            if has_ctx:
                sc = lax.dot_general(qh, kcg, NT_DIMS, preferred_element_type=F32)
                m = jnp.maximum(m, jnp.max(sc, axis=-1, keepdims=True))
            if has_sink:
                sk = sink_ref[h]
                m = jnp.maximum(m, sk)
            p = jnp.exp(s - m)
            den = jnp.sum(p, axis=-1, keepdims=True)
            o = _dot(p.astype(BF16), vg)
            if has_ctx:
                pc = jnp.exp(sc - m)
                den = den + jnp.sum(pc, axis=-1, keepdims=True)
                o = o + _dot(pc.astype(BF16), vcg)
            if has_sink:
                den = den + jnp.exp(sk - m)
            o_ref[:, hcols] = o / den


def _gqa(q, k, v, n_batch, n_heads, n_kv, tq, band=False, ctx=None, sink=None):
    seq = q.shape[0] // n_batch
    nq = seq // tq
    wq, wk = n_heads * HEAD_DIM, n_kv * HEAD_DIM
    in_specs = [pl.BlockSpec((tq, wq), lambda b, i: (b * nq + i, 0)),
                pl.BlockSpec((seq, wk), lambda b, i: (b, 0)),
                pl.BlockSpec((seq, wk), lambda b, i: (b, 0))]
    args = [q, k, v]
    if ctx is not None:
        kc, vc, layer = ctx
        spec = pl.BlockSpec((None, None, kc.shape[2], wk), lambda b, i: (b, layer, 0, 0))
        in_specs += [spec, spec]
        args += [kc, vc]
    if sink is not None:
        in_specs.append(pl.BlockSpec(memory_space=pltpu.SMEM))
        args.append(sink)
    return pl.pallas_call(
        functools.partial(_gqa_kernel, tq, n_heads, n_kv, seq, band, ctx is not None, sink is not None),
        grid=(n_batch, nq),
        in_specs=in_specs,
        out_specs=pl.BlockSpec((tq, wq), lambda b, i: (b * nq + i, 0)),
        out_shape=jax.ShapeDtypeStruct((q.shape[0], wq), F32),
        compiler_params=_cparams(2),
        name="gqa_attention",
    )(*args)


def _mla_kernel(has_ctx, *refs):
    qn_ref, qr_ref, kn_ref, vc_ref, pe_ref = refs[:5]
    idx = 5
    if has_ctx:
        knc_ref, vcc_ref, pec_ref = refs[idx:idx + 3]
        idx += 3
    o_ref = refs[idx]
    scale = (C_NOPE + C_ROPE) ** -0.5
    pe = pe_ref[:, 0:C_ROPE].astype(BF16)
    if has_ctx:
        pec = pec_ref[...].astype(BF16)
    for h in range(C_HEADS):
        ncols = slice(h * C_NOPE, (h + 1) * C_NOPE)
        qn = qn_ref[:, ncols].astype(BF16)
        qr = qr_ref[:, h * C_ROPE:(h + 1) * C_ROPE].astype(BF16)
        s = (lax.dot_general(qn, kn_ref[:, ncols].astype(BF16), NT_DIMS, preferred_element_type=F32)
             + lax.dot_general(qr, pe, NT_DIMS, preferred_element_type=F32)) * scale
        m = jnp.max(s, axis=-1, keepdims=True)
        if has_ctx:
            sc = (lax.dot_general(qn, knc_ref[:, ncols].astype(BF16), NT_DIMS, preferred_element_type=F32)
                  + lax.dot_general(qr, pec, NT_DIMS, preferred_element_type=F32)) * scale
            m = jnp.maximum(m, jnp.max(sc, axis=-1, keepdims=True))
        p = jnp.exp(s - m)
        den = jnp.sum(p, axis=-1, keepdims=True)
        o = _dot(p.astype(BF16), vc_ref[:, ncols].astype(BF16))
        if has_ctx:
            pc = jnp.exp(sc - m)
            den = den + jnp.sum(pc, axis=-1, keepdims=True)
            o = o + _dot(pc.astype(BF16), vcc_ref[:, ncols].astype(BF16))
        o_ref[:, ncols] = o / den


def _mla(qn, qr, kn, vc, pe, n_batch, tq, ctx=None):
    seq = qn.shape[0] // n_batch
    nq = seq // tq
    in_specs = [pl.BlockSpec((tq, 512), lambda b, i: (b * nq + i, 0)),
                pl.BlockSpec((tq, 256), lambda b, i: (b * nq + i, 0)),
                pl.BlockSpec((seq, 512), lambda b, i: (b, 0)),
                pl.BlockSpec((seq, 512), lambda b, i: (b, 0)),
                pl.BlockSpec((seq, LANES), lambda b, i: (b, 0))]
    args = [qn, qr, kn, vc, pe]
    if ctx is not None:
        knc, vcc, pec, layer = ctx
        past = pec.shape[2]
        in_specs += [pl.BlockSpec((past, 512), lambda b, i: (b, 0)),
                     pl.BlockSpec((past, 512), lambda b, i: (b, 0)),
                     pl.BlockSpec((None, None, past, C_ROPE), lambda b, i: (b, layer, 0, 0))]
        args += [knc, vcc, pec]
    return pl.pallas_call(
        functools.partial(_mla_kernel, ctx is not None),
        grid=(n_batch, nq),
        in_specs=in_specs,
        out_specs=pl.BlockSpec((tq, 512), lambda b, i: (b * nq + i, 0)),
        out_shape=jax.ShapeDtypeStruct((qn.shape[0], 512), F32),
        compiler_params=_cparams(2),
        name="mla_attention",
    )(*args)


def _hgrn_consts():
    i = np.arange(HGRN_GROUP)
    t, s = i[:, None], i[None, :]
    level = np.where(t != s, np.floor(np.log2(np.maximum(t ^ s, 1))) + 1, 0).astype(np.int32)
    return (jnp.asarray(s <= t, F32), jnp.asarray(s >= t, F32),
            jnp.asarray(np.where(s <= t, level, -1), jnp.int32),
            jnp.asarray(np.where(s >= t, level, -1), jnp.int32))


def _level_refs(c, level, rev):
    n = c.shape[0]
    half = 1 << (level - 1)
    bs = 2 * half
    off = half if rev else half - 1
    if bs >= 16:
        return jnp.concatenate([jnp.broadcast_to(c[i * bs + off:i * bs + off + 1, :], (bs, LANES))
                                for i in range(n // bs)], axis=0)
    c3 = c.reshape(n // 8, 8, LANES)
    sub = lax.broadcasted_iota(jnp.int32, c3.shape, 1)
    out = None
    for j in reversed(range(8 // bs)):
        b = jnp.broadcast_to(c3[:, j * bs + off:j * bs + off + 1, :], c3.shape)
        out = b if out is None else jnp.where(sub < (j + 1) * bs, b, out)
    return out.reshape(n, LANES)


def _neg_abs(x):
    return lax.bitcast_convert_type(lax.bitcast_convert_type(x, jnp.uint32) | jnp.uint32(0x80000000), F32)


def _hgrn_kernel(seq, has_s0, want_state, *refs):
    (q_ref, kf_ref, lf_ref, kb_ref, lbw_ref, v_ref, gb_ref, gn_ref,
     trif_ref, trib_ref, lvf_ref, lvb_ref) = refs[:12]
    idx = 12
    if has_s0:
        s0_ref = refs[idx]
        idx += 1
    o_ref = refs[idx]
    idx += 1
    if want_state:
        sfin_ref = refs[idx]
        idx += 1
    of_scr, ob_scr, stf, stb = refs[idx:idx + 4]
    n_groups = seq // HGRN_GROUP
    scale = B_DK ** -0.5
    n_levels = HGRN_GROUP.bit_length() - 1
    dirs = ((kf_ref, lf_ref, trif_ref, lvf_ref, of_scr, stf, False),
            (kb_ref, lbw_ref, trib_ref, lvb_ref, ob_scr, stb, True))

    for d, st in enumerate((stf, stb)):
        st[...] = s0_ref[d].T if has_s0 else jnp.zeros((B_DV, B_DK), F32)

    def nt(a, b):
        return lax.dot_general(a.astype(BF16), b.astype(BF16), NT_DIMS, preferred_element_type=F32)

    def group_step(i, carry):
        for k_ref, l_ref, tri_ref, lv_ref, o_scr, st, rev in dirs:
            g = (n_groups - 1 - i) if rev else i
            rows = pl.ds(pl.multiple_of(g * HGRN_GROUP, HGRN_GROUP), HGRN_GROUP)
            q = q_ref[rows, :] * scale
            k = k_ref[rows, :]
            v_bf = v_ref[rows, :].astype(BF16)
            c = jnp.dot(tri_ref[...], l_ref[rows, :], precision=HIGHEST, preferred_element_type=F32) * LOG2E
            tot = c[0:1] if rev else c[HGRN_GROUP - 1:HGRN_GROUP]
            lv = lv_ref[...]
            attn = jnp.where(lv == 0, nt(q, k), 0.0)
            for level in range(1, n_levels + 1):
                e = jnp.exp2(_neg_abs(c - _level_refs(c, level, rev)))
                attn = jnp.where(lv == level, nt(q * e, k * e), attn)
            s_t = st[...]
            o_scr[rows, :] = _dot(attn.astype(BF16), v_bf) + nt(q * jnp.exp2(c), s_t)
            kt = (k * jnp.exp2(tot - c)).astype(BF16)
            st[...] = s_t * jnp.exp2(tot) + lax.dot_general(v_bf, kt, TN_DIMS, preferred_element_type=F32)
        return carry

    lax.fori_loop(0, n_groups, group_step, 0)
    if want_state:
        sfin_ref[0] = stf[...].T
        sfin_ref[1] = stb[...].T
    o_ref[...] = _rms(of_scr[...] + ob_scr[...], gn_ref[...]) * gb_ref[...]


def _hgrn(q, kf, lf, kb, lbw, v, gb, gnorm, n_batch, s0=None, want_state=False):
    seq = q.shape[0] // n_batch
    tok = pl.BlockSpec((seq, LANES), lambda b, h: (b, h))
    const = pl.BlockSpec((HGRN_GROUP, HGRN_GROUP), lambda b, h: (0, 0))
    in_specs = [tok] * 7 + [pl.BlockSpec((1, LANES), lambda b, h: (0, 0))] + [const] * 4
    args = [q, kf, lf, kb, lbw, v, gb, gnorm.reshape(1, LANES)] + list(_hgrn_consts())
    if s0 is not None:
        state, layer = s0
        in_specs.append(pl.BlockSpec((None, None, 2, None, B_DK, B_DV), lambda b, h: (b, layer, 0, h, 0, 0)))
        args.append(state)
    out_specs = [tok]
    out_shape = [jax.ShapeDtypeStruct(q.shape, F32)]
    if want_state:
        out_specs.append(pl.BlockSpec((None, 2, None, B_DK, B_DV), lambda b, h: (b, 0, h, 0, 0)))
        out_shape.append(jax.ShapeDtypeStruct((n_batch, 2, B_HEADS, B_DK, B_DV), F32))
    res = pl.pallas_call(
        functools.partial(_hgrn_kernel, seq, s0 is not None, want_state),
        grid=(n_batch, B_HEADS),
        in_specs=in_specs,
        out_specs=out_specs,
        out_shape=out_shape,
        scratch_shapes=[pltpu.VMEM((seq, LANES), F32),
                        pltpu.VMEM((seq, LANES), F32),
                        pltpu.VMEM((B_DV, B_DK), F32),
                        pltpu.VMEM((B_DV, B_DK), F32)],
        compiler_params=_cparams(2),
        name="hgrn2",
    )(*args)
    return res if want_state else (res[0], None)


def _outffn_kernel(o1_ref, o2_ref, x_ref, mod_ref, wo_ref, ln_ref, wg_ref, wu_ref, wd_ref, out_ref):
    m = mod_ref[0]
    half = o1_ref.shape[1]
    y = (_dot(o1_ref[...].astype(BF16), wo_ref[0:half, :])
         + _dot(o2_ref[...].astype(BF16), wo_ref[half:2 * half, :]))
    x1 = _layer_norm(ALPHA * x_ref[...] + m[2:3] * y, ln_ref[0:1, :], ln_ref[1:2, :])
    h = (x1 * (1.0 + m[4:5]) + m[3:4]).astype(BF16)
    acc = None
    for a in range(0, D_FF, 512):
        b = min(a + 512, D_FF)
        act = (_silu(_dot(h, wg_ref[:, a:b])) * _dot(h, wu_ref[:, a:b])).astype(BF16)
        part = _dot(act, wd_ref[a:b, :])
        acc = part if acc is None else acc + part
    out_ref[...] = _layer_norm(ALPHA * x1 + m[5:6] * acc, ln_ref[2:3, :], ln_ref[3:4, :])


def _outffn(o1, o2, x, mod, wo_bf, ln, wg_bf, wu_bf, wd_bf, rows_per_cond, tm=512):
    m_rows = x.shape[0]
    half = o1.shape[1]
    consts = [wo_bf, ln, wg_bf, wu_bf, wd_bf]
    in_specs = [pl.BlockSpec((tm, half), lambda i: (i, 0)),
                pl.BlockSpec((tm, half), lambda i: (i, 0)),
                pl.BlockSpec((tm, D_MODEL), lambda i: (i, 0)),
                pl.BlockSpec((1, 6, D_MODEL), lambda i: (i * tm // rows_per_cond, 0, 0))]
    in_specs += [pl.BlockSpec(a.shape, lambda i: (0, 0)) for a in consts]
    return pl.pallas_call(
        _outffn_kernel,
        grid=(m_rows // tm,),
        in_specs=in_specs,
        out_specs=pl.BlockSpec((tm, D_MODEL), lambda i: (i, 0)),
        out_shape=jax.ShapeDtypeStruct((m_rows, D_MODEL), F32),
        compiler_params=_cparams(1),
        name="outproj_ffn",
    )(o1, o2, x, mod, *consts)


def _even_layer(xp, xs, modp, mods, w_in_bf, lb, sink, gnorm, tail_w, rope_hd,
                cache_k, cache_v, state, e, nbp, nbs):
    seq_p = xp.shape[0] // nbp
    seq_s = xs.shape[0] // nbs
    pp = _inproj_even(xp, modp, w_in_bf, lb, None, xp.shape[0])
    ps = _inproj_even(xs, mods, w_in_bf, lb, rope_hd, seq_s)
    qa, ka, va, qb, ib, kf, lf, kb, lbw, gb = pp
    oa_p = _gqa(qa, ka, va, nbp, A_HEADS, A_KV, seq_p, sink=sink)
    ob_p, s_new = _hgrn(qb, kf, lf, kb, lbw, ib, gb, gnorm, nbp, want_state=True)
    xp = _outffn(oa_p, ob_p, xp, modp, *tail_w, xp.shape[0])
    new = (ka.reshape(nbp, seq_p, A_KV, HEAD_DIM), va.reshape(nbp, seq_p, A_KV, HEAD_DIM), s_new)
    qa, ka, va, qb, ib, kf, lf, kb, lbw, gb = ps
    oa_s = _gqa(qa, ka, va, nbs, A_HEADS, A_KV, 256, band=True, ctx=(cache_k, cache_v, e), sink=sink)
    ob_s, _ = _hgrn(qb, kf, lf, kb, lbw, ib, gb, gnorm, nbs, s0=(state, e))
    xs = _outffn(oa_s, ob_s, xs, mods, *tail_w, seq_s)
    return xp, xs, new


def _odd_layer(xp, xs, modp, mods, in_w, tail_w, rope_hd, rope_c,
               cache_ckv, cache_pe, cache_k, cache_v, o, nbp, nbs):
    seq_p = xp.shape[0] // nbp
    seq_s = xs.shape[0] // nbs
    wkv_bf = in_w[4]
    qn, qr, ckv, kn, vc, pe, qd, kd, vd = _inproj_odd(xp, modp, *in_w, None, xp.shape[0])
    oc_p = _mla(qn, qr, kn, vc, pe, nbp, seq_p)
    od_p = _gqa(qd, kd, vd, nbp, D_HEADS, D_KV, seq_p)
    xp = _outffn(oc_p, od_p, xp, modp, *tail_w, xp.shape[0])
    new = (ckv.reshape(nbp, seq_p, C_KV_LORA), pe[:, :C_ROPE].reshape(nbp, seq_p, C_ROPE),
           kd.reshape(nbp, seq_p, D_KV, HEAD_DIM), vd.reshape(nbp, seq_p, D_KV, HEAD_DIM))
    qn, qr, ckv, kn, vc, pe, qd, kd, vd = _inproj_odd(xs, mods, *in_w, rope_hd + rope_c, seq_s)
    past = cache_ckv.shape[2]
    knc, vcc = _kvup(cache_ckv[:, o].reshape(nbs * past, C_KV_LORA), wkv_bf)
    oc_s = _mla(qn, qr, kn, vc, pe, nbs, 256, ctx=(knc, vcc, cache_pe, o))
    od_s = _gqa(qd, kd, vd, nbs, D_HEADS, D_KV, 256, ctx=(cache_k, cache_v, o))
    xs = _outffn(oc_s, od_s, xs, mods, *tail_w, seq_s)
    return xp, xs, new


def kernel(x_prompt, x_sample, cache_a_k, cache_a_v, state_b, cache_c_kv, cache_c_pe, cache_d_k, cache_d_v, c, c_ctx, w_ada, b_ada, ln_g, ln_b, w_in_ab, a_sink, b_lb, b_gnorm, w_in_cd, c_q_norm, c_kv_norm, c_w_q_up, c_w_kv_up, d_q_norm, d_k_norm, w_out, w_ffn_gate, w_ffn_up, w_ffn_down):
    nbp, seq_p, _ = x_prompt.shape
    nbs, seq_s, _ = x_sample.shape
    past = cache_a_k.shape[2]
    xp = x_prompt.reshape(nbp * seq_p, D_MODEL)
    xs = x_sample.reshape(nbs * seq_s, D_MODEL)
    rope_hd = _rope_tables(seq_s, HEAD_DIM)
    rope_c = _rope_tables(seq_s, C_ROPE)
    lb_w = jax.nn.softmax(b_lb.astype(F32), axis=0)
    lb_all = jnp.cumsum(lb_w, axis=0) - lb_w[:1]
    mod_all = _modulation(c, c_ctx, w_ada, b_ada)
    n_even = cache_a_k.shape[1]
    n_odd = cache_c_kv.shape[1]
    ca_k = cache_a_k.reshape(nbs, n_even, past, A_KV * HEAD_DIM)
    ca_v = cache_a_v.reshape(nbs, n_even, past, A_KV * HEAD_DIM)
    cd_k = cache_d_k.reshape(nbs, n_odd, past, D_KV * HEAD_DIM)
    cd_v = cache_d_v.reshape(nbs, n_odd, past, D_KV * HEAD_DIM)
    ak, av, sb, ckv, cpe, dk, dv = [], [], [], [], [], [], []
    for l in range(DEPTH):
        modp = mod_all[l, 0:1]
        mods = mod_all[l, 1:1 + nbs]
        ln = jnp.concatenate([ln_g[l, 0:1], ln_b[l, 0:1], ln_g[l, 1:2], ln_b[l, 1:2]], axis=0)
        tail_w = (w_out[l].astype(BF16), ln, w_ffn_gate[l].astype(BF16), w_ffn_up[l].astype(BF16),
                  w_ffn_down[l].astype(BF16))
        if l % 2 == 0:
            e = l // 2
            xp, xs, (k_new, v_new, s_new) = _even_layer(
                xp, xs, modp, mods, w_in_ab[e].astype(BF16), lb_all[e], a_sink[e], b_gnorm[e], tail_w,
                rope_hd, ca_k, ca_v, state_b, e, nbp, nbs)
            ak.append(k_new)
            av.append(v_new)
            sb.append(s_new)
        else:
            o = l // 2
            w_in = w_in_cd[o]
            split = C_Q_LORA + C_KV_LORA + C_ROPE
            w_in = jnp.concatenate([w_in[:, :split], jnp.zeros((D_MODEL, LANES - C_ROPE), F32), w_in[:, split:]],
                                   axis=1).astype(BF16)
            wq = c_w_q_up[o].reshape(C_Q_LORA, C_HEADS, C_NOPE + C_ROPE)
            wq = jnp.concatenate([wq[:, :, :C_NOPE].reshape(C_Q_LORA, -1), wq[:, :, C_NOPE:].reshape(C_Q_LORA, -1)],
                                 axis=1).astype(BF16)
            wkv = c_w_kv_up[o].reshape(C_KV_LORA, C_HEADS, C_NOPE + C_V)
            wkv = jnp.concatenate([wkv[:, :, :C_NOPE].reshape(C_KV_LORA, -1), wkv[:, :, C_NOPE:].reshape(C_KV_LORA, -1)],
                                  axis=1).astype(BF16)
            in_w = (w_in, c_q_norm[o].reshape(1, -1), c_kv_norm[o].reshape(1, -1), wq, wkv,
                    jnp.tile(d_q_norm[o], D_HEADS).reshape(1, -1), jnp.tile(d_k_norm[o], D_KV).reshape(1, -1))
            xp, xs, (c_new, pe_new, k_new, v_new) = _odd_layer(
                xp, xs, modp, mods, in_w, tail_w, rope_hd, rope_c, cache_c_kv, cache_c_pe, cd_k, cd_v, o, nbp, nbs)
            ckv.append(c_new)
            cpe.append(pe_new)
            dk.append(k_new)
            dv.append(v_new)
    return (xp.reshape(nbp, seq_p, D_MODEL), xs.reshape(nbs, seq_s, D_MODEL),
            jnp.stack(ak, axis=1), jnp.stack(av, axis=1), jnp.stack(sb, axis=1),
            jnp.stack(ckv, axis=1), jnp.stack(cpe, axis=1), jnp.stack(dk, axis=1), jnp.stack(dv, axis=1))
```

```python
import functools

import numpy as np
import jax
import jax.numpy as jnp
from jax import lax
from jax.experimental import pallas as pl
from jax.experimental.pallas import tpu as pltpu

F32 = jnp.float32
BF16 = jnp.bfloat16
HIGHEST = lax.Precision.HIGHEST

D_MODEL = 1024
DEPTH = 4
GRID_W = 64
HEAD_DIM = 64
ROPE_BASE = 10000.0
MASK_VALUE = -1e30
F_FLOOR = 1e-30
A_HEADS = 8
A_KV = 2
WINDOW = 128
B_HEADS = 4
B_DK = 128
B_DV = 128
C_HEADS = 8
C_Q_LORA = 384
C_KV_LORA = 256
C_NOPE = 64
C_ROPE = 32
C_V = 64
D_HEADS = 8
D_KV = 4
D_FF = 2816
ALPHA = (2 * DEPTH) ** 0.25

LANES = 128
HGRN_GROUP = 128
LOG2E = 1.4426950408889634
SAMPLE_TQ = 512
BAND_TQ = 256
VMEM_LIMIT = 56 * 1024 * 1024
NT_DIMS = (((1,), (1,)), ((), ()))
TN_DIMS = (((0,), (0,)), ((), ()))


def _cparams(n_axes):
    return pltpu.CompilerParams(dimension_semantics=("arbitrary",) * n_axes,
                                vmem_limit_bytes=VMEM_LIMIT)


def _silu(x):
    return x * jax.nn.sigmoid(x)


def _dot(a, b):
    return jnp.dot(a, b, preferred_element_type=F32)


def _layer_norm(x, g, b):
    mu = jnp.mean(x, axis=-1, keepdims=True)
    xc = x - mu
    var = jnp.mean(xc * xc, axis=-1, keepdims=True)
    return xc * lax.rsqrt(var + 1e-5) * g + b


def _rms(x, g):
    return x * lax.rsqrt(jnp.mean(x * x, axis=-1, keepdims=True) + 1e-6) * g


def _rms_heads64(x, g):
    outs = []
    for j in range(x.shape[1] // LANES):
        xg = x[:, j * LANES:(j + 1) * LANES]
        x2 = xg * xg
        lo = lax.broadcasted_iota(jnp.int32, xg.shape, 1) < HEAD_DIM
        s_lo = jnp.sum(jnp.where(lo, x2, 0.0), axis=-1, keepdims=True)
        s_hi = jnp.sum(jnp.where(lo, 0.0, x2), axis=-1, keepdims=True)
        ms = jnp.where(lo, s_lo, s_hi) * (1.0 / HEAD_DIM)
        outs.append(xg * lax.rsqrt(ms + 1e-6))
    return jnp.concatenate(outs, axis=1) * g


def _rope(x, cos, sin, half):
    outs = []
    for j in range(x.shape[1] // LANES):
        xg = x[:, j * LANES:(j + 1) * LANES]
        lane = lax.broadcasted_iota(jnp.int32, xg.shape, 1)
        up = pltpu.roll(xg, LANES - half, 1)
        down = pltpu.roll(xg, half, 1)
        rot = jnp.where((lane & (2 * half - 1)) < half, up, down)
        outs.append(xg * cos + rot * sin)
    return outs[0] if len(outs) == 1 else jnp.concatenate(outs, axis=1)


def _rope_tables(n_tokens, rot_dim):
    n_rows = n_tokens // GRID_W
    row = jnp.broadcast_to(jnp.arange(n_rows, dtype=F32)[:, None], (n_rows, GRID_W)).reshape(-1)
    col = jnp.broadcast_to(jnp.arange(GRID_W, dtype=F32)[None, :], (n_rows, GRID_W)).reshape(-1)
    quarter = rot_dim // 4
    inv = ROPE_BASE ** (-jnp.arange(quarter, dtype=F32) / quarter)
    ar = row[:, None] * inv
    ac = col[:, None] * inv
    cos = jnp.concatenate([jnp.cos(ar), jnp.cos(ar), jnp.cos(ac), jnp.cos(ac)], axis=-1)
    sin = jnp.concatenate([-jnp.sin(ar), jnp.sin(ar), -jnp.sin(ac), jnp.sin(ac)], axis=-1)
    reps = LANES // rot_dim
    return jnp.tile(cos, (1, reps)), jnp.tile(sin, (1, reps))


def _cast_kernel(x_ref, o_ref):
    o_ref[...] = x_ref[...].astype(BF16)


def _to_bf16(w, layer, tm=256):
    _, rows, cols = w.shape
    return pl.pallas_call(
        _cast_kernel,
        grid=(rows // tm,),
        in_specs=[pl.BlockSpec((None, tm, cols), lambda i: (layer, i, 0))],
        out_specs=pl.BlockSpec((tm, cols), lambda i: (i, 0)),
        out_shape=jax.ShapeDtypeStruct((rows, cols), BF16),
        compiler_params=_cparams(1),
        name="to_bf16",
    )(w)


def _mod_kernel(c_ref, w_ref, b_ref, o_ref):
    a = _silu(c_ref[...]).astype(BF16)
    o_ref[0] = _dot(a, w_ref[0].astype(BF16)) + b_ref[0]


def _modulation(c, c_ctx, w_ada, b_ada):
    nb = c.shape[0]
    cond = jnp.zeros((16, D_MODEL), F32).at[0].set(c_ctx).at[1:1 + nb].set(c)
    tn = 1536
    out = pl.pallas_call(
        _mod_kernel,
        grid=(DEPTH, 6 * D_MODEL // tn),
        in_specs=[pl.BlockSpec((16, D_MODEL), lambda l, j: (0, 0)),
                  pl.BlockSpec((1, D_MODEL, tn), lambda l, j: (l, 0, j)),
                  pl.BlockSpec((1, 1, tn), lambda l, j: (l, 0, j))],
        out_specs=pl.BlockSpec((1, 16, tn), lambda l, j: (l, 0, j)),
        out_shape=jax.ShapeDtypeStruct((DEPTH, 16, 6 * D_MODEL), F32),
        compiler_params=_cparams(2),
        name="modulation",
    )(cond, w_ada, b_ada.reshape(DEPTH, 1, 6 * D_MODEL))
    return out.reshape(DEPTH, 16, 6, D_MODEL)


def _modulate(x_ref, mod_ref, shift_row):
    m = mod_ref[0]
    return (x_ref[...] * (1.0 + m[shift_row + 1:shift_row + 2]) + m[shift_row:shift_row + 1]).astype(BF16)


def _inproj_even_kernel(rope, *refs):
    if rope:
        x_ref, mod_ref, w_ref, lb_ref, cos_ref, sin_ref = refs[:6]
        outs = refs[6:]
    else:
        x_ref, mod_ref, w_ref, lb_ref = refs[:4]
        outs = refs[4:]
    qa_ref, ka_ref, va_ref, qb_ref, ib_ref, kf_ref, lf_ref, kb_ref, lbw_ref, gb_ref = outs
    h = _modulate(x_ref, mod_ref, 0)

    def proj(a, b):
        return _dot(h, w_ref[:, a:b])

    qa = proj(0, 512)
    ka = proj(512, 640)
    if rope:
        cos = cos_ref[...]
        sin = sin_ref[...]
        qa = _rope(qa, cos, sin, HEAD_DIM // 4)
        ka = _rope(ka, cos, sin, HEAD_DIM // 4)
    qa_ref[...] = qa
    ka_ref[...] = ka
    va_ref[...] = proj(640, 768)
    qb_ref[...] = _silu(proj(768, 1280))
    ib_ref[...] = proj(1280, 1792)
    for d, (k_ref, l_ref) in enumerate(((kf_ref, lf_ref), (kb_ref, lbw_ref))):
        lb = lb_ref[d:d + 1, :]
        f = lb + (1.0 - lb) * jax.nn.sigmoid(proj(1792 + 512 * d, 2304 + 512 * d))
        l_ref[...] = jnp.log(jnp.maximum(f, F_FLOOR))
        k_ref[...] = 1.0 - f
    gb_ref[...] = _silu(proj(2816, 3328))


def _inproj_even(x, mod, w_bf, lb, rope_tabs, rows_per_cond, tm=256):
    m_rows = x.shape[0]
    rope = rope_tabs is not None
    widths = (512, 128, 128, 512, 512, 512, 512, 512, 512, 512)
    in_specs = [pl.BlockSpec((tm, D_MODEL), lambda i: (i, 0)),
                pl.BlockSpec((1, 6, D_MODEL), lambda i: (i * tm // rows_per_cond, 0, 0)),
                pl.BlockSpec(w_bf.shape, lambda i: (0, 0)),
                pl.BlockSpec(lb.shape, lambda i: (0, 0))]
    args = [x, mod, w_bf, lb]
    if rope:
        nblk = rope_tabs[0].shape[0] // tm
        in_specs += [pl.BlockSpec((tm, LANES), lambda i: (i % nblk, 0))] * 2
        args += list(rope_tabs)
    return pl.pallas_call(
        functools.partial(_inproj_even_kernel, rope),
        grid=(m_rows // tm,),
        in_specs=in_specs,
        out_specs=[pl.BlockSpec((tm, w), lambda i: (i, 0)) for w in widths],
        out_shape=[jax.ShapeDtypeStruct((m_rows, w), F32) for w in widths],
        compiler_params=_cparams(1),
        name="inproj_even",
    )(*args)


def _inproj_odd_kernel(rope, *refs):
    if rope:
        (x_ref, mod_ref, w_ref, gcq_ref, gckv_ref, wq_ref, wkv_ref, gdq_ref, gdk_ref,
         cosd_ref, sind_ref, cosc_ref, sinc_ref) = refs[:13]
        outs = refs[13:]
    else:
        x_ref, mod_ref, w_ref, gcq_ref, gckv_ref, wq_ref, wkv_ref, gdq_ref, gdk_ref = refs[:9]
        outs = refs[9:]
    qc_ref, ckv_ref, kc_ref, vc_ref, pe_ref, qd_ref, kd_ref, vd_ref = outs
    h = _modulate(x_ref, mod_ref, 0)

    def proj(a, b):
        return _dot(h, w_ref[:, a:b])

    cq = _rms(proj(0, 384), gcq_ref[...]).astype(BF16)
    qc = _dot(cq, wq_ref[...])
    ckv = _rms(proj(384, 640), gckv_ref[...])
    ckv_ref[...] = ckv
    ckv_bf = ckv.astype(BF16)
    kn = _dot(ckv_bf, wkv_ref[:, 0:C_HEADS * LANES])
    vc_ref[...] = _dot(ckv_bf, wkv_ref[:, C_HEADS * LANES:]).astype(BF16)
    pe = proj(640, 768)
    qd = _rms_heads64(proj(768, 1280), gdq_ref[...])
    kd = _rms_heads64(proj(1280, 1536), gdk_ref[...])
    if rope:
        qc = _rope(qc, cosc_ref[...], sinc_ref[...], C_ROPE // 4)
        pe = _rope(pe, cosc_ref[...], sinc_ref[...], C_ROPE // 4)
        qd = _rope(qd, cosd_ref[...], sind_ref[...], HEAD_DIM // 4)
        kd = _rope(kd, cosd_ref[...], sind_ref[...], HEAD_DIM // 4)
    qc_ref[...] = qc.astype(BF16)
    pe_ref[...] = pe
    kc_ref[...] = jnp.concatenate([kn[:, j * LANES:(j + 1) * LANES] + pe for j in range(C_HEADS)],
                                  axis=1).astype(BF16)
    qd_ref[...] = qd
    kd_ref[...] = kd
    vd_ref[...] = proj(1536, 1792)


def _inproj_odd(x, mod, w_bf, gcq, gckv, wq_bf, wkv_bf, gdq, gdk, rope_tabs, rows_per_cond, tm=512):
    m_rows = x.shape[0]
    rope = rope_tabs is not None
    outs = ((C_HEADS * LANES, BF16), (C_KV_LORA, F32), (C_HEADS * LANES, BF16), (C_HEADS * C_V, BF16),
            (LANES, F32), (D_HEADS * HEAD_DIM, F32), (D_KV * HEAD_DIM, F32), (D_KV * HEAD_DIM, F32))
    consts = [w_bf, gcq, gckv, wq_bf, wkv_bf, gdq, gdk]
    in_specs = [pl.BlockSpec((tm, D_MODEL), lambda i: (i, 0)),
                pl.BlockSpec((1, 6, D_MODEL), lambda i: (i * tm // rows_per_cond, 0, 0))]
    in_specs += [pl.BlockSpec(a.shape, lambda i: (0, 0)) for a in consts]
    args = [x, mod] + consts
    if rope:
        nblk = rope_tabs[0].shape[0] // tm
        in_specs += [pl.BlockSpec((tm, LANES), lambda i: (i % nblk, 0))] * 4
        args += list(rope_tabs)
    return pl.pallas_call(
        functools.partial(_inproj_odd_kernel, rope),
        grid=(m_rows // tm,),
        in_specs=in_specs,
        out_specs=[pl.BlockSpec((tm, w), lambda i: (i, 0)) for w, _ in outs],
        out_shape=[jax.ShapeDtypeStruct((m_rows, w), dt) for w, dt in outs],
        compiler_params=_cparams(1),
        name="inproj_odd",
    )(*args)


def _kvup_kernel(x_ref, pe_ref, w_ref, kc_ref, vc_ref):
    x = x_ref[...].astype(BF16)
    kn = _dot(x, w_ref[:, 0:C_HEADS * LANES])
    pe = pe_ref[...]
    kc_ref[...] = jnp.concatenate([kn[:, j * LANES:(j + 1) * LANES] + pe for j in range(C_HEADS)],
                                  axis=1).astype(BF16)
    vc_ref[...] = _dot(x, w_ref[:, C_HEADS * LANES:]).astype(BF16)


def _kvup(lat, pe_slab, wkv_bf, tm=512):
    m_rows = lat.shape[0]
    return pl.pallas_call(
        _kvup_kernel,
        grid=(m_rows // tm,),
        in_specs=[pl.BlockSpec((tm, C_KV_LORA), lambda i: (i, 0)),
                  pl.BlockSpec((tm, LANES), lambda i: (i, 0)),
                  pl.BlockSpec(wkv_bf.shape, lambda i: (0, 0))],
        out_specs=[pl.BlockSpec((tm, C_HEADS * LANES), lambda i: (i, 0)),
                   pl.BlockSpec((tm, C_HEADS * C_V), lambda i: (i, 0))],
        out_shape=[jax.ShapeDtypeStruct((m_rows, C_HEADS * LANES), BF16),
                   jax.ShapeDtypeStruct((m_rows, C_HEADS * C_V), BF16)],
        compiler_params=_cparams(1),
        name="kv_up_ctx",
    )(lat, pe_slab, wkv_bf)


def _attn_kernel(tq, n_heads, group, seq_k, past, band, has_sink, wide, score_mul, *refs):
    q_ref, k_ref, v_ref = refs[:3]
    idx = 3
    if past:
        kc_ref, vc_ref = refs[idx:idx + 2]
        idx += 2
    if has_sink:
        sink_ref = refs[idx]
        idx += 1
    o_ref, kpad, vaug = refs[idx:idx + 3]
    d = HEAD_DIM
    n_kv = n_heads // group
    total = seq_k + past
    qi = pl.program_id(1)

    def _prepare_keys_values():
        parts = [(k_ref, v_ref, 0, seq_k)]
        if past:
            parts.append((kc_ref, vc_ref, seq_k, past))
        for ks_ref, vs_ref, r0, n in parts:
            rows = slice(r0, r0 + n)
            lo = lax.broadcasted_iota(jnp.int32, (n, LANES), 1) < d
            for j in range(n_kv * d // LANES):
                cols = slice(j * LANES, (j + 1) * LANES)
                vs = vs_ref[:, cols].astype(F32)
                if wide:
                    vaug[2 * j, rows, :] = jnp.where(lo, vs, 1.0).astype(BF16)
                    vaug[2 * j + 1, rows, :] = jnp.where(lo, 1.0, vs).astype(BF16)
                    continue
                vr = pltpu.roll(vs, d, 1)
                ks = ks_ref[:, cols]
                kr = pltpu.roll(ks, d, 1)
                for g, (k_even, k_odd, v_even, v_odd) in ((2 * j, (ks, kr, vs, vr)), (2 * j + 1, (kr, ks, vr, vs))):
                    kpad[2 * g, rows, :] = jnp.where(lo, k_even, 0.0).astype(BF16)
                    kpad[2 * g + 1, rows, :] = jnp.where(lo, 0.0, k_odd).astype(BF16)
                    vaug[2 * g, rows, :] = jnp.where(lo, v_even, 1.0).astype(BF16)
                    vaug[2 * g + 1, rows, :] = jnp.where(lo, 1.0, v_odd).astype(BF16)
            if wide:
                for h in range(n_heads):
                    kpad[h, rows, :] = ks_ref[:, h * LANES:(h + 1) * LANES]

    if seq_k == tq:
        _prepare_keys_values()
    else:
        pl.when(qi == 0)(_prepare_keys_values)

    if band:
        kb = tq + 2 * WINDOW
        start = pl.multiple_of(jnp.clip(qi * tq - WINDOW, 0, seq_k - kb), LANES)
        diff = ((qi * tq - start) + lax.broadcasted_iota(jnp.int32, (tq, kb), 0)
                - lax.broadcasted_iota(jnp.int32, (tq, kb), 1))
        bmask = jnp.abs(diff) <= WINDOW
        key_rows = [pl.ds(start, kb), pl.ds(seq_k, past)]
    else:
        key_rows = [slice(0, total)]

    def nt(a, b):
        return lax.dot_general(a, b, NT_DIMS, preferred_element_type=F32)

    lo_out = lax.broadcasted_iota(jnp.int32, (tq, LANES), 1) < d
    q_slabs = {}
    results = {}
    for g in range(n_kv):
        for parity in (0, 1):
            heads = [h for h in range(g * group, (g + 1) * group) if h % 2 == parity]
            if not heads:
                continue
            probs = [[] for _ in key_rows]
            maxes = []
            for h in heads:
                slot = h if wide else 2 * g + parity
                qs = h if wide else h // 2
                if qs not in q_slabs:
                    q = q_ref[:, qs * LANES:(qs + 1) * LANES]
                    q_slabs[qs] = q if wide else (q * d ** -0.5).astype(BF16)
                scores = [nt(q_slabs[qs], kpad[slot, r, :]) for r in key_rows]
                if score_mul is not None:
                    scores = [s * score_mul for s in scores]
                if band:
                    scores[0] = jnp.where(bmask, scores[0], MASK_VALUE)
                m = jnp.max(scores[0], axis=-1, keepdims=True)
                for s in scores[1:]:
                    m = jnp.maximum(m, jnp.max(s, axis=-1, keepdims=True))
                if has_sink:
                    m = jnp.maximum(m, sink_ref[h])
                maxes.append(m)
                for part, s in zip(probs, scores):
                    e = jnp.exp(s - m) if score_mul is None else jnp.exp2(s - m)
                    part.append(e.astype(BF16))
            vslot = heads[0] if wide else 2 * g + parity
            o_aug = None
            for part, r in zip(probs, key_rows):
                p = part[0] if len(part) == 1 else jnp.concatenate(part, axis=0)
                o = _dot(p, vaug[vslot, r, :])
                o_aug = o if o_aug is None else o_aug + o
            den = pltpu.roll(o_aug, d, 1)
            for i, h in enumerate(heads):
                rows = slice(i * tq, (i + 1) * tq)
                den_h = den[rows]
                if has_sink:
                    den_h = den_h + jnp.exp(sink_ref[h] - maxes[i])
                results[h] = o_aug[rows] / den_h
    for j in range(n_heads // 2):
        o_ref[:, j * LANES:(j + 1) * LANES] = jnp.where(lo_out, results[2 * j], results[2 * j + 1])


def _attention(q, k, v, n_batch, n_heads, group, tq, *, band=False, ctx=None, sink=None, wide=False,
               score_mul=None):
    seq = q.shape[0] // n_batch
    nq = seq // tq
    n_kv = n_heads // group
    in_specs = [pl.BlockSpec((tq, q.shape[1]), lambda b, i: (b * nq + i, 0)),
                pl.BlockSpec((seq, k.shape[1]), lambda b, i: (b, 0)),
                pl.BlockSpec((seq, v.shape[1]), lambda b, i: (b, 0))]
    args = [q, k, v]
    past = 0
    if ctx is not None:
        kc, vc, kc_spec, vc_spec, past = ctx
        in_specs += [kc_spec, vc_spec]
        args += [kc, vc]
    if sink is not None:
        in_specs.append(pl.BlockSpec(memory_space=pltpu.SMEM))
        args.append(sink)
    n_slots = n_heads if wide else 2 * n_kv
    return pl.pallas_call(
        functools.partial(_attn_kernel, tq, n_heads, group, seq, past, band, sink is not None, wide, score_mul),
        grid=(n_batch, nq),
        in_specs=in_specs,
        out_specs=pl.BlockSpec((tq, n_heads * HEAD_DIM), lambda b, i: (b * nq + i, 0)),
        out_shape=jax.ShapeDtypeStruct((q.shape[0], n_heads * HEAD_DIM), F32),
        scratch_shapes=[pltpu.VMEM((n_slots, seq + past, LANES), BF16),
                        pltpu.VMEM((n_slots, seq + past, LANES), BF16)],
        compiler_params=_cparams(2),
        name="mla_attention" if wide else "gqa_attention",
    )(*args)


def _hgrn_consts():
    i = np.arange(HGRN_GROUP)
    t, s = i[:, None], i[None, :]
    level = np.where(t != s, np.floor(np.log2(np.maximum(t ^ s, 1))) + 1, 0).astype(np.int32)
    return (jnp.asarray(s <= t, F32), jnp.asarray(s >= t, F32),
            jnp.asarray(np.where(s <= t, level, -1), jnp.int32),
            jnp.asarray(np.where(s >= t, level, -1), jnp.int32))


def _level_refs(c, level, rev):
    n = c.shape[0]
    half = 1 << (level - 1)
    bs = 2 * half
    off = half if rev else half - 1
    if bs >= 16:
        return jnp.concatenate([jnp.broadcast_to(c[i * bs + off:i * bs + off + 1, :], (bs, LANES))
                                for i in range(n // bs)], axis=0)
    c3 = c.reshape(n // 8, 8, LANES)
    sub = lax.broadcasted_iota(jnp.int32, c3.shape, 1)
    out = None
    for j in reversed(range(8 // bs)):
        b = jnp.broadcast_to(c3[:, j * bs + off:j * bs + off + 1, :], c3.shape)
        out = b if out is None else jnp.where(sub < (j + 1) * bs, b, out)
    return out.reshape(n, LANES)


def _neg_abs(x):
    return lax.bitcast_convert_type(lax.bitcast_convert_type(x, jnp.uint32) | jnp.uint32(0x80000000), F32)


def _hgrn_kernel(seq, has_s0, want_state, *refs):
    (q_ref, kf_ref, lf_ref, kb_ref, lbw_ref, v_ref, gb_ref, gn_ref,
     trif_ref, trib_ref, lvf_ref, lvb_ref) = refs[:12]
    idx = 12
    if has_s0:
        s0_ref = refs[idx]
        idx += 1
    o_ref = refs[idx]
    idx += 1
    if want_state:
        sfin_ref = refs[idx]
        idx += 1
    of_scr, ob_scr, stf, stb = refs[idx:idx + 4]
    n_groups = seq // HGRN_GROUP
    scale = B_DK ** -0.5
    n_levels = HGRN_GROUP.bit_length() - 1
    dirs = ((kf_ref, lf_ref, trif_ref, lvf_ref, of_scr, stf, False),
            (kb_ref, lbw_ref, trib_ref, lvb_ref, ob_scr, stb, True))

    for d, st in enumerate((stf, stb)):
        st[...] = s0_ref[d].T if has_s0 else jnp.zeros((B_DV, B_DK), F32)

    def nt(a, b):
        return lax.dot_general(a.astype(BF16), b.astype(BF16), NT_DIMS, preferred_element_type=F32)

    def group_step(i, carry):
        work = []
        for k_ref, l_ref, tri_ref, lv_ref, o_scr, st, rev in dirs:
            g = (n_groups - 1 - i) if rev else i
            rows = pl.ds(pl.multiple_of(g * HGRN_GROUP, HGRN_GROUP), HGRN_GROUP)
            c = jnp.dot(tri_ref[...], l_ref[rows, :], precision=HIGHEST, preferred_element_type=F32) * LOG2E
            work.append(dict(rows=rows, c=c, q=q_ref[rows, :] * scale, k=k_ref[rows, :],
                             v=v_ref[rows, :].astype(BF16), lv=lv_ref, o=o_scr, st=st, rev=rev))
        for w in work:
            w["attn"] = jnp.where(w["lv"][...] == 0, nt(w["q"], w["k"]), 0.0)
        for level in range(1, n_levels + 1):
            for w in work:
                e = jnp.exp2(_neg_abs(w["c"] - _level_refs(w["c"], level, w["rev"])))
                w["attn"] = jnp.where(w["lv"][...] == level, nt(w["q"] * e, w["k"] * e), w["attn"])
        for w in work:
            c, st = w["c"], w["st"]
            tot = c[0:1] if w["rev"] else c[HGRN_GROUP - 1:HGRN_GROUP]
            s_t = st[...]
            w["o"][w["rows"], :] = _dot(w["attn"].astype(BF16), w["v"]) + nt(w["q"] * jnp.exp2(c), s_t)
            kt = (w["k"] * jnp.exp2(tot - c)).astype(BF16)
            st[...] = s_t * jnp.exp2(tot) + lax.dot_general(w["v"], kt, TN_DIMS, preferred_element_type=F32)
        return carry

    lax.fori_loop(0, n_groups, group_step, 0, unroll=2)
    if want_state:
        sfin_ref[0] = stf[...].T
        sfin_ref[1] = stb[...].T
    o_ref[...] = _rms(of_scr[...] + ob_scr[...], gn_ref[...]) * gb_ref[...]


def _hgrn(q, kf, lf, kb, lbw, v, gb, gnorm, n_batch, s0=None, want_state=False):
    seq = q.shape[0] // n_batch
    tok = pl.BlockSpec((seq, LANES), lambda b, h: (b, h))
    const = pl.BlockSpec((HGRN_GROUP, HGRN_GROUP), lambda b, h: (0, 0))
    in_specs = [tok] * 7 + [pl.BlockSpec((1, LANES), lambda b, h: (0, 0))] + [const] * 4
    args = [q, kf, lf, kb, lbw, v, gb, gnorm.reshape(1, LANES)] + list(_hgrn_consts())
    if s0 is not None:
        state, layer = s0
        in_specs.append(pl.BlockSpec((None, None, 2, None, B_DK, B_DV), lambda b, h: (b, layer, 0, h, 0, 0)))
        args.append(state)
    out_specs = [tok]
    out_shape = [jax.ShapeDtypeStruct(q.shape, F32)]
    if want_state:
        out_specs.append(pl.BlockSpec((None, 2, None, B_DK, B_DV), lambda b, h: (b, 0, h, 0, 0)))
        out_shape.append(jax.ShapeDtypeStruct((n_batch, 2, B_HEADS, B_DK, B_DV), F32))
    res = pl.pallas_call(
        functools.partial(_hgrn_kernel, seq, s0 is not None, want_state),
        grid=(n_batch, B_HEADS),
        in_specs=in_specs,
        out_specs=out_specs,
        out_shape=out_shape,
        scratch_shapes=[pltpu.VMEM((seq, LANES), F32),
                        pltpu.VMEM((seq, LANES), F32),
                        pltpu.VMEM((B_DV, B_DK), F32),
                        pltpu.VMEM((B_DV, B_DK), F32)],
        compiler_params=_cparams(2),
        name="hgrn2",
    )(*args)
    return res if want_state else (res[0], None)


def _outffn_kernel(o1_ref, o2_ref, x_ref, mod_ref, wo_ref, ln_ref, wg_ref, wu_ref, wd_ref, out_ref):
    m = mod_ref[0]
    half = o1_ref.shape[1]
    y = (_dot(o1_ref[...].astype(BF16), wo_ref[0:half, :])
         + _dot(o2_ref[...].astype(BF16), wo_ref[half:2 * half, :]))
    x1 = _layer_norm(ALPHA * x_ref[...] + m[2:3] * y, ln_ref[0:1, :], ln_ref[1:2, :])
    h = (x1 * (1.0 + m[4:5]) + m[3:4]).astype(BF16)
    acc = None
    for a in range(0, D_FF, 512):
        b = min(a + 512, D_FF)
        act = (_silu(_dot(h, wg_ref[:, a:b])) * _dot(h, wu_ref[:, a:b])).astype(BF16)
        part = _dot(act, wd_ref[a:b, :])
        acc = part if acc is None else acc + part
    out_ref[...] = _layer_norm(ALPHA * x1 + m[5:6] * acc, ln_ref[2:3, :], ln_ref[3:4, :])


def _outffn(o1, o2, x, mod, wo_bf, ln, wg_bf, wu_bf, wd_bf, rows_per_cond, tm=512):
    m_rows = x.shape[0]
    half = o1.shape[1]
    consts = [wo_bf, ln, wg_bf, wu_bf, wd_bf]
    in_specs = [pl.BlockSpec((tm, half), lambda i: (i, 0)),
                pl.BlockSpec((tm, half), lambda i: (i, 0)),
                pl.BlockSpec((tm, D_MODEL), lambda i: (i, 0)),
                pl.BlockSpec((1, 6, D_MODEL), lambda i: (i * tm // rows_per_cond, 0, 0))]
    in_specs += [pl.BlockSpec(a.shape, lambda i: (0, 0)) for a in consts]
    return pl.pallas_call(
        _outffn_kernel,
        grid=(m_rows // tm,),
        in_specs=in_specs,
        out_specs=pl.BlockSpec((tm, D_MODEL), lambda i: (i, 0)),
        out_shape=jax.ShapeDtypeStruct((m_rows, D_MODEL), F32),
        compiler_params=_cparams(1),
        name="outproj_ffn",
    )(o1, o2, x, mod, *consts)


def _cache_spec(past, width, layer):
    return pl.BlockSpec((None, None, past, width), lambda b, i: (b, layer, 0, 0))


def _even_layer(xp, xs, modp, mods, w_in_bf, lb, sink, gnorm, tail_w, rope_hd,
                cache_k, cache_v, state, e, nbp, nbs):
    seq_p = xp.shape[0] // nbp
    seq_s = xs.shape[0] // nbs
    past = cache_k.shape[2]
    pp = _inproj_even(xp, modp, w_in_bf, lb, None, xp.shape[0])
    ps = _inproj_even(xs, mods, w_in_bf, lb, rope_hd, seq_s)
    qa, ka, va, qb, ib, kf, lf, kb, lbw, gb = pp
    oa_p = _attention(qa, ka, va, nbp, A_HEADS, A_HEADS // A_KV, seq_p, sink=sink)
    ob_p, s_new = _hgrn(qb, kf, lf, kb, lbw, ib, gb, gnorm, nbp, want_state=True)
    xp = _outffn(oa_p, ob_p, xp, modp, *tail_w, xp.shape[0])
    new = (ka.reshape(nbp, seq_p, A_KV, HEAD_DIM), va.reshape(nbp, seq_p, A_KV, HEAD_DIM), s_new)
    qa, ka, va, qb, ib, kf, lf, kb, lbw, gb = ps
    spec = _cache_spec(past, A_KV * HEAD_DIM, e)
    oa_s = _attention(qa, ka, va, nbs, A_HEADS, A_HEADS // A_KV, BAND_TQ, band=True,
                      ctx=(cache_k, cache_v, spec, spec, past), sink=sink)
    ob_s, _ = _hgrn(qb, kf, lf, kb, lbw, ib, gb, gnorm, nbs, s0=(state, e))
    xs = _outffn(oa_s, ob_s, xs, mods, *tail_w, seq_s)
    return xp, xs, new


def _odd_layer(xp, xs, modp, mods, in_w, tail_w, rope_hd, rope_c,
               cache_ckv, cache_pe, cache_k, cache_v, o, nbp, nbs):
    seq_p = xp.shape[0] // nbp
    seq_s = xs.shape[0] // nbs
    past = cache_ckv.shape[2]
    wkv_bf = in_w[4]
    mla_mul = (C_NOPE + C_ROPE) ** -0.5 * LOG2E
    qc, ckv, kc, vc, pe, qd, kd, vd = _inproj_odd(xp, modp, *in_w, None, xp.shape[0])
    oc_p = _attention(qc, kc, vc, nbp, C_HEADS, 1, seq_p, wide=True, score_mul=mla_mul)
    od_p = _attention(qd, kd, vd, nbp, D_HEADS, D_HEADS // D_KV, seq_p)
    xp = _outffn(oc_p, od_p, xp, modp, *tail_w, xp.shape[0])
    new = (ckv.reshape(nbp, seq_p, C_KV_LORA), pe[:, C_NOPE:C_NOPE + C_ROPE].reshape(nbp, seq_p, C_ROPE),
           kd.reshape(nbp, seq_p, D_KV, HEAD_DIM), vd.reshape(nbp, seq_p, D_KV, HEAD_DIM))
    qc, ckv, kc, vc, pe, qd, kd, vd = _inproj_odd(xs, mods, *in_w, rope_hd + rope_c, seq_s)
    pe_ctx = jnp.pad(cache_pe[:, o].reshape(nbs * past, C_ROPE), ((0, 0), (C_NOPE, LANES - C_NOPE - C_ROPE)))
    kcc, vcc = _kvup(cache_ckv[:, o].reshape(nbs * past, C_KV_LORA), pe_ctx, wkv_bf)
    oc_s = _attention(qc, kc, vc, nbs, C_HEADS, 1, SAMPLE_TQ, wide=True, score_mul=mla_mul,
                      ctx=(kcc, vcc, pl.BlockSpec((past, C_HEADS * LANES), lambda b, i: (b, 0)),
                           pl.BlockSpec((past, C_HEADS * C_V), lambda b, i: (b, 0)), past))
    spec = _cache_spec(past, D_KV * HEAD_DIM, o)
    od_s = _attention(qd, kd, vd, nbs, D_HEADS, D_HEADS // D_KV, SAMPLE_TQ, ctx=(cache_k, cache_v, spec, spec, past))
    xs = _outffn(oc_s, od_s, xs, mods, *tail_w, seq_s)
    return xp, xs, new


def kernel(x_prompt, x_sample, cache_a_k, cache_a_v, state_b, cache_c_kv, cache_c_pe, cache_d_k, cache_d_v, c, c_ctx, w_ada, b_ada, ln_g, ln_b, w_in_ab, a_sink, b_lb, b_gnorm, w_in_cd, c_q_norm, c_kv_norm, c_w_q_up, c_w_kv_up, d_q_norm, d_k_norm, w_out, w_ffn_gate, w_ffn_up, w_ffn_down):
    nbp, seq_p, _ = x_prompt.shape
    nbs, seq_s, _ = x_sample.shape
    past = cache_a_k.shape[2]
    xp = x_prompt.reshape(nbp * seq_p, D_MODEL)
    xs = x_sample.reshape(nbs * seq_s, D_MODEL)
    rope_hd = _rope_tables(seq_s, HEAD_DIM)
    cos_c, sin_c = _rope_tables(seq_s, C_ROPE)
    ones = jnp.ones((seq_s, C_NOPE), F32)
    rope_c = (jnp.concatenate([ones, cos_c[:, :C_NOPE]], axis=1),
              jnp.concatenate([0.0 * ones, sin_c[:, :C_ROPE], 0.0 * ones[:, :LANES - C_NOPE - C_ROPE]], axis=1))
    lb_w = jax.nn.softmax(b_lb.astype(F32), axis=0)
    lb_all = jnp.cumsum(lb_w, axis=0) - lb_w[:1]
    mod_all = _modulation(c, c_ctx, w_ada, b_ada)
    n_even = cache_a_k.shape[1]
    n_odd = cache_c_kv.shape[1]
    ca_k = cache_a_k.reshape(nbs, n_even, past, A_KV * HEAD_DIM)
    ca_v = cache_a_v.reshape(nbs, n_even, past, A_KV * HEAD_DIM)
    cd_k = cache_d_k.reshape(nbs, n_odd, past, D_KV * HEAD_DIM)
    cd_v = cache_d_v.reshape(nbs, n_odd, past, D_KV * HEAD_DIM)
    ak, av, sb, ckv, cpe, dk, dv = [], [], [], [], [], [], []
    for l in range(DEPTH):
        modp = mod_all[l, 0:1]
        mods = mod_all[l, 1:1 + nbs]
        ln = jnp.concatenate([ln_g[l, 0:1], ln_b[l, 0:1], ln_g[l, 1:2], ln_b[l, 1:2]], axis=0)
        tail_w = (_to_bf16(w_out, l), ln, _to_bf16(w_ffn_gate, l), _to_bf16(w_ffn_up, l), _to_bf16(w_ffn_down, l))
        if l % 2 == 0:
            e = l // 2
            xp, xs, (k_new, v_new, s_new) = _even_layer(
                xp, xs, modp, mods, _to_bf16(w_in_ab, e), lb_all[e], a_sink[e], b_gnorm[e], tail_w,
                rope_hd, ca_k, ca_v, state_b, e, nbp, nbs)
            ak.append(k_new)
            av.append(v_new)
            sb.append(s_new)
        else:
            o = l // 2
            w_in = w_in_cd[o]
            lat = C_Q_LORA + C_KV_LORA
            zeros = lambda n: jnp.zeros((D_MODEL, n), F32)
            w_in = jnp.concatenate([w_in[:, :lat], zeros(C_NOPE), w_in[:, lat:lat + C_ROPE],
                                    zeros(LANES - C_NOPE - C_ROPE), w_in[:, lat + C_ROPE:]], axis=1).astype(BF16)
            wq = c_w_q_up[o].reshape(C_Q_LORA, C_HEADS, C_NOPE + C_ROPE)
            wq = jnp.pad(wq, ((0, 0), (0, 0), (0, LANES - C_NOPE - C_ROPE))).reshape(C_Q_LORA, -1).astype(BF16)
            wkv = c_w_kv_up[o].reshape(C_KV_LORA, C_HEADS, C_NOPE + C_V)
            wk = jnp.pad(wkv[:, :, :C_NOPE], ((0, 0), (0, 0), (0, LANES - C_NOPE))).reshape(C_KV_LORA, -1)
            wkv = jnp.concatenate([wk, wkv[:, :, C_NOPE:].reshape(C_KV_LORA, -1)], axis=1).astype(BF16)
            in_w = (w_in, c_q_norm[o].reshape(1, -1), c_kv_norm[o].reshape(1, -1), wq, wkv,
                    jnp.tile(d_q_norm[o], D_HEADS).reshape(1, -1), jnp.tile(d_k_norm[o], D_KV).reshape(1, -1))
            xp, xs, (c_new, pe_new, k_new, v_new) = _odd_layer(
                xp, xs, modp, mods, in_w, tail_w, rope_hd, rope_c, cache_c_kv, cache_c_pe, cd_k, cd_v, o, nbp, nbs)
            ckv.append(c_new)
            cpe.append(pe_new)
            dk.append(k_new)
            dv.append(v_new)
    return (xp.reshape(nbp, seq_p, D_MODEL), xs.reshape(nbs, seq_s, D_MODEL),
            jnp.stack(ak, axis=1), jnp.stack(av, axis=1), jnp.stack(sb, axis=1),
            jnp.stack(ckv, axis=1), jnp.stack(cpe, axis=1), jnp.stack(dk, axis=1), jnp.stack(dv, axis=1))
```

```python
import functools

import numpy as np
import jax
import jax.numpy as jnp
from jax import lax
from jax.experimental import pallas as pl
from jax.experimental.pallas import tpu as pltpu

F32 = jnp.float32
BF16 = jnp.bfloat16

D_MODEL = 1024
DEPTH = 4
GRID_W = 64
HEAD_DIM = 64
ROPE_BASE = 10000.0
MASK_VALUE = -1e30
F_FLOOR = 1e-30
A_HEADS = 8
A_KV = 2
WINDOW = 128
B_HEADS = 4
B_DK = 128
B_DV = 128
C_HEADS = 8
C_Q_LORA = 384
C_KV_LORA = 256
C_NOPE = 64
C_ROPE = 32
C_V = 64
D_HEADS = 8
D_KV = 4
D_FF = 2816
ALPHA = (2 * DEPTH) ** 0.25

LANES = 128
HGRN_GROUP = 128
LOG2E = 1.4426950408889634
SAMPLE_TQ = 512
PROMPT_SUB_A = 4
PROMPT_SUB_D = 2
BAND_TQ = 256
VMEM_LIMIT = 56 * 1024 * 1024
NT_DIMS = (((1,), (1,)), ((), ()))
TN_DIMS = (((0,), (0,)), ((), ()))


def _cparams(n_axes):
    return pltpu.CompilerParams(dimension_semantics=("arbitrary",) * n_axes,
                                vmem_limit_bytes=VMEM_LIMIT)


def _silu(x):
    return x * jax.nn.sigmoid(x)


def _dot(a, b):
    return jnp.dot(a, b, preferred_element_type=F32)


def _layer_norm(x, g, b):
    mu = jnp.mean(x, axis=-1, keepdims=True)
    xc = x - mu
    var = jnp.mean(xc * xc, axis=-1, keepdims=True)
    return xc * lax.rsqrt(var + 1e-5) * g + b


def _rms(x, g):
    return x * lax.rsqrt(jnp.mean(x * x, axis=-1, keepdims=True) + 1e-6) * g


def _rms_heads64(x, g):
    outs = []
    for j in range(x.shape[1] // LANES):
        xg = x[:, j * LANES:(j + 1) * LANES]
        x2 = xg * xg
        lo = lax.broadcasted_iota(jnp.int32, xg.shape, 1) < HEAD_DIM
        s_lo = jnp.sum(jnp.where(lo, x2, 0.0), axis=-1, keepdims=True)
        s_hi = jnp.sum(jnp.where(lo, 0.0, x2), axis=-1, keepdims=True)
        ms = jnp.where(lo, s_lo, s_hi) * (1.0 / HEAD_DIM)
        outs.append(xg * lax.rsqrt(ms + 1e-6))
    return jnp.concatenate(outs, axis=1) * g


def _rope(x, cos, sin, half):
    outs = []
    for j in range(x.shape[1] // LANES):
        xg = x[:, j * LANES:(j + 1) * LANES]
        lane = lax.broadcasted_iota(jnp.int32, xg.shape, 1)
        up = pltpu.roll(xg, LANES - half, 1)
        down = pltpu.roll(xg, half, 1)
        rot = jnp.where((lane & (2 * half - 1)) < half, up, down)
        outs.append(xg * cos + rot * sin)
    return outs[0] if len(outs) == 1 else jnp.concatenate(outs, axis=1)


def _rope_tables(n_tokens, rot_dim):
    n_rows = n_tokens // GRID_W
    row = jnp.broadcast_to(jnp.arange(n_rows, dtype=F32)[:, None], (n_rows, GRID_W)).reshape(-1)
    col = jnp.broadcast_to(jnp.arange(GRID_W, dtype=F32)[None, :], (n_rows, GRID_W)).reshape(-1)
    quarter = rot_dim // 4
    inv = ROPE_BASE ** (-jnp.arange(quarter, dtype=F32) / quarter)
    ar = row[:, None] * inv
    ac = col[:, None] * inv
    cos = jnp.concatenate([jnp.cos(ar), jnp.cos(ar), jnp.cos(ac), jnp.cos(ac)], axis=-1)
    sin = jnp.concatenate([-jnp.sin(ar), jnp.sin(ar), -jnp.sin(ac), jnp.sin(ac)], axis=-1)
    reps = LANES // rot_dim
    return jnp.tile(cos, (1, reps)), jnp.tile(sin, (1, reps))


def _cast_kernel(x_ref, o_ref):
    o_ref[...] = x_ref[...].astype(BF16)


def _to_bf16(w, layer, tm=256):
    _, rows, cols = w.shape
    return pl.pallas_call(
        _cast_kernel,
        grid=(rows // tm,),
        in_specs=[pl.BlockSpec((None, tm, cols), lambda i: (layer, i, 0))],
        out_specs=pl.BlockSpec((tm, cols), lambda i: (i, 0)),
        out_shape=jax.ShapeDtypeStruct((rows, cols), BF16),
        compiler_params=_cparams(1),
        name="to_bf16",
    )(w)


def _mod_kernel(c_ref, w_ref, b_ref, o_ref):
    a = _silu(c_ref[...]).astype(BF16)
    o_ref[0] = _dot(a, w_ref[0].astype(BF16)) + b_ref[0]


def _modulation(c, c_ctx, w_ada, b_ada):
    nb = c.shape[0]
    cond = jnp.zeros((16, D_MODEL), F32).at[0].set(c_ctx).at[1:1 + nb].set(c)
    tn = 1536
    out = pl.pallas_call(
        _mod_kernel,
        grid=(DEPTH, 6 * D_MODEL // tn),
        in_specs=[pl.BlockSpec((16, D_MODEL), lambda l, j: (0, 0)),
                  pl.BlockSpec((1, D_MODEL, tn), lambda l, j: (l, 0, j)),
                  pl.BlockSpec((1, 1, tn), lambda l, j: (l, 0, j))],
        out_specs=pl.BlockSpec((1, 16, tn), lambda l, j: (l, 0, j)),
        out_shape=jax.ShapeDtypeStruct((DEPTH, 16, 6 * D_MODEL), F32),
        compiler_params=_cparams(2),
        name="modulation",
    )(cond, w_ada, b_ada.reshape(DEPTH, 1, 6 * D_MODEL))
    return out.reshape(DEPTH, 16, 6, D_MODEL)


def _modulate(x_ref, mod_ref, shift_row):
    m = mod_ref[0]
    return (x_ref[...] * (1.0 + m[shift_row + 1:shift_row + 2]) + m[shift_row:shift_row + 1]).astype(BF16)


def _inproj_even_kernel(rope, *refs):
    if rope:
        x_ref, mod_ref, w_ref, lb_ref, cos_ref, sin_ref = refs[:6]
        outs = refs[6:]
    else:
        x_ref, mod_ref, w_ref, lb_ref = refs[:4]
        outs = refs[4:]
    qa_ref, ka_ref, va_ref, qb_ref, ib_ref, kf_ref, lf_ref, kb_ref, lbw_ref, gb_ref = outs
    h = _modulate(x_ref, mod_ref, 0)

    def proj(a, b):
        return _dot(h, w_ref[:, a:b])

    qa = proj(0, 512)
    ka = proj(512, 640)
    if rope:
        cos = cos_ref[...]
        sin = sin_ref[...]
        qa = _rope(qa, cos, sin, HEAD_DIM // 4)
        ka = _rope(ka, cos, sin, HEAD_DIM // 4)
    qa_ref[...] = qa
    ka_ref[...] = ka
    va_ref[...] = proj(640, 768)
    qb_ref[...] = _silu(proj(768, 1280))
    ib_ref[...] = proj(1280, 1792)
    for d, (k_ref, l_ref) in enumerate(((kf_ref, lf_ref), (kb_ref, lbw_ref))):
        lb = lb_ref[d:d + 1, :]
        f = lb + (1.0 - lb) * jax.nn.sigmoid(proj(1792 + 512 * d, 2304 + 512 * d))
        l_ref[...] = jnp.log(jnp.maximum(f, F_FLOOR))
        k_ref[...] = 1.0 - f
    gb_ref[...] = _silu(proj(2816, 3328))


def _inproj_even(x, mod, w_bf, lb, rope_tabs, rows_per_cond, tm=256):
    m_rows = x.shape[0]
    rope = rope_tabs is not None
    widths = (512, 128, 128, 512, 512, 512, 512, 512, 512, 512)
    in_specs = [pl.BlockSpec((tm, D_MODEL), lambda i: (i, 0)),
                pl.BlockSpec((1, 6, D_MODEL), lambda i: (i * tm // rows_per_cond, 0, 0)),
                pl.BlockSpec(w_bf.shape, lambda i: (0, 0)),
                pl.BlockSpec(lb.shape, lambda i: (0, 0))]
    args = [x, mod, w_bf, lb]
    if rope:
        nblk = rope_tabs[0].shape[0] // tm
        in_specs += [pl.BlockSpec((tm, LANES), lambda i: (i % nblk, 0))] * 2
        args += list(rope_tabs)
    return pl.pallas_call(
        functools.partial(_inproj_even_kernel, rope),
        grid=(m_rows // tm,),
        in_specs=in_specs,
        out_specs=[pl.BlockSpec((tm, w), lambda i: (i, 0)) for w in widths],
        out_shape=[jax.ShapeDtypeStruct((m_rows, w), F32) for w in widths],
        compiler_params=_cparams(1),
        name="inproj_even",
    )(*args)


def _inproj_odd_kernel(rope, *refs):
    if rope:
        (x_ref, mod_ref, w_ref, gcq_ref, gckv_ref, wq_ref, wkv_ref, gdq_ref, gdk_ref,
         cosd_ref, sind_ref, cosc_ref, sinc_ref) = refs[:13]
        outs = refs[13:]
    else:
        x_ref, mod_ref, w_ref, gcq_ref, gckv_ref, wq_ref, wkv_ref, gdq_ref, gdk_ref = refs[:9]
        outs = refs[9:]
    qc_ref, ckv_ref, kc_ref, vc_ref, pe_ref, qd_ref, kd_ref, vd_ref = outs
    h = _modulate(x_ref, mod_ref, 0)

    def proj(a, b):
        return _dot(h, w_ref[:, a:b])

    cq = _rms(proj(0, 384), gcq_ref[...]).astype(BF16)
    qc = _dot(cq, wq_ref[...])
    ckv = _rms(proj(384, 640), gckv_ref[...])
    ckv_ref[...] = ckv
    ckv_bf = ckv.astype(BF16)
    kn = _dot(ckv_bf, wkv_ref[:, 0:C_HEADS * LANES])
    vc_ref[...] = _dot(ckv_bf, wkv_ref[:, C_HEADS * LANES:]).astype(BF16)
    pe = proj(640, 768)
    qd = _rms_heads64(proj(768, 1280), gdq_ref[...])
    kd = _rms_heads64(proj(1280, 1536), gdk_ref[...])
    if rope:
        qc = _rope(qc, cosc_ref[...], sinc_ref[...], C_ROPE // 4)
        pe = _rope(pe, cosc_ref[...], sinc_ref[...], C_ROPE // 4)
        qd = _rope(qd, cosd_ref[...], sind_ref[...], HEAD_DIM // 4)
        kd = _rope(kd, cosd_ref[...], sind_ref[...], HEAD_DIM // 4)
    qc_ref[...] = qc.astype(BF16)
    pe_ref[...] = pe
    kc_ref[...] = jnp.concatenate([kn[:, j * LANES:(j + 1) * LANES] + pe for j in range(C_HEADS)],
                                  axis=1).astype(BF16)
    qd_ref[...] = qd
    kd_ref[...] = kd
    vd_ref[...] = proj(1536, 1792)


def _inproj_odd(x, mod, w_bf, gcq, gckv, wq_bf, wkv_bf, gdq, gdk, rope_tabs, rows_per_cond, tm=512):
    m_rows = x.shape[0]
    rope = rope_tabs is not None
    outs = ((C_HEADS * LANES, BF16), (C_KV_LORA, F32), (C_HEADS * LANES, BF16), (C_HEADS * C_V, BF16),
            (LANES, F32), (D_HEADS * HEAD_DIM, F32), (D_KV * HEAD_DIM, F32), (D_KV * HEAD_DIM, F32))
    consts = [w_bf, gcq, gckv, wq_bf, wkv_bf, gdq, gdk]
    in_specs = [pl.BlockSpec((tm, D_MODEL), lambda i: (i, 0)),
                pl.BlockSpec((1, 6, D_MODEL), lambda i: (i * tm // rows_per_cond, 0, 0))]
    in_specs += [pl.BlockSpec(a.shape, lambda i: (0, 0)) for a in consts]
    args = [x, mod] + consts
    if rope:
        nblk = rope_tabs[0].shape[0] // tm
        in_specs += [pl.BlockSpec((tm, LANES), lambda i: (i % nblk, 0))] * 4
        args += list(rope_tabs)
    return pl.pallas_call(
        functools.partial(_inproj_odd_kernel, rope),
        grid=(m_rows // tm,),
        in_specs=in_specs,
        out_specs=[pl.BlockSpec((tm, w), lambda i: (i, 0)) for w, _ in outs],
        out_shape=[jax.ShapeDtypeStruct((m_rows, w), dt) for w, dt in outs],
        compiler_params=_cparams(1),
        name="inproj_odd",
    )(*args)


def _kvup_kernel(x_ref, pe_ref, w_ref, kc_ref, vc_ref):
    x = x_ref[...].astype(BF16)
    kn = _dot(x, w_ref[:, 0:C_HEADS * LANES])
    pe = pe_ref[...]
    kc_ref[...] = jnp.concatenate([kn[:, j * LANES:(j + 1) * LANES] + pe for j in range(C_HEADS)],
                                  axis=1).astype(BF16)
    vc_ref[...] = _dot(x, w_ref[:, C_HEADS * LANES:]).astype(BF16)


def _kvup(lat, pe_slab, wkv_bf, tm=512):
    m_rows = lat.shape[0]
    return pl.pallas_call(
        _kvup_kernel,
        grid=(m_rows // tm,),
        in_specs=[pl.BlockSpec((tm, C_KV_LORA), lambda i: (i, 0)),
                  pl.BlockSpec((tm, LANES), lambda i: (i, 0)),
                  pl.BlockSpec(wkv_bf.shape, lambda i: (0, 0))],
        out_specs=[pl.BlockSpec((tm, C_HEADS * LANES), lambda i: (i, 0)),
                   pl.BlockSpec((tm, C_HEADS * C_V), lambda i: (i, 0))],
        out_shape=[jax.ShapeDtypeStruct((m_rows, C_HEADS * LANES), BF16),
                   jax.ShapeDtypeStruct((m_rows, C_HEADS * C_V), BF16)],
        compiler_params=_cparams(1),
        name="kv_up_ctx",
    )(lat, pe_slab, wkv_bf)


def _attn_kernel(tq, n_heads, group, seq_k, past, band, has_sink, wide, score_mul, n_sub, *refs):
    q_ref, k_ref, v_ref = refs[:3]
    idx = 3
    if past:
        kc_ref, vc_ref = refs[idx:idx + 2]
        idx += 2
    if has_sink:
        sink_ref = refs[idx]
        idx += 1
    o_ref, kpad, vaug = refs[idx:idx + 3]
    d = HEAD_DIM
    n_kv = n_heads // group
    n_slots = n_heads if wide else 2 * n_kv
    total = seq_k + past
    qi = pl.program_id(1)

    def _prepare_keys_values():
        for sub in range(n_sub):
            parts = [(k_ref, v_ref, sub * seq_k, 0, seq_k)]
            if past:
                parts.append((kc_ref, vc_ref, 0, seq_k, past))
            base = sub * n_slots
            for ks_ref, vs_ref, src0, r0, n in parts:
                src = slice(src0, src0 + n)
                rows = slice(r0, r0 + n)
                lo = lax.broadcasted_iota(jnp.int32, (n, LANES), 1) < d
                for j in range(n_kv * d // LANES):
                    cols = slice(j * LANES, (j + 1) * LANES)
                    vs = vs_ref[src, cols].astype(F32)
                    if wide:
                        vaug[base + 2 * j, rows, :] = jnp.where(lo, vs, 1.0).astype(BF16)
                        vaug[base + 2 * j + 1, rows, :] = jnp.where(lo, 1.0, vs).astype(BF16)
                        continue
                    vr = pltpu.roll(vs, d, 1)
                    ks = ks_ref[src, cols]
                    kr = pltpu.roll(ks, d, 1)
                    for g, (k_even, k_odd, v_even, v_odd) in ((2 * j, (ks, kr, vs, vr)), (2 * j + 1, (kr, ks, vr, vs))):
                        kpad[base + 2 * g, rows, :] = jnp.where(lo, k_even, 0.0).astype(BF16)
                        kpad[base + 2 * g + 1, rows, :] = jnp.where(lo, 0.0, k_odd).astype(BF16)
                        vaug[base + 2 * g, rows, :] = jnp.where(lo, v_even, 1.0).astype(BF16)
                        vaug[base + 2 * g + 1, rows, :] = jnp.where(lo, 1.0, v_odd).astype(BF16)
                if wide:
                    for h in range(n_heads):
                        kpad[base + h, rows, :] = ks_ref[src, h * LANES:(h + 1) * LANES]

    if seq_k == tq:
        _prepare_keys_values()
    else:
        pl.when(qi == 0)(_prepare_keys_values)

    if band:
        kb = tq + 2 * WINDOW
        start = pl.multiple_of(jnp.clip(qi * tq - WINDOW, 0, seq_k - kb), LANES)
        diff = ((qi * tq - start) + lax.broadcasted_iota(jnp.int32, (tq, kb), 0)
                - lax.broadcasted_iota(jnp.int32, (tq, kb), 1))
        bmask = jnp.abs(diff) <= WINDOW
        key_rows = [pl.ds(start, kb), pl.ds(seq_k, past)]
    else:
        key_rows = [slice(0, total)]

    def nt(a, b):
        return lax.dot_general(a, b, NT_DIMS, preferred_element_type=F32)

    lo_out = lax.broadcasted_iota(jnp.int32, (tq, LANES), 1) < d
    for sub in range(n_sub):
        base = sub * n_slots
        q_rows = slice(sub * tq, (sub + 1) * tq)
        q_slabs = {}
        results = {}
        for g in range(n_kv):
            for parity in (0, 1):
                heads = [h for h in range(g * group, (g + 1) * group) if h % 2 == parity]
                if not heads:
                    continue
                probs = [[] for _ in key_rows]
                maxes = []
                for h in heads:
                    slot = base + (h if wide else 2 * g + parity)
                    qs = h if wide else h // 2
                    if qs not in q_slabs:
                        q = q_ref[q_rows, qs * LANES:(qs + 1) * LANES]
                        q_slabs[qs] = q if wide else (q * d ** -0.5).astype(BF16)
                    scores = [nt(q_slabs[qs], kpad[slot, r, :]) for r in key_rows]
                    if score_mul is not None:
                        scores = [s * score_mul for s in scores]
                    if band:
                        scores[0] = jnp.where(bmask, scores[0], MASK_VALUE)
                    m = jnp.max(scores[0], axis=-1, keepdims=True)
                    for s in scores[1:]:
                        m = jnp.maximum(m, jnp.max(s, axis=-1, keepdims=True))
                    if has_sink:
                        m = jnp.maximum(m, sink_ref[h])
                    maxes.append(m)
                    for part, s in zip(probs, scores):
                        e = jnp.exp(s - m) if score_mul is None else jnp.exp2(s - m)
                        part.append(e.astype(BF16))
                vslot = base + (heads[0] if wide else 2 * g + parity)
                o_aug = None
                for part, r in zip(probs, key_rows):
                    p = part[0] if len(part) == 1 else jnp.concatenate(part, axis=0)
                    o = _dot(p, vaug[vslot, r, :])
                    o_aug = o if o_aug is None else o_aug + o
                den = pltpu.roll(o_aug, d, 1)
                for i, h in enumerate(heads):
                    rows = slice(i * tq, (i + 1) * tq)
                    den_h = den[rows]
                    if has_sink:
                        den_h = den_h + jnp.exp(sink_ref[h] - maxes[i])
                    results[h] = o_aug[rows] / den_h
        for j in range(n_heads // 2):
            o_ref[q_rows, j * LANES:(j + 1) * LANES] = jnp.where(lo_out, results[2 * j], results[2 * j + 1])


def _attention(q, k, v, n_batch, n_heads, group, tq, *, band=False, ctx=None, sink=None, wide=False,
               score_mul=None, n_sub=1):
    seq = q.shape[0] // n_batch
    nq = seq // tq
    n_kv = n_heads // group
    assert n_sub == 1 or (nq == 1 and ctx is None and n_batch % n_sub == 0)
    in_specs = [pl.BlockSpec((n_sub * tq, q.shape[1]), lambda b, i: (b * nq + i, 0)),
                pl.BlockSpec((n_sub * seq, k.shape[1]), lambda b, i: (b, 0)),
                pl.BlockSpec((n_sub * seq, v.shape[1]), lambda b, i: (b, 0))]
    args = [q, k, v]
    past = 0
    if ctx is not None:
        kc, vc, kc_spec, vc_spec, past = ctx
        in_specs += [kc_spec, vc_spec]
        args += [kc, vc]
    if sink is not None:
        in_specs.append(pl.BlockSpec(memory_space=pltpu.SMEM))
        args.append(sink)
    n_slots = n_sub * (n_heads if wide else 2 * n_kv)
    return pl.pallas_call(
        functools.partial(_attn_kernel, tq, n_heads, group, seq, past, band, sink is not None, wide, score_mul,
                          n_sub),
        grid=(n_batch // n_sub, nq),
        in_specs=in_specs,
        out_specs=pl.BlockSpec((n_sub * tq, n_heads * HEAD_DIM), lambda b, i: (b * nq + i, 0)),
        out_shape=jax.ShapeDtypeStruct((q.shape[0], n_heads * HEAD_DIM), F32),
        scratch_shapes=[pltpu.VMEM((n_slots, seq + past, LANES), BF16),
                        pltpu.VMEM((n_slots, seq + past, LANES), BF16)],
        compiler_params=_cparams(2),
        name="mla_attention" if wide else "gqa_attention",
    )(*args)


def _hgrn_levels():
    i = np.arange(HGRN_GROUP)
    t, s = i[:, None], i[None, :]
    level = np.where(t != s, np.floor(np.log2(np.maximum(t ^ s, 1))) + 1, 0).astype(np.int32)
    return (jnp.asarray(np.where(s <= t, level, -1), jnp.int32),
            jnp.asarray(np.where(s >= t, level, -1), jnp.int32))


def _scan_rows(x, rev):
    n = x.shape[0]
    row = lax.broadcasted_iota(jnp.int32, x.shape, 0)
    step = 1
    while step < n:
        if step < 8:
            if rev:
                x = x + jnp.where(row < n - step, pltpu.roll(x, n - step, 0), 0.0)
            else:
                x = x + jnp.where(row >= step, pltpu.roll(x, step, 0), 0.0)
        elif rev:
            x = jnp.concatenate([x[:n - step] + x[step:], x[n - step:]], axis=0)
        else:
            x = jnp.concatenate([x[:step], x[step:] + x[:n - step]], axis=0)
        step *= 2
    return x


def _level_refs(c, level, rev):
    n = c.shape[0]
    half = 1 << (level - 1)
    bs = 2 * half
    off = half if rev else half - 1
    if bs >= 16:
        return jnp.concatenate([jnp.broadcast_to(c[i * bs + off:i * bs + off + 1, :], (bs, LANES))
                                for i in range(n // bs)], axis=0)
    c3 = c.reshape(n // 8, 8, LANES)
    sub = lax.broadcasted_iota(jnp.int32, c3.shape, 1)
    out = None
    for j in reversed(range(8 // bs)):
        b = jnp.broadcast_to(c3[:, j * bs + off:j * bs + off + 1, :], c3.shape)
        out = b if out is None else jnp.where(sub < (j + 1) * bs, b, out)
    return out.reshape(n, LANES)


def _neg_abs(x):
    return lax.bitcast_convert_type(lax.bitcast_convert_type(x, jnp.uint32) | jnp.uint32(0x80000000), F32)


def _hgrn_kernel(seq, has_s0, want_state, *refs):
    q_ref, kf_ref, lf_ref, kb_ref, lbw_ref, v_ref, gb_ref, gn_ref, lvf_ref, lvb_ref = refs[:10]
    idx = 10
    if has_s0:
        s0_ref = refs[idx]
        idx += 1
    o_ref = refs[idx]
    idx += 1
    if want_state:
        sfin_ref = refs[idx]
        idx += 1
    of_scr, ob_scr, stf, stb = refs[idx:idx + 4]
    n_groups = seq // HGRN_GROUP
    scale = B_DK ** -0.5
    n_levels = HGRN_GROUP.bit_length() - 1
    dirs = ((kf_ref, lf_ref, lvf_ref, of_scr, stf, False),
            (kb_ref, lbw_ref, lvb_ref, ob_scr, stb, True))

    for d, st in enumerate((stf, stb)):
        st[...] = s0_ref[d].T if has_s0 else jnp.zeros((B_DV, B_DK), F32)

    def nt(a, b):
        return lax.dot_general(a.astype(BF16), b.astype(BF16), NT_DIMS, preferred_element_type=F32)

    def group_step(i, carry):
        work = []
        for k_ref, l_ref, lv_ref, o_scr, st, rev in dirs:
            g = (n_groups - 1 - i) if rev else i
            rows = pl.ds(pl.multiple_of(g * HGRN_GROUP, HGRN_GROUP), HGRN_GROUP)
            c = _scan_rows(l_ref[rows, :], rev) * LOG2E
            work.append(dict(rows=rows, c=c, q=q_ref[rows, :] * scale, k=k_ref[rows, :],
                             v=v_ref[rows, :].astype(BF16), lv=lv_ref, o=o_scr, st=st, rev=rev))
        for w in work:
            w["qb"] = w["q"].astype(BF16)
            w["kb"] = w["k"].astype(BF16)
            w["attn"] = jnp.where(w["lv"][...] == 0, nt(w["qb"], w["kb"]), 0.0)
        for level in range(1, n_levels + 1):
            for w in work:
                e = jnp.exp2(_neg_abs(w["c"] - _level_refs(w["c"], level, w["rev"]))).astype(BF16)
                w["attn"] = jnp.where(w["lv"][...] == level, nt(w["qb"] * e, w["kb"] * e), w["attn"])
        for w in work:
            c, st = w["c"], w["st"]
            tot = c[0:1] if w["rev"] else c[HGRN_GROUP - 1:HGRN_GROUP]
            s_t = st[...]
            w["o"][w["rows"], :] = _dot(w["attn"].astype(BF16), w["v"]) + nt(w["q"] * jnp.exp2(c), s_t)
            kt = (w["k"] * jnp.exp2(tot - c)).astype(BF16)
            st[...] = s_t * jnp.exp2(tot) + lax.dot_general(w["v"], kt, TN_DIMS, preferred_element_type=F32)
        return carry

    lax.fori_loop(0, n_groups, group_step, 0, unroll=2)
    if want_state:
        sfin_ref[0] = stf[...].T
        sfin_ref[1] = stb[...].T
    o_ref[...] = _rms(of_scr[...] + ob_scr[...], gn_ref[...]) * gb_ref[...]


def _hgrn(q, kf, lf, kb, lbw, v, gb, gnorm, n_batch, s0=None, want_state=False):
    seq = q.shape[0] // n_batch
    tok = pl.BlockSpec((seq, LANES), lambda b, h: (b, h))
    const = pl.BlockSpec((HGRN_GROUP, HGRN_GROUP), lambda b, h: (0, 0))
    in_specs = [tok] * 7 + [pl.BlockSpec((1, LANES), lambda b, h: (0, 0))] + [const] * 2
    args = [q, kf, lf, kb, lbw, v, gb, gnorm.reshape(1, LANES)] + list(_hgrn_levels())
    if s0 is not None:
        state, layer = s0
        in_specs.append(pl.BlockSpec((None, None, 2, None, B_DK, B_DV), lambda b, h: (b, layer, 0, h, 0, 0)))
        args.append(state)
    out_specs = [tok]
    out_shape = [jax.ShapeDtypeStruct(q.shape, F32)]
    if want_state:
        out_specs.append(pl.BlockSpec((None, 2, None, B_DK, B_DV), lambda b, h: (b, 0, h, 0, 0)))
        out_shape.append(jax.ShapeDtypeStruct((n_batch, 2, B_HEADS, B_DK, B_DV), F32))
    res = pl.pallas_call(
        functools.partial(_hgrn_kernel, seq, s0 is not None, want_state),
        grid=(n_batch, B_HEADS),
        in_specs=in_specs,
        out_specs=out_specs,
        out_shape=out_shape,
        scratch_shapes=[pltpu.VMEM((seq, LANES), F32),
                        pltpu.VMEM((seq, LANES), F32),
                        pltpu.VMEM((B_DV, B_DK), F32),
                        pltpu.VMEM((B_DV, B_DK), F32)],
        compiler_params=_cparams(2),
        name="hgrn2",
    )(*args)
    return res if want_state else (res[0], None)


def _outffn_kernel(o1_ref, o2_ref, x_ref, mod_ref, wo_ref, ln_ref, wg_ref, wu_ref, wd_ref, out_ref):
    m = mod_ref[0]
    half = o1_ref.shape[1]
    y = (_dot(o1_ref[...].astype(BF16), wo_ref[0:half, :])
         + _dot(o2_ref[...].astype(BF16), wo_ref[half:2 * half, :]))
    x1 = _layer_norm(ALPHA * x_ref[...] + m[2:3] * y, ln_ref[0:1, :], ln_ref[1:2, :])
    h = (x1 * (1.0 + m[4:5]) + m[3:4]).astype(BF16)
    acc = None
    for a in range(0, D_FF, 512):
        b = min(a + 512, D_FF)
        act = (_silu(_dot(h, wg_ref[:, a:b])) * _dot(h, wu_ref[:, a:b])).astype(BF16)
        part = _dot(act, wd_ref[a:b, :])
        acc = part if acc is None else acc + part
    out_ref[...] = _layer_norm(ALPHA * x1 + m[5:6] * acc, ln_ref[2:3, :], ln_ref[3:4, :])


def _outffn(o1, o2, x, mod, wo_bf, ln, wg_bf, wu_bf, wd_bf, rows_per_cond, tm=512):
    m_rows = x.shape[0]
    half = o1.shape[1]
    consts = [wo_bf, ln, wg_bf, wu_bf, wd_bf]
    in_specs = [pl.BlockSpec((tm, half), lambda i: (i, 0)),
                pl.BlockSpec((tm, half), lambda i: (i, 0)),
                pl.BlockSpec((tm, D_MODEL), lambda i: (i, 0)),
                pl.BlockSpec((1, 6, D_MODEL), lambda i: (i * tm // rows_per_cond, 0, 0))]
    in_specs += [pl.BlockSpec(a.shape, lambda i: (0, 0)) for a in consts]
    return pl.pallas_call(
        _outffn_kernel,
        grid=(m_rows // tm,),
        in_specs=in_specs,
        out_specs=pl.BlockSpec((tm, D_MODEL), lambda i: (i, 0)),
        out_shape=jax.ShapeDtypeStruct((m_rows, D_MODEL), F32),
        compiler_params=_cparams(1),
        name="outproj_ffn",
    )(o1, o2, x, mod, *consts)


def _cache_spec(past, width, layer):
    return pl.BlockSpec((None, None, past, width), lambda b, i: (b, layer, 0, 0))


def _even_layer(xp, xs, modp, mods, w_in_bf, lb, sink, gnorm, tail_w, rope_hd,
                cache_k, cache_v, state, e, nbp, nbs):
    seq_p = xp.shape[0] // nbp
    seq_s = xs.shape[0] // nbs
    past = cache_k.shape[2]
    pp = _inproj_even(xp, modp, w_in_bf, lb, None, xp.shape[0])
    ps = _inproj_even(xs, mods, w_in_bf, lb, rope_hd, seq_s)
    qa, ka, va, qb, ib, kf, lf, kb, lbw, gb = pp
    oa_p = _attention(qa, ka, va, nbp, A_HEADS, A_HEADS // A_KV, seq_p, sink=sink, n_sub=PROMPT_SUB_A)
    ob_p, s_new = _hgrn(qb, kf, lf, kb, lbw, ib, gb, gnorm, nbp, want_state=True)
    xp = _outffn(oa_p, ob_p, xp, modp, *tail_w, xp.shape[0])
    new = (ka.reshape(nbp, seq_p, A_KV, HEAD_DIM), va.reshape(nbp, seq_p, A_KV, HEAD_DIM), s_new)
    qa, ka, va, qb, ib, kf, lf, kb, lbw, gb = ps
    spec = _cache_spec(past, A_KV * HEAD_DIM, e)
    oa_s = _attention(qa, ka, va, nbs, A_HEADS, A_HEADS // A_KV, BAND_TQ, band=True,
                      ctx=(cache_k, cache_v, spec, spec, past), sink=sink)
    ob_s, _ = _hgrn(qb, kf, lf, kb, lbw, ib, gb, gnorm, nbs, s0=(state, e))
    xs = _outffn(oa_s, ob_s, xs, mods, *tail_w, seq_s)
    return xp, xs, new


def _odd_layer(xp, xs, modp, mods, in_w, tail_w, rope_hd, rope_c,
               cache_ckv, cache_pe, cache_k, cache_v, o, nbp, nbs):
    seq_p = xp.shape[0] // nbp
    seq_s = xs.shape[0] // nbs
    past = cache_ckv.shape[2]
    wkv_bf = in_w[4]
    mla_mul = (C_NOPE + C_ROPE) ** -0.5 * LOG2E
    qc, ckv, kc, vc, pe, qd, kd, vd = _inproj_odd(xp, modp, *in_w, None, xp.shape[0])
    oc_p = _attention(qc, kc, vc, nbp, C_HEADS, 1, seq_p, wide=True, score_mul=mla_mul)
    od_p = _attention(qd, kd, vd, nbp, D_HEADS, D_HEADS // D_KV, seq_p, n_sub=PROMPT_SUB_D)
    xp = _outffn(oc_p, od_p, xp, modp, *tail_w, xp.shape[0])
    new = (ckv.reshape(nbp, seq_p, C_KV_LORA), pe[:, C_NOPE:C_NOPE + C_ROPE].reshape(nbp, seq_p, C_ROPE),
           kd.reshape(nbp, seq_p, D_KV, HEAD_DIM), vd.reshape(nbp, seq_p, D_KV, HEAD_DIM))
    qc, ckv, kc, vc, pe, qd, kd, vd = _inproj_odd(xs, mods, *in_w, rope_hd + rope_c, seq_s)
    pe_ctx = jnp.pad(cache_pe[:, o].reshape(nbs * past, C_ROPE), ((0, 0), (C_NOPE, LANES - C_NOPE - C_ROPE)))
    kcc, vcc = _kvup(cache_ckv[:, o].reshape(nbs * past, C_KV_LORA), pe_ctx, wkv_bf)
    oc_s = _attention(qc, kc, vc, nbs, C_HEADS, 1, SAMPLE_TQ, wide=True, score_mul=mla_mul,
                      ctx=(kcc, vcc, pl.BlockSpec((past, C_HEADS * LANES), lambda b, i: (b, 0)),
                           pl.BlockSpec((past, C_HEADS * C_V), lambda b, i: (b, 0)), past))
    spec = _cache_spec(past, D_KV * HEAD_DIM, o)
    od_s = _attention(qd, kd, vd, nbs, D_HEADS, D_HEADS // D_KV, SAMPLE_TQ, ctx=(cache_k, cache_v, spec, spec, past))
    xs = _outffn(oc_s, od_s, xs, mods, *tail_w, seq_s)
    return xp, xs, new


def kernel(x_prompt, x_sample, cache_a_k, cache_a_v, state_b, cache_c_kv, cache_c_pe, cache_d_k, cache_d_v, c, c_ctx, w_ada, b_ada, ln_g, ln_b, w_in_ab, a_sink, b_lb, b_gnorm, w_in_cd, c_q_norm, c_kv_norm, c_w_q_up, c_w_kv_up, d_q_norm, d_k_norm, w_out, w_ffn_gate, w_ffn_up, w_ffn_down):
    nbp, seq_p, _ = x_prompt.shape
    nbs, seq_s, _ = x_sample.shape
    past = cache_a_k.shape[2]
    xp = x_prompt.reshape(nbp * seq_p, D_MODEL)
    xs = x_sample.reshape(nbs * seq_s, D_MODEL)
    rope_hd = _rope_tables(seq_s, HEAD_DIM)
    cos_c, sin_c = _rope_tables(seq_s, C_ROPE)
    ones = jnp.ones((seq_s, C_NOPE), F32)
    rope_c = (jnp.concatenate([ones, cos_c[:, :C_NOPE]], axis=1),
              jnp.concatenate([0.0 * ones, sin_c[:, :C_ROPE], 0.0 * ones[:, :LANES - C_NOPE - C_ROPE]], axis=1))
    lb_w = jax.nn.softmax(b_lb.astype(F32), axis=0)
    lb_all = jnp.cumsum(lb_w, axis=0) - lb_w[:1]
    mod_all = _modulation(c, c_ctx, w_ada, b_ada)
    n_even = cache_a_k.shape[1]
    n_odd = cache_c_kv.shape[1]
    ca_k = cache_a_k.reshape(nbs, n_even, past, A_KV * HEAD_DIM)
    ca_v = cache_a_v.reshape(nbs, n_even, past, A_KV * HEAD_DIM)
    cd_k = cache_d_k.reshape(nbs, n_odd, past, D_KV * HEAD_DIM)
    cd_v = cache_d_v.reshape(nbs, n_odd, past, D_KV * HEAD_DIM)
    ak, av, sb, ckv, cpe, dk, dv = [], [], [], [], [], [], []
    for l in range(DEPTH):
        modp = mod_all[l, 0:1]
        mods = mod_all[l, 1:1 + nbs]
        ln = jnp.concatenate([ln_g[l, 0:1], ln_b[l, 0:1], ln_g[l, 1:2], ln_b[l, 1:2]], axis=0)
        tail_w = (_to_bf16(w_out, l), ln, _to_bf16(w_ffn_gate, l), _to_bf16(w_ffn_up, l), _to_bf16(w_ffn_down, l))
        if l % 2 == 0:
            e = l // 2
            xp, xs, (k_new, v_new, s_new) = _even_layer(
                xp, xs, modp, mods, _to_bf16(w_in_ab, e), lb_all[e], a_sink[e], b_gnorm[e], tail_w,
                rope_hd, ca_k, ca_v, state_b, e, nbp, nbs)
            ak.append(k_new)
            av.append(v_new)
            sb.append(s_new)
        else:
            o = l // 2
            w_in = w_in_cd[o]
            lat = C_Q_LORA + C_KV_LORA
            zeros = lambda n: jnp.zeros((D_MODEL, n), F32)
            w_in = jnp.concatenate([w_in[:, :lat], zeros(C_NOPE), w_in[:, lat:lat + C_ROPE],
                                    zeros(LANES - C_NOPE - C_ROPE), w_in[:, lat + C_ROPE:]], axis=1).astype(BF16)
            wq = c_w_q_up[o].reshape(C_Q_LORA, C_HEADS, C_NOPE + C_ROPE)
            wq = jnp.pad(wq, ((0, 0), (0, 0), (0, LANES - C_NOPE - C_ROPE))).reshape(C_Q_LORA, -1).astype(BF16)
            wkv = c_w_kv_up[o].reshape(C_KV_LORA, C_HEADS, C_NOPE + C_V)
            wk = jnp.pad(wkv[:, :, :C_NOPE], ((0, 0), (0, 0), (0, LANES - C_NOPE))).reshape(C_KV_LORA, -1)
            wkv = jnp.concatenate([wk, wkv[:, :, C_NOPE:].reshape(C_KV_LORA, -1)], axis=1).astype(BF16)
            in_w = (w_in, c_q_norm[o].reshape(1, -1), c_kv_norm[o].reshape(1, -1), wq, wkv,
                    jnp.tile(d_q_norm[o], D_HEADS).reshape(1, -1), jnp.tile(d_k_norm[o], D_KV).reshape(1, -1))
            xp, xs, (c_new, pe_new, k_new, v_new) = _odd_layer(
                xp, xs, modp, mods, in_w, tail_w, rope_hd, rope_c, cache_c_kv, cache_c_pe, cd_k, cd_v, o, nbp, nbs)
            ckv.append(c_new)
            cpe.append(pe_new)
            dk.append(k_new)
            dv.append(v_new)
    return (xp.reshape(nbp, seq_p, D_MODEL), xs.reshape(nbs, seq_s, D_MODEL),
            jnp.stack(ak, axis=1), jnp.stack(av, axis=1), jnp.stack(sb, axis=1),
            jnp.stack(ckv, axis=1), jnp.stack(cpe, axis=1), jnp.stack(dk, axis=1), jnp.stack(dv, axis=1))
```

```python
import functools

import numpy as np
import jax
import jax.numpy as jnp
from jax import lax
from jax.experimental import pallas as pl
from jax.experimental.pallas import tpu as pltpu

F32 = jnp.float32
BF16 = jnp.bfloat16

D_MODEL = 1024
DEPTH = 4
GRID_W = 64
HEAD_DIM = 64
ROPE_BASE = 10000.0
MASK_VALUE = -1e30
F_FLOOR = 1e-30
A_HEADS = 8
A_KV = 2
WINDOW = 128
B_HEADS = 4
B_DK = 128
B_DV = 128
C_HEADS = 8
C_Q_LORA = 384
C_KV_LORA = 256
C_NOPE = 64
C_ROPE = 32
C_V = 64
D_HEADS = 8
D_KV = 4
D_FF = 2816
ALPHA = (2 * DEPTH) ** 0.25

LANES = 128
HGRN_GROUP = 128
LOG2E = 1.4426950408889634
SAMPLE_TQ = 512
PROMPT_SUB_A = 4
PROMPT_SUB_D = 2
SAMPLE_QT = 2
BAND_QT = 4
BAND_TQ = 256
VMEM_LIMIT = 56 * 1024 * 1024
NT_DIMS = (((1,), (1,)), ((), ()))
TN_DIMS = (((0,), (0,)), ((), ()))


def _cparams(n_axes):
    return pltpu.CompilerParams(dimension_semantics=("arbitrary",) * n_axes,
                                vmem_limit_bytes=VMEM_LIMIT)


def _silu(x):
    return x * jax.nn.sigmoid(x)


def _dot(a, b):
    return jnp.dot(a, b, preferred_element_type=F32)


def _layer_norm(x, g, b):
    mu = jnp.mean(x, axis=-1, keepdims=True)
    xc = x - mu
    var = jnp.mean(xc * xc, axis=-1, keepdims=True)
    return xc * lax.rsqrt(var + 1e-5) * g + b


def _rms(x, g):
    return x * lax.rsqrt(jnp.mean(x * x, axis=-1, keepdims=True) + 1e-6) * g


def _rms_heads64(x, g):
    outs = []
    for j in range(x.shape[1] // LANES):
        xg = x[:, j * LANES:(j + 1) * LANES]
        x2 = xg * xg
        lo = lax.broadcasted_iota(jnp.int32, xg.shape, 1) < HEAD_DIM
        s_lo = jnp.sum(jnp.where(lo, x2, 0.0), axis=-1, keepdims=True)
        s_hi = jnp.sum(jnp.where(lo, 0.0, x2), axis=-1, keepdims=True)
        ms = jnp.where(lo, s_lo, s_hi) * (1.0 / HEAD_DIM)
        outs.append(xg * lax.rsqrt(ms + 1e-6))
    return jnp.concatenate(outs, axis=1) * g


def _rope(x, cos, sin, half):
    outs = []
    for j in range(x.shape[1] // LANES):
        xg = x[:, j * LANES:(j + 1) * LANES]
        lane = lax.broadcasted_iota(jnp.int32, xg.shape, 1)
        up = pltpu.roll(xg, LANES - half, 1)
        down = pltpu.roll(xg, half, 1)
        rot = jnp.where((lane & (2 * half - 1)) < half, up, down)
        outs.append(xg * cos + rot * sin)
    return outs[0] if len(outs) == 1 else jnp.concatenate(outs, axis=1)


def _rope_tables(n_tokens, rot_dim):
    n_rows = n_tokens // GRID_W
    row = jnp.broadcast_to(jnp.arange(n_rows, dtype=F32)[:, None], (n_rows, GRID_W)).reshape(-1)
    col = jnp.broadcast_to(jnp.arange(GRID_W, dtype=F32)[None, :], (n_rows, GRID_W)).reshape(-1)
    quarter = rot_dim // 4
    inv = ROPE_BASE ** (-jnp.arange(quarter, dtype=F32) / quarter)
    ar = row[:, None] * inv
    ac = col[:, None] * inv
    cos = jnp.concatenate([jnp.cos(ar), jnp.cos(ar), jnp.cos(ac), jnp.cos(ac)], axis=-1)
    sin = jnp.concatenate([-jnp.sin(ar), jnp.sin(ar), -jnp.sin(ac), jnp.sin(ac)], axis=-1)
    reps = LANES // rot_dim
    return jnp.tile(cos, (1, reps)), jnp.tile(sin, (1, reps))


def _cast_kernel(x_ref, o_ref):
    o_ref[...] = x_ref[...].astype(BF16)


def _to_bf16(w, layer, tm=256):
    _, rows, cols = w.shape
    return pl.pallas_call(
        _cast_kernel,
        grid=(rows // tm,),
        in_specs=[pl.BlockSpec((None, tm, cols), lambda i: (layer, i, 0))],
        out_specs=pl.BlockSpec((tm, cols), lambda i: (i, 0)),
        out_shape=jax.ShapeDtypeStruct((rows, cols), BF16),
        compiler_params=_cparams(1),
        name="to_bf16",
    )(w)


def _mod_kernel(c_ref, w_ref, b_ref, o_ref):
    a = _silu(c_ref[...]).astype(BF16)
    o_ref[0] = _dot(a, w_ref[0].astype(BF16)) + b_ref[0]


def _modulation(c, c_ctx, w_ada, b_ada):
    nb = c.shape[0]
    cond = jnp.zeros((16, D_MODEL), F32).at[0].set(c_ctx).at[1:1 + nb].set(c)
    tn = 1536
    out = pl.pallas_call(
        _mod_kernel,
        grid=(DEPTH, 6 * D_MODEL // tn),
        in_specs=[pl.BlockSpec((16, D_MODEL), lambda l, j: (0, 0)),
                  pl.BlockSpec((1, D_MODEL, tn), lambda l, j: (l, 0, j)),
                  pl.BlockSpec((1, 1, tn), lambda l, j: (l, 0, j))],
        out_specs=pl.BlockSpec((1, 16, tn), lambda l, j: (l, 0, j)),
        out_shape=jax.ShapeDtypeStruct((DEPTH, 16, 6 * D_MODEL), F32),
        compiler_params=_cparams(2),
        name="modulation",
    )(cond, w_ada, b_ada.reshape(DEPTH, 1, 6 * D_MODEL))
    return out.reshape(DEPTH, 16, 6, D_MODEL)


def _modulate(x_ref, mod_ref, shift_row):
    m = mod_ref[0]
    return (x_ref[...] * (1.0 + m[shift_row + 1:shift_row + 2]) + m[shift_row:shift_row + 1]).astype(BF16)


def _inproj_even_kernel(rope, *refs):
    if rope:
        x_ref, mod_ref, w_ref, lb_ref, cos_ref, sin_ref = refs[:6]
        outs = refs[6:]
    else:
        x_ref, mod_ref, w_ref, lb_ref = refs[:4]
        outs = refs[4:]
    qa_ref, ka_ref, va_ref, qb_ref, ib_ref, kf_ref, lf_ref, kb_ref, lbw_ref, gb_ref = outs
    h = _modulate(x_ref, mod_ref, 0)

    def proj(a, b):
        return _dot(h, w_ref[:, a:b])

    qa = proj(0, 512)
    ka = proj(512, 640)
    if rope:
        cos = cos_ref[...]
        sin = sin_ref[...]
        qa = _rope(qa, cos, sin, HEAD_DIM // 4)
        ka = _rope(ka, cos, sin, HEAD_DIM // 4)
    qa_ref[...] = qa
    ka_ref[...] = ka
    va_ref[...] = proj(640, 768)
    qb_ref[...] = _silu(proj(768, 1280))
    ib_ref[...] = proj(1280, 1792)
    for d, (k_ref, l_ref) in enumerate(((kf_ref, lf_ref), (kb_ref, lbw_ref))):
        lb = lb_ref[d:d + 1, :]
        f = lb + (1.0 - lb) * jax.nn.sigmoid(proj(1792 + 512 * d, 2304 + 512 * d))
        l_ref[...] = jnp.log(jnp.maximum(f, F_FLOOR))
        k_ref[...] = 1.0 - f
    gb_ref[...] = _silu(proj(2816, 3328))


def _inproj_even(x, mod, w_bf, lb, rope_tabs, rows_per_cond, tm=256):
    m_rows = x.shape[0]
    rope = rope_tabs is not None
    widths = (512, 128, 128, 512, 512, 512, 512, 512, 512, 512)
    in_specs = [pl.BlockSpec((tm, D_MODEL), lambda i: (i, 0)),
                pl.BlockSpec((1, 6, D_MODEL), lambda i: (i * tm // rows_per_cond, 0, 0)),
                pl.BlockSpec(w_bf.shape, lambda i: (0, 0)),
                pl.BlockSpec(lb.shape, lambda i: (0, 0))]
    args = [x, mod, w_bf, lb]
    if rope:
        nblk = rope_tabs[0].shape[0] // tm
        in_specs += [pl.BlockSpec((tm, LANES), lambda i: (i % nblk, 0))] * 2
        args += list(rope_tabs)
    return pl.pallas_call(
        functools.partial(_inproj_even_kernel, rope),
        grid=(m_rows // tm,),
        in_specs=in_specs,
        out_specs=[pl.BlockSpec((tm, w), lambda i: (i, 0)) for w in widths],
        out_shape=[jax.ShapeDtypeStruct((m_rows, w), F32) for w in widths],
        compiler_params=_cparams(1),
        name="inproj_even",
    )(*args)


def _inproj_odd_kernel(rope, *refs):
    if rope:
        (x_ref, mod_ref, w_ref, gcq_ref, gckv_ref, wq_ref, wkv_ref, gdq_ref, gdk_ref,
         cosd_ref, sind_ref, cosc_ref, sinc_ref) = refs[:13]
        outs = refs[13:]
    else:
        x_ref, mod_ref, w_ref, gcq_ref, gckv_ref, wq_ref, wkv_ref, gdq_ref, gdk_ref = refs[:9]
        outs = refs[9:]
    qc_ref, ckv_ref, kc_ref, vc_ref, pe_ref, qd_ref, kd_ref, vd_ref = outs
    h = _modulate(x_ref, mod_ref, 0)

    def proj(a, b):
        return _dot(h, w_ref[:, a:b])

    cq = _rms(proj(0, 384), gcq_ref[...]).astype(BF16)
    qc = _dot(cq, wq_ref[...])
    ckv = _rms(proj(384, 640), gckv_ref[...])
    ckv_ref[...] = ckv
    ckv_bf = ckv.astype(BF16)
    kn = _dot(ckv_bf, wkv_ref[:, 0:C_HEADS * LANES])
    vc_ref[...] = _dot(ckv_bf, wkv_ref[:, C_HEADS * LANES:]).astype(BF16)
    pe = proj(640, 768)
    qd = _rms_heads64(proj(768, 1280), gdq_ref[...])
    kd = _rms_heads64(proj(1280, 1536), gdk_ref[...])
    if rope:
        qc = _rope(qc, cosc_ref[...], sinc_ref[...], C_ROPE // 4)
        pe = _rope(pe, cosc_ref[...], sinc_ref[...], C_ROPE // 4)
        qd = _rope(qd, cosd_ref[...], sind_ref[...], HEAD_DIM // 4)
        kd = _rope(kd, cosd_ref[...], sind_ref[...], HEAD_DIM // 4)
    qc_ref[...] = qc.astype(BF16)
    pe_ref[...] = pe
    kc_ref[...] = jnp.concatenate([kn[:, j * LANES:(j + 1) * LANES] + pe for j in range(C_HEADS)],
                                  axis=1).astype(BF16)
    qd_ref[...] = qd
    kd_ref[...] = kd
    vd_ref[...] = proj(1536, 1792)


def _inproj_odd(x, mod, w_bf, gcq, gckv, wq_bf, wkv_bf, gdq, gdk, rope_tabs, rows_per_cond, tm=512):
    m_rows = x.shape[0]
    rope = rope_tabs is not None
    outs = ((C_HEADS * LANES, BF16), (C_KV_LORA, F32), (C_HEADS * LANES, BF16), (C_HEADS * C_V, BF16),
            (LANES, F32), (D_HEADS * HEAD_DIM, F32), (D_KV * HEAD_DIM, F32), (D_KV * HEAD_DIM, F32))
    consts = [w_bf, gcq, gckv, wq_bf, wkv_bf, gdq, gdk]
    in_specs = [pl.BlockSpec((tm, D_MODEL), lambda i: (i, 0)),
                pl.BlockSpec((1, 6, D_MODEL), lambda i: (i * tm // rows_per_cond, 0, 0))]
    in_specs += [pl.BlockSpec(a.shape, lambda i: (0, 0)) for a in consts]
    args = [x, mod] + consts
    if rope:
        nblk = rope_tabs[0].shape[0] // tm
        in_specs += [pl.BlockSpec((tm, LANES), lambda i: (i % nblk, 0))] * 4
        args += list(rope_tabs)
    return pl.pallas_call(
        functools.partial(_inproj_odd_kernel, rope),
        grid=(m_rows // tm,),
        in_specs=in_specs,
        out_specs=[pl.BlockSpec((tm, w), lambda i: (i, 0)) for w, _ in outs],
        out_shape=[jax.ShapeDtypeStruct((m_rows, w), dt) for w, dt in outs],
        compiler_params=_cparams(1),
        name="inproj_odd",
    )(*args)


def _kvup_kernel(x_ref, pe_ref, w_ref, kc_ref, vc_ref):
    x = x_ref[...].astype(BF16)
    kn = _dot(x, w_ref[:, 0:C_HEADS * LANES])
    pe = pe_ref[...]
    kc_ref[...] = jnp.concatenate([kn[:, j * LANES:(j + 1) * LANES] + pe for j in range(C_HEADS)],
                                  axis=1).astype(BF16)
    vc_ref[...] = _dot(x, w_ref[:, C_HEADS * LANES:]).astype(BF16)


def _kvup(lat, pe_slab, wkv_bf, tm=512):
    m_rows = lat.shape[0]
    return pl.pallas_call(
        _kvup_kernel,
        grid=(m_rows // tm,),
        in_specs=[pl.BlockSpec((tm, C_KV_LORA), lambda i: (i, 0)),
                  pl.BlockSpec((tm, LANES), lambda i: (i, 0)),
                  pl.BlockSpec(wkv_bf.shape, lambda i: (0, 0))],
        out_specs=[pl.BlockSpec((tm, C_HEADS * LANES), lambda i: (i, 0)),
                   pl.BlockSpec((tm, C_HEADS * C_V), lambda i: (i, 0))],
        out_shape=[jax.ShapeDtypeStruct((m_rows, C_HEADS * LANES), BF16),
                   jax.ShapeDtypeStruct((m_rows, C_HEADS * C_V), BF16)],
        compiler_params=_cparams(1),
        name="kv_up_ctx",
    )(lat, pe_slab, wkv_bf)


def _attn_kernel(tq, n_heads, group, seq_k, past, band, has_sink, wide, score_mul, n_sub, n_qt, *refs):
    q_ref, k_ref, v_ref = refs[:3]
    idx = 3
    if past:
        kc_ref, vc_ref = refs[idx:idx + 2]
        idx += 2
    if has_sink:
        sink_ref = refs[idx]
        idx += 1
    o_ref, kpad, vaug = refs[idx:idx + 3]
    d = HEAD_DIM
    n_kv = n_heads // group
    n_slots = n_heads if wide else 2 * n_kv
    total = seq_k + past
    qi = pl.program_id(1)

    def _prepare_keys_values():
        for sub in range(n_sub):
            parts = [(k_ref, v_ref, sub * seq_k, 0, seq_k)]
            if past:
                parts.append((kc_ref, vc_ref, 0, seq_k, past))
            base = sub * n_slots
            for ks_ref, vs_ref, src0, r0, n in parts:
                src = slice(src0, src0 + n)
                rows = slice(r0, r0 + n)
                lo = lax.broadcasted_iota(jnp.int32, (n, LANES), 1) < d
                for j in range(n_kv * d // LANES):
                    cols = slice(j * LANES, (j + 1) * LANES)
                    vs = vs_ref[src, cols].astype(F32)
                    if wide:
                        vaug[base + 2 * j, rows, :] = jnp.where(lo, vs, 1.0).astype(BF16)
                        vaug[base + 2 * j + 1, rows, :] = jnp.where(lo, 1.0, vs).astype(BF16)
                        continue
                    vr = pltpu.roll(vs, d, 1)
                    ks = ks_ref[src, cols]
                    kr = pltpu.roll(ks, d, 1)
                    for g, (k_even, k_odd, v_even, v_odd) in ((2 * j, (ks, kr, vs, vr)), (2 * j + 1, (kr, ks, vr, vs))):
                        kpad[base + 2 * g, rows, :] = jnp.where(lo, k_even, 0.0).astype(BF16)
                        kpad[base + 2 * g + 1, rows, :] = jnp.where(lo, 0.0, k_odd).astype(BF16)
                        vaug[base + 2 * g, rows, :] = jnp.where(lo, v_even, 1.0).astype(BF16)
                        vaug[base + 2 * g + 1, rows, :] = jnp.where(lo, 1.0, v_odd).astype(BF16)
                if wide:
                    for h in range(n_heads):
                        kpad[base + h, rows, :] = ks_ref[src, h * LANES:(h + 1) * LANES]

    if seq_k == tq * n_qt:
        _prepare_keys_values()
    else:
        pl.when(qi == 0)(_prepare_keys_values)

    def key_window(tile):
        if not band:
            return [slice(0, total)], None
        kb = tq + 2 * WINDOW
        start = pl.multiple_of(jnp.clip(tile * tq - WINDOW, 0, seq_k - kb), LANES)
        diff = ((tile * tq - start) + lax.broadcasted_iota(jnp.int32, (tq, kb), 0)
                - lax.broadcasted_iota(jnp.int32, (tq, kb), 1))
        return [pl.ds(start, kb), pl.ds(seq_k, past)], jnp.abs(diff) <= WINDOW

    def nt(a, b):
        return lax.dot_general(a, b, NT_DIMS, preferred_element_type=F32)

    lo_out = lax.broadcasted_iota(jnp.int32, (tq, LANES), 1) < d
    for unit in range(n_sub * n_qt):
        base = (unit // n_qt) * n_slots
        q_rows = slice(unit * tq, (unit + 1) * tq)
        key_rows, bmask = key_window(qi * n_qt + unit % n_qt)
        q_slabs = {}
        results = {}
        for g in range(n_kv):
            for parity in (0, 1):
                heads = [h for h in range(g * group, (g + 1) * group) if h % 2 == parity]
                if not heads:
                    continue
                probs = [[] for _ in key_rows]
                maxes = []
                for h in heads:
                    slot = base + (h if wide else 2 * g + parity)
                    qs = h if wide else h // 2
                    if qs not in q_slabs:
                        q = q_ref[q_rows, qs * LANES:(qs + 1) * LANES]
                        q_slabs[qs] = q if wide else (q * d ** -0.5).astype(BF16)
                    scores = [nt(q_slabs[qs], kpad[slot, r, :]) for r in key_rows]
                    if score_mul is not None:
                        scores = [s * score_mul for s in scores]
                    if band:
                        scores[0] = jnp.where(bmask, scores[0], MASK_VALUE)
                    m = jnp.max(scores[0], axis=-1, keepdims=True)
                    for s in scores[1:]:
                        m = jnp.maximum(m, jnp.max(s, axis=-1, keepdims=True))
                    if has_sink:
                        m = jnp.maximum(m, sink_ref[h])
                    maxes.append(m)
                    for part, s in zip(probs, scores):
                        e = jnp.exp(s - m) if score_mul is None else jnp.exp2(s - m)
                        part.append(e.astype(BF16))
                vslot = base + (heads[0] if wide else 2 * g + parity)
                o_aug = None
                for part, r in zip(probs, key_rows):
                    p = part[0] if len(part) == 1 else jnp.concatenate(part, axis=0)
                    o = _dot(p, vaug[vslot, r, :])
                    o_aug = o if o_aug is None else o_aug + o
                den = pltpu.roll(o_aug, d, 1)
                for i, h in enumerate(heads):
                    rows = slice(i * tq, (i + 1) * tq)
                    den_h = den[rows]
                    if has_sink:
                        den_h = den_h + jnp.exp(sink_ref[h] - maxes[i])
                    results[h] = o_aug[rows] / den_h
        for j in range(n_heads // 2):
            o_ref[q_rows, j * LANES:(j + 1) * LANES] = jnp.where(lo_out, results[2 * j], results[2 * j + 1])


def _attention(q, k, v, n_batch, n_heads, group, tq, *, band=False, ctx=None, sink=None, wide=False,
               score_mul=None, n_sub=1, n_qt=1):
    seq = q.shape[0] // n_batch
    nq = seq // (tq * n_qt)
    n_kv = n_heads // group
    assert n_sub == 1 or (n_qt == 1 and nq == 1 and ctx is None and n_batch % n_sub == 0)
    in_specs = [pl.BlockSpec((n_sub * n_qt * tq, q.shape[1]), lambda b, i: (b * nq + i, 0)),
                pl.BlockSpec((n_sub * seq, k.shape[1]), lambda b, i: (b, 0)),
                pl.BlockSpec((n_sub * seq, v.shape[1]), lambda b, i: (b, 0))]
    args = [q, k, v]
    past = 0
    if ctx is not None:
        kc, vc, kc_spec, vc_spec, past = ctx
        in_specs += [kc_spec, vc_spec]
        args += [kc, vc]
    if sink is not None:
        in_specs.append(pl.BlockSpec(memory_space=pltpu.SMEM))
        args.append(sink)
    n_slots = n_sub * (n_heads if wide else 2 * n_kv)
    return pl.pallas_call(
        functools.partial(_attn_kernel, tq, n_heads, group, seq, past, band, sink is not None, wide, score_mul,
                          n_sub, n_qt),
        grid=(n_batch // n_sub, nq),
        in_specs=in_specs,
        out_specs=pl.BlockSpec((n_sub * n_qt * tq, n_heads * HEAD_DIM), lambda b, i: (b * nq + i, 0)),
        out_shape=jax.ShapeDtypeStruct((q.shape[0], n_heads * HEAD_DIM), F32),
        scratch_shapes=[pltpu.VMEM((n_slots, seq + past, LANES), BF16),
                        pltpu.VMEM((n_slots, seq + past, LANES), BF16)],
        compiler_params=_cparams(2),
        name="mla_attention" if wide else "gqa_attention",
    )(*args)


def _hgrn_levels():
    i = np.arange(HGRN_GROUP)
    t, s = i[:, None], i[None, :]
    level = np.where(t != s, np.floor(np.log2(np.maximum(t ^ s, 1))) + 1, 0).astype(np.int32)
    return (jnp.asarray(np.where(s <= t, level, -1), jnp.int32),
            jnp.asarray(np.where(s >= t, level, -1), jnp.int32))


def _scan_rows(x, rev):
    n = x.shape[0]
    row = lax.broadcasted_iota(jnp.int32, x.shape, 0)
    step = 1
    while step < n:
        if step < 8:
            if rev:
                x = x + jnp.where(row < n - step, pltpu.roll(x, n - step, 0), 0.0)
            else:
                x = x + jnp.where(row >= step, pltpu.roll(x, step, 0), 0.0)
        elif rev:
            x = jnp.concatenate([x[:n - step] + x[step:], x[n - step:]], axis=0)
        else:
            x = jnp.concatenate([x[:step], x[step:] + x[:n - step]], axis=0)
        step *= 2
    return x


def _level_refs(c, level, rev):
    n = c.shape[0]
    half = 1 << (level - 1)
    bs = 2 * half
    off = half if rev else half - 1
    if bs >= 16:
        return jnp.concatenate([jnp.broadcast_to(c[i * bs + off:i * bs + off + 1, :], (bs, LANES))
                                for i in range(n // bs)], axis=0)
    c3 = c.reshape(n // 8, 8, LANES)
    sub = lax.broadcasted_iota(jnp.int32, c3.shape, 1)
    out = None
    for j in reversed(range(8 // bs)):
        b = jnp.broadcast_to(c3[:, j * bs + off:j * bs + off + 1, :], c3.shape)
        out = b if out is None else jnp.where(sub < (j + 1) * bs, b, out)
    return out.reshape(n, LANES)


def _neg_abs(x):
    return lax.bitcast_convert_type(lax.bitcast_convert_type(x, jnp.uint32) | jnp.uint32(0x80000000), F32)


def _hgrn_kernel(seq, has_s0, want_state, *refs):
    q_ref, kf_ref, lf_ref, kb_ref, lbw_ref, v_ref, gb_ref, gn_ref, lvf_ref, lvb_ref = refs[:10]
    idx = 10
    if has_s0:
        s0_ref = refs[idx]
        idx += 1
    o_ref = refs[idx]
    idx += 1
    if want_state:
        sfin_ref = refs[idx]
        idx += 1
    of_scr, ob_scr, stf, stb = refs[idx:idx + 4]
    n_groups = seq // HGRN_GROUP
    scale = B_DK ** -0.5
    n_levels = HGRN_GROUP.bit_length() - 1
    dirs = ((kf_ref, lf_ref, lvf_ref, of_scr, stf, False),
            (kb_ref, lbw_ref, lvb_ref, ob_scr, stb, True))

    for d, st in enumerate((stf, stb)):
        st[...] = s0_ref[d].T if has_s0 else jnp.zeros((B_DV, B_DK), F32)

    def nt(a, b):
        return lax.dot_general(a.astype(BF16), b.astype(BF16), NT_DIMS, preferred_element_type=F32)

    def group_step(i, carry):
        work = []
        for k_ref, l_ref, lv_ref, o_scr, st, rev in dirs:
            g = (n_groups - 1 - i) if rev else i
            rows = pl.ds(pl.multiple_of(g * HGRN_GROUP, HGRN_GROUP), HGRN_GROUP)
            logf = l_ref[rows, :] * LOG2E
            c = _scan_rows(logf, rev)
            work.append(dict(rows=rows, c=c, logf=logf, q=q_ref[rows, :] * scale, k=k_ref[rows, :],
                             v=v_ref[rows, :].astype(BF16), lv=lv_ref, o=o_scr, st=st, rev=rev))
        for w in work:
            w["qb"] = w["q"].astype(BF16)
            w["kb"] = w["k"].astype(BF16)
            w["attn"] = jnp.where(w["lv"][...] == 0, nt(w["qb"], w["kb"]), 0.0)
        odd = (lax.broadcasted_iota(jnp.int32, (HGRN_GROUP, LANES), 0) & 1) == 1
        for level in range(1, n_levels + 1):
            for w in work:
                if level == 1:
                    arg = jnp.where(odd, 0.0, w["logf"]) if w["rev"] else jnp.where(odd, w["logf"], 0.0)
                else:
                    arg = _neg_abs(w["c"] - _level_refs(w["c"], level, w["rev"]))
                e = jnp.exp2(arg).astype(BF16)
                w["attn"] = jnp.where(w["lv"][...] == level, nt(w["qb"] * e, w["kb"] * e), w["attn"])
        for w in work:
            c, st = w["c"], w["st"]
            tot = c[0:1] if w["rev"] else c[HGRN_GROUP - 1:HGRN_GROUP]
            s_t = st[...]
            w["o"][w["rows"], :] = _dot(w["attn"].astype(BF16), w["v"]) + nt(w["q"] * jnp.exp2(c), s_t)
            kt = (w["k"] * jnp.exp2(tot - c)).astype(BF16)
            st[...] = s_t * jnp.exp2(tot) + lax.dot_general(w["v"], kt, TN_DIMS, preferred_element_type=F32)
        return carry

    lax.fori_loop(0, n_groups, group_step, 0, unroll=2)
    if want_state:
        sfin_ref[0] = stf[...].T
        sfin_ref[1] = stb[...].T
    o_ref[...] = _rms(of_scr[...] + ob_scr[...], gn_ref[...]) * gb_ref[...]


def _hgrn(q, kf, lf, kb, lbw, v, gb, gnorm, n_batch, s0=None, want_state=False):
    seq = q.shape[0] // n_batch
    tok = pl.BlockSpec((seq, LANES), lambda b, h: (b, h))
    const = pl.BlockSpec((HGRN_GROUP, HGRN_GROUP), lambda b, h: (0, 0))
    in_specs = [tok] * 7 + [pl.BlockSpec((1, LANES), lambda b, h: (0, 0))] + [const] * 2
    args = [q, kf, lf, kb, lbw, v, gb, gnorm.reshape(1, LANES)] + list(_hgrn_levels())
    if s0 is not None:
        state, layer = s0
        in_specs.append(pl.BlockSpec((None, None, 2, None, B_DK, B_DV), lambda b, h: (b, layer, 0, h, 0, 0)))
        args.append(state)
    out_specs = [tok]
    out_shape = [jax.ShapeDtypeStruct(q.shape, F32)]
    if want_state:
        out_specs.append(pl.BlockSpec((None, 2, None, B_DK, B_DV), lambda b, h: (b, 0, h, 0, 0)))
        out_shape.append(jax.ShapeDtypeStruct((n_batch, 2, B_HEADS, B_DK, B_DV), F32))
    res = pl.pallas_call(
        functools.partial(_hgrn_kernel, seq, s0 is not None, want_state),
        grid=(n_batch, B_HEADS),
        in_specs=in_specs,
        out_specs=out_specs,
        out_shape=out_shape,
        scratch_shapes=[pltpu.VMEM((seq, LANES), F32),
                        pltpu.VMEM((seq, LANES), F32),
                        pltpu.VMEM((B_DV, B_DK), F32),
                        pltpu.VMEM((B_DV, B_DK), F32)],
        compiler_params=_cparams(2),
        name="hgrn2",
    )(*args)
    return res if want_state else (res[0], None)


def _outffn_kernel(o1_ref, o2_ref, x_ref, mod_ref, wo_ref, ln_ref, wg_ref, wu_ref, wd_ref, out_ref):
    m = mod_ref[0]
    half = o1_ref.shape[1]
    y = (_dot(o1_ref[...].astype(BF16), wo_ref[0:half, :])
         + _dot(o2_ref[...].astype(BF16), wo_ref[half:2 * half, :]))
    x1 = _layer_norm(ALPHA * x_ref[...] + m[2:3] * y, ln_ref[0:1, :], ln_ref[1:2, :])
    h = (x1 * (1.0 + m[4:5]) + m[3:4]).astype(BF16)
    acc = None
    for a in range(0, D_FF, 512):
        b = min(a + 512, D_FF)
        act = (_silu(_dot(h, wg_ref[:, a:b])) * _dot(h, wu_ref[:, a:b])).astype(BF16)
        part = _dot(act, wd_ref[a:b, :])
        acc = part if acc is None else acc + part
    out_ref[...] = _layer_norm(ALPHA * x1 + m[5:6] * acc, ln_ref[2:3, :], ln_ref[3:4, :])


def _outffn(o1, o2, x, mod, wo_bf, ln, wg_bf, wu_bf, wd_bf, rows_per_cond, tm=512):
    m_rows = x.shape[0]
    half = o1.shape[1]
    consts = [wo_bf, ln, wg_bf, wu_bf, wd_bf]
    in_specs = [pl.BlockSpec((tm, half), lambda i: (i, 0)),
                pl.BlockSpec((tm, half), lambda i: (i, 0)),
                pl.BlockSpec((tm, D_MODEL), lambda i: (i, 0)),
                pl.BlockSpec((1, 6, D_MODEL), lambda i: (i * tm // rows_per_cond, 0, 0))]
    in_specs += [pl.BlockSpec(a.shape, lambda i: (0, 0)) for a in consts]
    return pl.pallas_call(
        _outffn_kernel,
        grid=(m_rows // tm,),
        in_specs=in_specs,
        out_specs=pl.BlockSpec((tm, D_MODEL), lambda i: (i, 0)),
        out_shape=jax.ShapeDtypeStruct((m_rows, D_MODEL), F32),
        compiler_params=_cparams(1),
        name="outproj_ffn",
    )(o1, o2, x, mod, *consts)


def _cache_spec(past, width, layer):
    return pl.BlockSpec((None, None, past, width), lambda b, i: (b, layer, 0, 0))


def _even_layer(xp, xs, modp, mods, w_in_bf, lb, sink, gnorm, tail_w, rope_hd,
                cache_k, cache_v, state, e, nbp, nbs):
    seq_p = xp.shape[0] // nbp
    seq_s = xs.shape[0] // nbs
    past = cache_k.shape[2]
    pp = _inproj_even(xp, modp, w_in_bf, lb, None, xp.shape[0])
    ps = _inproj_even(xs, mods, w_in_bf, lb, rope_hd, seq_s)
    qa, ka, va, qb, ib, kf, lf, kb, lbw, gb = pp
    oa_p = _attention(qa, ka, va, nbp, A_HEADS, A_HEADS // A_KV, seq_p, sink=sink, n_sub=PROMPT_SUB_A)
    ob_p, s_new = _hgrn(qb, kf, lf, kb, lbw, ib, gb, gnorm, nbp, want_state=True)
    xp = _outffn(oa_p, ob_p, xp, modp, *tail_w, xp.shape[0])
    new = (ka.reshape(nbp, seq_p, A_KV, HEAD_DIM), va.reshape(nbp, seq_p, A_KV, HEAD_DIM), s_new)
    qa, ka, va, qb, ib, kf, lf, kb, lbw, gb = ps
    spec = _cache_spec(past, A_KV * HEAD_DIM, e)
    oa_s = _attention(qa, ka, va, nbs, A_HEADS, A_HEADS // A_KV, BAND_TQ, band=True,
                      ctx=(cache_k, cache_v, spec, spec, past), sink=sink, n_qt=BAND_QT)
    ob_s, _ = _hgrn(qb, kf, lf, kb, lbw, ib, gb, gnorm, nbs, s0=(state, e))
    xs = _outffn(oa_s, ob_s, xs, mods, *tail_w, seq_s)
    return xp, xs, new


def _odd_layer(xp, xs, modp, mods, in_w, tail_w, rope_hd, rope_c,
               cache_ckv, cache_pe, cache_k, cache_v, o, nbp, nbs):
    seq_p = xp.shape[0] // nbp
    seq_s = xs.shape[0] // nbs
    past = cache_ckv.shape[2]
    wkv_bf = in_w[4]
    mla_mul = (C_NOPE + C_ROPE) ** -0.5 * LOG2E
    qc, ckv, kc, vc, pe, qd, kd, vd = _inproj_odd(xp, modp, *in_w, None, xp.shape[0])
    oc_p = _attention(qc, kc, vc, nbp, C_HEADS, 1, seq_p, wide=True, score_mul=mla_mul)
    od_p = _attention(qd, kd, vd, nbp, D_HEADS, D_HEADS // D_KV, seq_p, n_sub=PROMPT_SUB_D)
    xp = _outffn(oc_p, od_p, xp, modp, *tail_w, xp.shape[0])
    new = (ckv.reshape(nbp, seq_p, C_KV_LORA), pe[:, C_NOPE:C_NOPE + C_ROPE].reshape(nbp, seq_p, C_ROPE),
           kd.reshape(nbp, seq_p, D_KV, HEAD_DIM), vd.reshape(nbp, seq_p, D_KV, HEAD_DIM))
    qc, ckv, kc, vc, pe, qd, kd, vd = _inproj_odd(xs, mods, *in_w, rope_hd + rope_c, seq_s)
    pe_ctx = jnp.pad(cache_pe[:, o].reshape(nbs * past, C_ROPE), ((0, 0), (C_NOPE, LANES - C_NOPE - C_ROPE)))
    kcc, vcc = _kvup(cache_ckv[:, o].reshape(nbs * past, C_KV_LORA), pe_ctx, wkv_bf)
    oc_s = _attention(qc, kc, vc, nbs, C_HEADS, 1, SAMPLE_TQ, wide=True, score_mul=mla_mul,
                      ctx=(kcc, vcc, pl.BlockSpec((past, C_HEADS * LANES), lambda b, i: (b, 0)),
                           pl.BlockSpec((past, C_HEADS * C_V), lambda b, i: (b, 0)), past), n_qt=SAMPLE_QT)
    spec = _cache_spec(past, D_KV * HEAD_DIM, o)
    od_s = _attention(qd, kd, vd, nbs, D_HEADS, D_HEADS // D_KV, SAMPLE_TQ, ctx=(cache_k, cache_v, spec, spec, past),
                      n_qt=SAMPLE_QT)
    xs = _outffn(oc_s, od_s, xs, mods, *tail_w, seq_s)
    return xp, xs, new


def kernel(x_prompt, x_sample, cache_a_k, cache_a_v, state_b, cache_c_kv, cache_c_pe, cache_d_k, cache_d_v, c, c_ctx, w_ada, b_ada, ln_g, ln_b, w_in_ab, a_sink, b_lb, b_gnorm, w_in_cd, c_q_norm, c_kv_norm, c_w_q_up, c_w_kv_up, d_q_norm, d_k_norm, w_out, w_ffn_gate, w_ffn_up, w_ffn_down):
    nbp, seq_p, _ = x_prompt.shape
    nbs, seq_s, _ = x_sample.shape
    past = cache_a_k.shape[2]
    xp = x_prompt.reshape(nbp * seq_p, D_MODEL)
    xs = x_sample.reshape(nbs * seq_s, D_MODEL)
    rope_hd = _rope_tables(seq_s, HEAD_DIM)
    cos_c, sin_c = _rope_tables(seq_s, C_ROPE)
    ones = jnp.ones((seq_s, C_NOPE), F32)
    rope_c = (jnp.concatenate([ones, cos_c[:, :C_NOPE]], axis=1),
              jnp.concatenate([0.0 * ones, sin_c[:, :C_ROPE], 0.0 * ones[:, :LANES - C_NOPE - C_ROPE]], axis=1))
    lb_w = jax.nn.softmax(b_lb.astype(F32), axis=0)
    lb_all = jnp.cumsum(lb_w, axis=0) - lb_w[:1]
    mod_all = _modulation(c, c_ctx, w_ada, b_ada)
    n_even = cache_a_k.shape[1]
    n_odd = cache_c_kv.shape[1]
    ca_k = cache_a_k.reshape(nbs, n_even, past, A_KV * HEAD_DIM)
    ca_v = cache_a_v.reshape(nbs, n_even, past, A_KV * HEAD_DIM)
    cd_k = cache_d_k.reshape(nbs, n_odd, past, D_KV * HEAD_DIM)
    cd_v = cache_d_v.reshape(nbs, n_odd, past, D_KV * HEAD_DIM)
    ak, av, sb, ckv, cpe, dk, dv = [], [], [], [], [], [], []
    for l in range(DEPTH):
        modp = mod_all[l, 0:1]
        mods = mod_all[l, 1:1 + nbs]
        ln = jnp.concatenate([ln_g[l, 0:1], ln_b[l, 0:1], ln_g[l, 1:2], ln_b[l, 1:2]], axis=0)
        tail_w = (_to_bf16(w_out, l), ln, _to_bf16(w_ffn_gate, l), _to_bf16(w_ffn_up, l), _to_bf16(w_ffn_down, l))
        if l % 2 == 0:
            e = l // 2
            xp, xs, (k_new, v_new, s_new) = _even_layer(
                xp, xs, modp, mods, _to_bf16(w_in_ab, e), lb_all[e], a_sink[e], b_gnorm[e], tail_w,
                rope_hd, ca_k, ca_v, state_b, e, nbp, nbs)
            ak.append(k_new)
            av.append(v_new)
            sb.append(s_new)
        else:
            o = l // 2
            w_in = w_in_cd[o]
            lat = C_Q_LORA + C_KV_LORA
            zeros = lambda n: jnp.zeros((D_MODEL, n), F32)
            w_in = jnp.concatenate([w_in[:, :lat], zeros(C_NOPE), w_in[:, lat:lat + C_ROPE],
                                    zeros(LANES - C_NOPE - C_ROPE), w_in[:, lat + C_ROPE:]], axis=1).astype(BF16)
            wq = c_w_q_up[o].reshape(C_Q_LORA, C_HEADS, C_NOPE + C_ROPE)
            wq = jnp.pad(wq, ((0, 0), (0, 0), (0, LANES - C_NOPE - C_ROPE))).reshape(C_Q_LORA, -1).astype(BF16)
            wkv = c_w_kv_up[o].reshape(C_KV_LORA, C_HEADS, C_NOPE + C_V)
            wk = jnp.pad(wkv[:, :, :C_NOPE], ((0, 0), (0, 0), (0, LANES - C_NOPE))).reshape(C_KV_LORA, -1)
            wkv = jnp.concatenate([wk, wkv[:, :, C_NOPE:].reshape(C_KV_LORA, -1)], axis=1).astype(BF16)
            in_w = (w_in, c_q_norm[o].reshape(1, -1), c_kv_norm[o].reshape(1, -1), wq, wkv,
                    jnp.tile(d_q_norm[o], D_HEADS).reshape(1, -1), jnp.tile(d_k_norm[o], D_KV).reshape(1, -1))
            xp, xs, (c_new, pe_new, k_new, v_new) = _odd_layer(
                xp, xs, modp, mods, in_w, tail_w, rope_hd, rope_c, cache_c_kv, cache_c_pe, cd_k, cd_v, o, nbp, nbs)
            ckv.append(c_new)
            cpe.append(pe_new)
            dk.append(k_new)
            dv.append(v_new)
    return (xp.reshape(nbp, seq_p, D_MODEL), xs.reshape(nbs, seq_s, D_MODEL),
            jnp.stack(ak, axis=1), jnp.stack(av, axis=1), jnp.stack(sb, axis=1),
            jnp.stack(ckv, axis=1), jnp.stack(cpe, axis=1), jnp.stack(dk, axis=1), jnp.stack(dv, axis=1))
```

```python
import functools

import numpy as np
import jax
import jax.numpy as jnp
from jax import lax
from jax.experimental import pallas as pl
from jax.experimental.pallas import tpu as pltpu

F32 = jnp.float32
BF16 = jnp.bfloat16

D_MODEL = 1024
DEPTH = 4
GRID_W = 64
HEAD_DIM = 64
ROPE_BASE = 10000.0
MASK_VALUE = -1e30
F_FLOOR = 1e-30
A_HEADS = 8
A_KV = 2
WINDOW = 128
B_HEADS = 4
B_DK = 128
B_DV = 128
C_HEADS = 8
C_Q_LORA = 384
C_KV_LORA = 256
C_NOPE = 64
C_ROPE = 32
C_V = 64
D_HEADS = 8
D_KV = 4
D_FF = 2816
ALPHA = (2 * DEPTH) ** 0.25

LANES = 128
HGRN_GROUP = 128
LOG2E = 1.4426950408889634
SAMPLE_TQ = 512
PROMPT_HGRN_HEADS = 4
PROMPT_SUB_A = 4
PROMPT_SUB_D = 2
SAMPLE_QT = 2
BAND_QT = 4
BAND_TQ = 256
VMEM_LIMIT = 56 * 1024 * 1024
NT_DIMS = (((1,), (1,)), ((), ()))
TN_DIMS = (((0,), (0,)), ((), ()))


def _cparams(n_axes):
    return pltpu.CompilerParams(dimension_semantics=("arbitrary",) * n_axes,
                                vmem_limit_bytes=VMEM_LIMIT)


def _silu(x):
    return x * jax.nn.sigmoid(x)


def _dot(a, b):
    return jnp.dot(a, b, preferred_element_type=F32)


def _layer_norm(x, g, b):
    mu = jnp.mean(x, axis=-1, keepdims=True)
    xc = x - mu
    var = jnp.mean(xc * xc, axis=-1, keepdims=True)
    return xc * lax.rsqrt(var + 1e-5) * g + b


def _rms(x, g):
    return x * lax.rsqrt(jnp.mean(x * x, axis=-1, keepdims=True) + 1e-6) * g


def _rms_heads64(x, g):
    outs = []
    for j in range(x.shape[1] // LANES):
        xg = x[:, j * LANES:(j + 1) * LANES]
        x2 = xg * xg
        lo = lax.broadcasted_iota(jnp.int32, xg.shape, 1) < HEAD_DIM
        s_lo = jnp.sum(jnp.where(lo, x2, 0.0), axis=-1, keepdims=True)
        s_hi = jnp.sum(jnp.where(lo, 0.0, x2), axis=-1, keepdims=True)
        ms = jnp.where(lo, s_lo, s_hi) * (1.0 / HEAD_DIM)
        outs.append(xg * lax.rsqrt(ms + 1e-6))
    return jnp.concatenate(outs, axis=1) * g


def _rope(x, cos, sin, half):
    outs = []
    for j in range(x.shape[1] // LANES):
        xg = x[:, j * LANES:(j + 1) * LANES]
        lane = lax.broadcasted_iota(jnp.int32, xg.shape, 1)
        up = pltpu.roll(xg, LANES - half, 1)
        down = pltpu.roll(xg, half, 1)
        rot = jnp.where((lane & (2 * half - 1)) < half, up, down)
        outs.append(xg * cos + rot * sin)
    return outs[0] if len(outs) == 1 else jnp.concatenate(outs, axis=1)


def _rope_tables(n_tokens, rot_dim):
    n_rows = n_tokens // GRID_W
    row = jnp.broadcast_to(jnp.arange(n_rows, dtype=F32)[:, None], (n_rows, GRID_W)).reshape(-1)
    col = jnp.broadcast_to(jnp.arange(GRID_W, dtype=F32)[None, :], (n_rows, GRID_W)).reshape(-1)
    quarter = rot_dim // 4
    inv = ROPE_BASE ** (-jnp.arange(quarter, dtype=F32) / quarter)
    ar = row[:, None] * inv
    ac = col[:, None] * inv
    cos = jnp.concatenate([jnp.cos(ar), jnp.cos(ar), jnp.cos(ac), jnp.cos(ac)], axis=-1)
    sin = jnp.concatenate([-jnp.sin(ar), jnp.sin(ar), -jnp.sin(ac), jnp.sin(ac)], axis=-1)
    reps = LANES // rot_dim
    return jnp.tile(cos, (1, reps)), jnp.tile(sin, (1, reps))


def _cast_kernel(x_ref, o_ref):
    o_ref[...] = x_ref[...].astype(BF16)


def _to_bf16(w, layer):
    _, rows, cols = w.shape
    tm = rows // 4 if rows % 64 == 0 else rows
    return pl.pallas_call(
        _cast_kernel,
        grid=(rows // tm,),
        in_specs=[pl.BlockSpec((None, tm, cols), lambda i: (layer, i, 0))],
        out_specs=pl.BlockSpec((tm, cols), lambda i: (i, 0)),
        out_shape=jax.ShapeDtypeStruct((rows, cols), BF16),
        compiler_params=_cparams(1),
        name="to_bf16",
    )(w)


def _mod_kernel(c_ref, w_ref, b_ref, o_ref):
    a = _silu(c_ref[...]).astype(BF16)
    o_ref[0] = _dot(a, w_ref[0].astype(BF16)) + b_ref[0]


def _modulation(c, c_ctx, w_ada, b_ada):
    nb = c.shape[0]
    cond = jnp.zeros((16, D_MODEL), F32).at[0].set(c_ctx).at[1:1 + nb].set(c)
    tn = 1536
    out = pl.pallas_call(
        _mod_kernel,
        grid=(DEPTH, 6 * D_MODEL // tn),
        in_specs=[pl.BlockSpec((16, D_MODEL), lambda l, j: (0, 0)),
                  pl.BlockSpec((1, D_MODEL, tn), lambda l, j: (l, 0, j)),
                  pl.BlockSpec((1, 1, tn), lambda l, j: (l, 0, j))],
        out_specs=pl.BlockSpec((1, 16, tn), lambda l, j: (l, 0, j)),
        out_shape=jax.ShapeDtypeStruct((DEPTH, 16, 6 * D_MODEL), F32),
        compiler_params=_cparams(2),
        name="modulation",
    )(cond, w_ada, b_ada.reshape(DEPTH, 1, 6 * D_MODEL))
    return out.reshape(DEPTH, 16, 6, D_MODEL)


def _modulate(x_ref, mod_ref, shift_row):
    m = mod_ref[0]
    return (x_ref[...] * (1.0 + m[shift_row + 1:shift_row + 2]) + m[shift_row:shift_row + 1]).astype(BF16)


def _inproj_even_kernel(rope, *refs):
    if rope:
        x_ref, mod_ref, w_ref, lb_ref, cos_ref, sin_ref = refs[:6]
        outs = refs[6:]
    else:
        x_ref, mod_ref, w_ref, lb_ref = refs[:4]
        outs = refs[4:]
    qa_ref, ka_ref, va_ref, qb_ref, ib_ref, kf_ref, lf_ref, kb_ref, lbw_ref, gb_ref = outs
    h = _modulate(x_ref, mod_ref, 0)

    def proj(a, b):
        return _dot(h, w_ref[:, a:b])

    qa = proj(0, 512)
    ka = proj(512, 640)
    if rope:
        cos = cos_ref[...]
        sin = sin_ref[...]
        qa = _rope(qa, cos, sin, HEAD_DIM // 4)
        ka = _rope(ka, cos, sin, HEAD_DIM // 4)
    qa_ref[...] = qa
    ka_ref[...] = ka
    va_ref[...] = proj(640, 768)
    qb_ref[...] = _silu(proj(768, 1280))
    ib_ref[...] = proj(1280, 1792)
    for d, (k_ref, l_ref) in enumerate(((kf_ref, lf_ref), (kb_ref, lbw_ref))):
        lb = lb_ref[d:d + 1, :]
        f = lb + (1.0 - lb) * jax.nn.sigmoid(proj(1792 + 512 * d, 2304 + 512 * d))
        l_ref[...] = jnp.log(jnp.maximum(f, F_FLOOR))
        k_ref[...] = 1.0 - f
    gb_ref[...] = _silu(proj(2816, 3328))


def _inproj_even(x, mod, w_bf, lb, rope_tabs, rows_per_cond, tm=256):
    m_rows = x.shape[0]
    rope = rope_tabs is not None
    widths = (512, 128, 128, 512, 512, 512, 512, 512, 512, 512)
    in_specs = [pl.BlockSpec((tm, D_MODEL), lambda i: (i, 0)),
                pl.BlockSpec((1, 6, D_MODEL), lambda i: (i * tm // rows_per_cond, 0, 0)),
                pl.BlockSpec(w_bf.shape, lambda i: (0, 0)),
                pl.BlockSpec(lb.shape, lambda i: (0, 0))]
    args = [x, mod, w_bf, lb]
    if rope:
        nblk = rope_tabs[0].shape[0] // tm
        in_specs += [pl.BlockSpec((tm, LANES), lambda i: (i % nblk, 0))] * 2
        args += list(rope_tabs)
    return pl.pallas_call(
        functools.partial(_inproj_even_kernel, rope),
        grid=(m_rows // tm,),
        in_specs=in_specs,
        out_specs=[pl.BlockSpec((tm, w), lambda i: (i, 0)) for w in widths],
        out_shape=[jax.ShapeDtypeStruct((m_rows, w), F32) for w in widths],
        compiler_params=_cparams(1),
        name="inproj_even",
    )(*args)


def _inproj_odd_kernel(rope, *refs):
    if rope:
        (x_ref, mod_ref, w_ref, gcq_ref, gckv_ref, wq_ref, wkv_ref, gdq_ref, gdk_ref,
         cosd_ref, sind_ref, cosc_ref, sinc_ref) = refs[:13]
        outs = refs[13:]
    else:
        x_ref, mod_ref, w_ref, gcq_ref, gckv_ref, wq_ref, wkv_ref, gdq_ref, gdk_ref = refs[:9]
        outs = refs[9:]
    qc_ref, ckv_ref, kc_ref, vc_ref, pe_ref, qd_ref, kd_ref, vd_ref = outs
    h = _modulate(x_ref, mod_ref, 0)

    def proj(a, b):
        return _dot(h, w_ref[:, a:b])

    cq = _rms(proj(0, 384), gcq_ref[...]).astype(BF16)
    qc = _dot(cq, wq_ref[...])
    ckv = _rms(proj(384, 640), gckv_ref[...])
    ckv_ref[...] = ckv
    ckv_bf = ckv.astype(BF16)
    kn = _dot(ckv_bf, wkv_ref[:, 0:C_HEADS * LANES])
    vc_ref[...] = _dot(ckv_bf, wkv_ref[:, C_HEADS * LANES:]).astype(BF16)
    pe = proj(640, 768)
    qd = _rms_heads64(proj(768, 1280), gdq_ref[...])
    kd = _rms_heads64(proj(1280, 1536), gdk_ref[...])
    if rope:
        qc = _rope(qc, cosc_ref[...], sinc_ref[...], C_ROPE // 4)
        pe = _rope(pe, cosc_ref[...], sinc_ref[...], C_ROPE // 4)
        qd = _rope(qd, cosd_ref[...], sind_ref[...], HEAD_DIM // 4)
        kd = _rope(kd, cosd_ref[...], sind_ref[...], HEAD_DIM // 4)
    qc_ref[...] = qc.astype(BF16)
    pe_ref[...] = pe
    kc_ref[...] = jnp.concatenate([kn[:, j * LANES:(j + 1) * LANES] + pe for j in range(C_HEADS)],
                                  axis=1).astype(BF16)
    qd_ref[...] = qd
    kd_ref[...] = kd
    vd_ref[...] = proj(1536, 1792)


def _inproj_odd(x, mod, w_bf, gcq, gckv, wq_bf, wkv_bf, gdq, gdk, rope_tabs, rows_per_cond, tm=512):
    m_rows = x.shape[0]
    rope = rope_tabs is not None
    outs = ((C_HEADS * LANES, BF16), (C_KV_LORA, F32), (C_HEADS * LANES, BF16), (C_HEADS * C_V, BF16),
            (LANES, F32), (D_HEADS * HEAD_DIM, F32), (D_KV * HEAD_DIM, F32), (D_KV * HEAD_DIM, F32))
    consts = [w_bf, gcq, gckv, wq_bf, wkv_bf, gdq, gdk]
    in_specs = [pl.BlockSpec((tm, D_MODEL), lambda i: (i, 0)),
                pl.BlockSpec((1, 6, D_MODEL), lambda i: (i * tm // rows_per_cond, 0, 0))]
    in_specs += [pl.BlockSpec(a.shape, lambda i: (0, 0)) for a in consts]
    args = [x, mod] + consts
    if rope:
        nblk = rope_tabs[0].shape[0] // tm
        in_specs += [pl.BlockSpec((tm, LANES), lambda i: (i % nblk, 0))] * 4
        args += list(rope_tabs)
    return pl.pallas_call(
        functools.partial(_inproj_odd_kernel, rope),
        grid=(m_rows // tm,),
        in_specs=in_specs,
        out_specs=[pl.BlockSpec((tm, w), lambda i: (i, 0)) for w, _ in outs],
        out_shape=[jax.ShapeDtypeStruct((m_rows, w), dt) for w, dt in outs],
        compiler_params=_cparams(1),
        name="inproj_odd",
    )(*args)


def _kvup_kernel(x_ref, pe_ref, w_ref, kc_ref, vc_ref):
    x = x_ref[...].astype(BF16)
    kn = _dot(x, w_ref[:, 0:C_HEADS * LANES])
    pe = pe_ref[...]
    kc_ref[...] = jnp.concatenate([kn[:, j * LANES:(j + 1) * LANES] + pe for j in range(C_HEADS)],
                                  axis=1).astype(BF16)
    vc_ref[...] = _dot(x, w_ref[:, C_HEADS * LANES:]).astype(BF16)


def _kvup(lat, pe_slab, wkv_bf, tm=512):
    m_rows = lat.shape[0]
    return pl.pallas_call(
        _kvup_kernel,
        grid=(m_rows // tm,),
        in_specs=[pl.BlockSpec((tm, C_KV_LORA), lambda i: (i, 0)),
                  pl.BlockSpec((tm, LANES), lambda i: (i, 0)),
                  pl.BlockSpec(wkv_bf.shape, lambda i: (0, 0))],
        out_specs=[pl.BlockSpec((tm, C_HEADS * LANES), lambda i: (i, 0)),
                   pl.BlockSpec((tm, C_HEADS * C_V), lambda i: (i, 0))],
        out_shape=[jax.ShapeDtypeStruct((m_rows, C_HEADS * LANES), BF16),
                   jax.ShapeDtypeStruct((m_rows, C_HEADS * C_V), BF16)],
        compiler_params=_cparams(1),
        name="kv_up_ctx",
    )(lat, pe_slab, wkv_bf)


def _attn_kernel(tq, n_heads, group, seq_k, past, band, has_sink, wide, score_mul, n_sub, n_qt, *refs):
    q_ref, k_ref, v_ref = refs[:3]
    idx = 3
    if past:
        kc_ref, vc_ref = refs[idx:idx + 2]
        idx += 2
    if has_sink:
        sink_ref = refs[idx]
        idx += 1
    o_ref, kpad, vaug = refs[idx:idx + 3]
    d = HEAD_DIM
    n_kv = n_heads // group
    n_slots = n_heads if wide else 2 * n_kv
    total = seq_k + past
    qi = pl.program_id(1)

    def _prepare_keys_values():
        for sub in range(n_sub):
            parts = [(k_ref, v_ref, sub * seq_k, 0, seq_k)]
            if past:
                parts.append((kc_ref, vc_ref, 0, seq_k, past))
            base = sub * n_slots
            for ks_ref, vs_ref, src0, r0, n in parts:
                src = slice(src0, src0 + n)
                rows = slice(r0, r0 + n)
                lo = lax.broadcasted_iota(jnp.int32, (n, LANES), 1) < d
                for j in range(n_kv * d // LANES):
                    cols = slice(j * LANES, (j + 1) * LANES)
                    vs = vs_ref[src, cols].astype(F32)
                    if wide:
                        vaug[base + 2 * j, rows, :] = jnp.where(lo, vs, 1.0).astype(BF16)
                        vaug[base + 2 * j + 1, rows, :] = jnp.where(lo, 1.0, vs).astype(BF16)
                        continue
                    vr = pltpu.roll(vs, d, 1)
                    ks = ks_ref[src, cols]
                    kr = pltpu.roll(ks, d, 1)
                    for g, (k_even, k_odd, v_even, v_odd) in ((2 * j, (ks, kr, vs, vr)), (2 * j + 1, (kr, ks, vr, vs))):
                        kpad[base + 2 * g, rows, :] = jnp.where(lo, k_even, 0.0).astype(BF16)
                        kpad[base + 2 * g + 1, rows, :] = jnp.where(lo, 0.0, k_odd).astype(BF16)
                        vaug[base + 2 * g, rows, :] = jnp.where(lo, v_even, 1.0).astype(BF16)
                        vaug[base + 2 * g + 1, rows, :] = jnp.where(lo, 1.0, v_odd).astype(BF16)
                if wide:
                    for h in range(n_heads):
                        kpad[base + h, rows, :] = ks_ref[src, h * LANES:(h + 1) * LANES]

    if seq_k == tq * n_qt:
        _prepare_keys_values()
    else:
        pl.when(qi == 0)(_prepare_keys_values)

    def key_window(tile):
        if not band:
            return [slice(0, total)], None
        kb = tq + 2 * WINDOW
        start = pl.multiple_of(jnp.clip(tile * tq - WINDOW, 0, seq_k - kb), LANES)
        diff = ((tile * tq - start) + lax.broadcasted_iota(jnp.int32, (tq, kb), 0)
                - lax.broadcasted_iota(jnp.int32, (tq, kb), 1))
        return [pl.ds(start, kb), pl.ds(seq_k, past)], jnp.abs(diff) <= WINDOW

    def nt(a, b):
        return lax.dot_general(a, b, NT_DIMS, preferred_element_type=F32)

    lo_out = lax.broadcasted_iota(jnp.int32, (tq, LANES), 1) < d
    for unit in range(n_sub * n_qt):
        base = (unit // n_qt) * n_slots
        q_rows = slice(unit * tq, (unit + 1) * tq)
        key_rows, bmask = key_window(qi * n_qt + unit % n_qt)
        q_slabs = {}
        results = {}
        for g in range(n_kv):
            for parity in (0, 1):
                heads = [h for h in range(g * group, (g + 1) * group) if h % 2 == parity]
                if not heads:
                    continue
                probs = [[] for _ in key_rows]
                maxes = []
                for h in heads:
                    slot = base + (h if wide else 2 * g + parity)
                    qs = h if wide else h // 2
                    if qs not in q_slabs:
                        q = q_ref[q_rows, qs * LANES:(qs + 1) * LANES]
                        q_slabs[qs] = q if wide else (q * d ** -0.5).astype(BF16)
                    scores = [nt(q_slabs[qs], kpad[slot, r, :]) for r in key_rows]
                    if score_mul is not None:
                        scores = [s * score_mul for s in scores]
                    if band:
                        scores[0] = jnp.where(bmask, scores[0], MASK_VALUE)
                    m = jnp.max(scores[0], axis=-1, keepdims=True)
                    for s in scores[1:]:
                        m = jnp.maximum(m, jnp.max(s, axis=-1, keepdims=True))
                    if has_sink:
                        m = jnp.maximum(m, sink_ref[h])
                    maxes.append(m)
                    for part, s in zip(probs, scores):
                        e = jnp.exp(s - m) if score_mul is None else jnp.exp2(s - m)
                        part.append(e.astype(BF16))
                vslot = base + (heads[0] if wide else 2 * g + parity)
                o_aug = None
                for part, r in zip(probs, key_rows):
                    p = part[0] if len(part) == 1 else jnp.concatenate(part, axis=0)
                    o = _dot(p, vaug[vslot, r, :])
                    o_aug = o if o_aug is None else o_aug + o
                den = pltpu.roll(o_aug, d, 1)
                for i, h in enumerate(heads):
                    rows = slice(i * tq, (i + 1) * tq)
                    den_h = den[rows]
                    if has_sink:
                        den_h = den_h + jnp.exp(sink_ref[h] - maxes[i])
                    results[h] = o_aug[rows] / den_h
        for j in range(n_heads // 2):
            o_ref[q_rows, j * LANES:(j + 1) * LANES] = jnp.where(lo_out, results[2 * j], results[2 * j + 1])


def _attention(q, k, v, n_batch, n_heads, group, tq, *, band=False, ctx=None, sink=None, wide=False,
               score_mul=None, n_sub=1, n_qt=1):
    seq = q.shape[0] // n_batch
    nq = seq // (tq * n_qt)
    n_kv = n_heads // group
    assert n_sub == 1 or (n_qt == 1 and nq == 1 and ctx is None and n_batch % n_sub == 0)
    in_specs = [pl.BlockSpec((n_sub * n_qt * tq, q.shape[1]), lambda b, i: (b * nq + i, 0)),
                pl.BlockSpec((n_sub * seq, k.shape[1]), lambda b, i: (b, 0)),
                pl.BlockSpec((n_sub * seq, v.shape[1]), lambda b, i: (b, 0))]
    args = [q, k, v]
    past = 0
    if ctx is not None:
        kc, vc, kc_spec, vc_spec, past = ctx
        in_specs += [kc_spec, vc_spec]
        args += [kc, vc]
    if sink is not None:
        in_specs.append(pl.BlockSpec(memory_space=pltpu.SMEM))
        args.append(sink)
    n_slots = n_sub * (n_heads if wide else 2 * n_kv)
    return pl.pallas_call(
        functools.partial(_attn_kernel, tq, n_heads, group, seq, past, band, sink is not None, wide, score_mul,
                          n_sub, n_qt),
        grid=(n_batch // n_sub, nq),
        in_specs=in_specs,
        out_specs=pl.BlockSpec((n_sub * n_qt * tq, n_heads * HEAD_DIM), lambda b, i: (b * nq + i, 0)),
        out_shape=jax.ShapeDtypeStruct((q.shape[0], n_heads * HEAD_DIM), F32),
        scratch_shapes=[pltpu.VMEM((n_slots, seq + past, LANES), BF16),
                        pltpu.VMEM((n_slots, seq + past, LANES), BF16)],
        compiler_params=_cparams(2),
        name="mla_attention" if wide else "gqa_attention",
    )(*args)


def _hgrn_levels():
    i = np.arange(HGRN_GROUP)
    t, s = i[:, None], i[None, :]
    level = np.where(t != s, np.floor(np.log2(np.maximum(t ^ s, 1))) + 1, 0).astype(np.int32)
    return (jnp.asarray(np.where(s <= t, level, -1), jnp.int32),
            jnp.asarray(np.where(s >= t, level, -1), jnp.int32))


def _scan_rows(x, rev):
    n = x.shape[0]
    row = lax.broadcasted_iota(jnp.int32, x.shape, 0)
    step = 1
    while step < n:
        if step < 8:
            if rev:
                x = x + jnp.where(row < n - step, pltpu.roll(x, n - step, 0), 0.0)
            else:
                x = x + jnp.where(row >= step, pltpu.roll(x, step, 0), 0.0)
        elif rev:
            x = jnp.concatenate([x[:n - step] + x[step:], x[n - step:]], axis=0)
        else:
            x = jnp.concatenate([x[:step], x[step:] + x[:n - step]], axis=0)
        step *= 2
    return x


def _level_refs(c, level, rev):
    n = c.shape[0]
    half = 1 << (level - 1)
    bs = 2 * half
    off = half if rev else half - 1
    if bs >= 16:
        return jnp.concatenate([jnp.broadcast_to(c[i * bs + off:i * bs + off + 1, :], (bs, LANES))
                                for i in range(n // bs)], axis=0)
    c3 = c.reshape(n // 8, 8, LANES)
    sub = lax.broadcasted_iota(jnp.int32, c3.shape, 1)
    out = None
    for j in reversed(range(8 // bs)):
        b = jnp.broadcast_to(c3[:, j * bs + off:j * bs + off + 1, :], c3.shape)
        out = b if out is None else jnp.where(sub < (j + 1) * bs, b, out)
    return out.reshape(n, LANES)


def _neg_abs(x):
    return lax.bitcast_convert_type(lax.bitcast_convert_type(x, jnp.uint32) | jnp.uint32(0x80000000), F32)


def _hgrn_kernel(seq, n_h, has_s0, want_state, *refs):
    q_ref, kf_ref, lf_ref, kb_ref, lbw_ref, v_ref, gb_ref, gn_ref, lvf_ref, lvb_ref = refs[:10]
    idx = 10
    if has_s0:
        s0_ref = refs[idx]
        idx += 1
    o_ref = refs[idx]
    idx += 1
    if want_state:
        sfin_ref = refs[idx]
        idx += 1
    of_scr, ob_scr, stf, stb = refs[idx:idx + 4]
    n_groups = seq // HGRN_GROUP
    scale = B_DK ** -0.5
    n_levels = HGRN_GROUP.bit_length() - 1
    chains = [(hh, d) + spec for hh in range(n_h)
              for d, spec in enumerate(((kf_ref, lf_ref, lvf_ref, of_scr, stf, False),
                                        (kb_ref, lbw_ref, lvb_ref, ob_scr, stb, True)))]

    for hh, d, _, _, _, _, st, _ in chains:
        st[hh] = s0_ref[d, hh].T if has_s0 else jnp.zeros((B_DV, B_DK), F32)

    def nt(a, b):
        return lax.dot_general(a.astype(BF16), b.astype(BF16), NT_DIMS, preferred_element_type=F32)

    def group_step(i, carry):
        work = []
        for hh, d, k_ref, l_ref, lv_ref, o_scr, st, rev in chains:
            g = (n_groups - 1 - i) if rev else i
            rows = pl.ds(pl.multiple_of(g * HGRN_GROUP, HGRN_GROUP), HGRN_GROUP)
            cols = slice(hh * LANES, (hh + 1) * LANES)
            logf = l_ref[rows, cols] * LOG2E
            c = _scan_rows(logf, rev)
            work.append(dict(rows=rows, cols=cols, hh=hh, c=c, logf=logf, q=q_ref[rows, cols] * scale,
                             k=k_ref[rows, cols], v=v_ref[rows, cols].astype(BF16), lv=lv_ref, o=o_scr, st=st,
                             rev=rev))
        for w in work:
            w["qb"] = w["q"].astype(BF16)
            w["kb"] = w["k"].astype(BF16)
            w["attn"] = jnp.where(w["lv"][...] == 0, nt(w["qb"], w["kb"]), 0.0)
        odd = (lax.broadcasted_iota(jnp.int32, (HGRN_GROUP, LANES), 0) & 1) == 1
        for level in range(1, n_levels + 1):
            for w in work:
                if level == 1:
                    arg = jnp.where(odd, 0.0, w["logf"]) if w["rev"] else jnp.where(odd, w["logf"], 0.0)
                else:
                    arg = _neg_abs(w["c"] - _level_refs(w["c"], level, w["rev"]))
                e = jnp.exp2(arg).astype(BF16)
                w["attn"] = jnp.where(w["lv"][...] == level, nt(w["qb"] * e, w["kb"] * e), w["attn"])
        for w in work:
            c, st, hh = w["c"], w["st"], w["hh"]
            tot = c[0:1] if w["rev"] else c[HGRN_GROUP - 1:HGRN_GROUP]
            s_t = st[hh]
            w["o"][w["rows"], w["cols"]] = _dot(w["attn"].astype(BF16), w["v"]) + nt(w["q"] * jnp.exp2(c), s_t)
            kt = (w["k"] * jnp.exp2(tot - c)).astype(BF16)
            st[hh] = s_t * jnp.exp2(tot) + lax.dot_general(w["v"], kt, TN_DIMS, preferred_element_type=F32)
        return carry

    lax.fori_loop(0, n_groups, group_step, 0, unroll=True)
    for hh in range(n_h):
        cols = slice(hh * LANES, (hh + 1) * LANES)
        if want_state:
            sfin_ref[0, hh] = stf[hh].T
            sfin_ref[1, hh] = stb[hh].T
        o_ref[:, cols] = _rms(of_scr[:, cols] + ob_scr[:, cols], gn_ref[...]) * gb_ref[:, cols]


def _hgrn(q, kf, lf, kb, lbw, v, gb, gnorm, n_batch, s0=None, want_state=False, n_h=1):
    seq = q.shape[0] // n_batch
    tok = pl.BlockSpec((seq, n_h * LANES), lambda b, h: (b, h))
    const = pl.BlockSpec((HGRN_GROUP, HGRN_GROUP), lambda b, h: (0, 0))
    in_specs = [tok] * 7 + [pl.BlockSpec((1, LANES), lambda b, h: (0, 0))] + [const] * 2
    args = [q, kf, lf, kb, lbw, v, gb, gnorm.reshape(1, LANES)] + list(_hgrn_levels())
    if s0 is not None:
        state, layer = s0
        in_specs.append(pl.BlockSpec((None, None, 2, n_h, B_DK, B_DV), lambda b, h: (b, layer, 0, h, 0, 0)))
        args.append(state)
    out_specs = [tok]
    out_shape = [jax.ShapeDtypeStruct(q.shape, F32)]
    if want_state:
        out_specs.append(pl.BlockSpec((None, 2, n_h, B_DK, B_DV), lambda b, h: (b, 0, h, 0, 0)))
        out_shape.append(jax.ShapeDtypeStruct((n_batch, 2, B_HEADS, B_DK, B_DV), F32))
    res = pl.pallas_call(
        functools.partial(_hgrn_kernel, seq, n_h, s0 is not None, want_state),
        grid=(n_batch, B_HEADS // n_h),
        in_specs=in_specs,
        out_specs=out_specs,
        out_shape=out_shape,
        scratch_shapes=[pltpu.VMEM((seq, n_h * LANES), F32),
                        pltpu.VMEM((seq, n_h * LANES), F32),
                        pltpu.VMEM((n_h, B_DV, B_DK), F32),
                        pltpu.VMEM((n_h, B_DV, B_DK), F32)],
        compiler_params=_cparams(2),
        name="hgrn2",
    )(*args)
    return res if want_state else (res[0], None)


def _outffn_kernel(o1_ref, o2_ref, x_ref, mod_ref, wo_ref, ln_ref, wg_ref, wu_ref, wd_ref, out_ref):
    m = mod_ref[0]
    half = o1_ref.shape[1]
    y = (_dot(o1_ref[...].astype(BF16), wo_ref[0:half, :])
         + _dot(o2_ref[...].astype(BF16), wo_ref[half:2 * half, :]))
    x1 = _layer_norm(ALPHA * x_ref[...] + m[2:3] * y, ln_ref[0:1, :], ln_ref[1:2, :])
    h = (x1 * (1.0 + m[4:5]) + m[3:4]).astype(BF16)
    acc = None
    for a in range(0, D_FF, 512):
        b = min(a + 512, D_FF)
        act = (_silu(_dot(h, wg_ref[:, a:b])) * _dot(h, wu_ref[:, a:b])).astype(BF16)
        part = _dot(act, wd_ref[a:b, :])
        acc = part if acc is None else acc + part
    out_ref[...] = _layer_norm(ALPHA * x1 + m[5:6] * acc, ln_ref[2:3, :], ln_ref[3:4, :])


def _outffn(o1, o2, x, mod, wo_bf, ln, wg_bf, wu_bf, wd_bf, rows_per_cond, tm=512):
    m_rows = x.shape[0]
    half = o1.shape[1]
    consts = [wo_bf, ln, wg_bf, wu_bf, wd_bf]
    in_specs = [pl.BlockSpec((tm, half), lambda i: (i, 0)),
                pl.BlockSpec((tm, half), lambda i: (i, 0)),
                pl.BlockSpec((tm, D_MODEL), lambda i: (i, 0)),
                pl.BlockSpec((1, 6, D_MODEL), lambda i: (i * tm // rows_per_cond, 0, 0))]
    in_specs += [pl.BlockSpec(a.shape, lambda i: (0, 0)) for a in consts]
    return pl.pallas_call(
        _outffn_kernel,
        grid=(m_rows // tm,),
        in_specs=in_specs,
        out_specs=pl.BlockSpec((tm, D_MODEL), lambda i: (i, 0)),
        out_shape=jax.ShapeDtypeStruct((m_rows, D_MODEL), F32),
        compiler_params=_cparams(1),
        name="outproj_ffn",
    )(o1, o2, x, mod, *consts)


def _cache_spec(past, width, layer):
    return pl.BlockSpec((None, None, past, width), lambda b, i: (b, layer, 0, 0))


def _even_layer(xp, xs, modp, mods, w_in_bf, lb, sink, gnorm, tail_w, rope_hd,
                cache_k, cache_v, state, e, nbp, nbs):
    seq_p = xp.shape[0] // nbp
    seq_s = xs.shape[0] // nbs
    past = cache_k.shape[2]
    pp = _inproj_even(xp, modp, w_in_bf, lb, None, xp.shape[0])
    ps = _inproj_even(xs, mods, w_in_bf, lb, rope_hd, seq_s)
    qa, ka, va, qb, ib, kf, lf, kb, lbw, gb = pp
    oa_p = _attention(qa, ka, va, nbp, A_HEADS, A_HEADS // A_KV, seq_p, sink=sink, n_sub=PROMPT_SUB_A)
    ob_p, s_new = _hgrn(qb, kf, lf, kb, lbw, ib, gb, gnorm, nbp, want_state=True, n_h=PROMPT_HGRN_HEADS)
    xp = _outffn(oa_p, ob_p, xp, modp, *tail_w, xp.shape[0])
    new = (ka.reshape(nbp, seq_p, A_KV, HEAD_DIM), va.reshape(nbp, seq_p, A_KV, HEAD_DIM), s_new)
    qa, ka, va, qb, ib, kf, lf, kb, lbw, gb = ps
    spec = _cache_spec(past, A_KV * HEAD_DIM, e)
    oa_s = _attention(qa, ka, va, nbs, A_HEADS, A_HEADS // A_KV, BAND_TQ, band=True,
                      ctx=(cache_k, cache_v, spec, spec, past), sink=sink, n_qt=BAND_QT)
    ob_s, _ = _hgrn(qb, kf, lf, kb, lbw, ib, gb, gnorm, nbs, s0=(state, e))
    xs = _outffn(oa_s, ob_s, xs, mods, *tail_w, seq_s)
    return xp, xs, new


def _odd_layer(xp, xs, modp, mods, in_w, tail_w, rope_hd, rope_c,
               cache_ckv, cache_pe, cache_k, cache_v, o, nbp, nbs):
    seq_p = xp.shape[0] // nbp
    seq_s = xs.shape[0] // nbs
    past = cache_ckv.shape[2]
    wkv_bf = in_w[4]
    mla_mul = (C_NOPE + C_ROPE) ** -0.5 * LOG2E
    qc, ckv, kc, vc, pe, qd, kd, vd = _inproj_odd(xp, modp, *in_w, None, xp.shape[0])
    oc_p = _attention(qc, kc, vc, nbp, C_HEADS, 1, seq_p, wide=True, score_mul=mla_mul)
    od_p = _attention(qd, kd, vd, nbp, D_HEADS, D_HEADS // D_KV, seq_p, n_sub=PROMPT_SUB_D)
    xp = _outffn(oc_p, od_p, xp, modp, *tail_w, xp.shape[0])
    new = (ckv.reshape(nbp, seq_p, C_KV_LORA), pe[:, C_NOPE:C_NOPE + C_ROPE].reshape(nbp, seq_p, C_ROPE),
           kd.reshape(nbp, seq_p, D_KV, HEAD_DIM), vd.reshape(nbp, seq_p, D_KV, HEAD_DIM))
    qc, ckv, kc, vc, pe, qd, kd, vd = _inproj_odd(xs, mods, *in_w, rope_hd + rope_c, seq_s)
    pe_ctx = jnp.pad(cache_pe[:, o].reshape(nbs * past, C_ROPE), ((0, 0), (C_NOPE, LANES - C_NOPE - C_ROPE)))
    kcc, vcc = _kvup(cache_ckv[:, o].reshape(nbs * past, C_KV_LORA), pe_ctx, wkv_bf)
    oc_s = _attention(qc, kc, vc, nbs, C_HEADS, 1, SAMPLE_TQ, wide=True, score_mul=mla_mul,
                      ctx=(kcc, vcc, pl.BlockSpec((past, C_HEADS * LANES), lambda b, i: (b, 0)),
                           pl.BlockSpec((past, C_HEADS * C_V), lambda b, i: (b, 0)), past), n_qt=SAMPLE_QT)
    spec = _cache_spec(past, D_KV * HEAD_DIM, o)
    od_s = _attention(qd, kd, vd, nbs, D_HEADS, D_HEADS // D_KV, SAMPLE_TQ, ctx=(cache_k, cache_v, spec, spec, past),
                      n_qt=SAMPLE_QT)
    xs = _outffn(oc_s, od_s, xs, mods, *tail_w, seq_s)
    return xp, xs, new


def kernel(x_prompt, x_sample, cache_a_k, cache_a_v, state_b, cache_c_kv, cache_c_pe, cache_d_k, cache_d_v, c, c_ctx, w_ada, b_ada, ln_g, ln_b, w_in_ab, a_sink, b_lb, b_gnorm, w_in_cd, c_q_norm, c_kv_norm, c_w_q_up, c_w_kv_up, d_q_norm, d_k_norm, w_out, w_ffn_gate, w_ffn_up, w_ffn_down):
    nbp, seq_p, _ = x_prompt.shape
    nbs, seq_s, _ = x_sample.shape
    past = cache_a_k.shape[2]
    xp = x_prompt.reshape(nbp * seq_p, D_MODEL)
    xs = x_sample.reshape(nbs * seq_s, D_MODEL)
    rope_hd = _rope_tables(seq_s, HEAD_DIM)
    cos_c, sin_c = _rope_tables(seq_s, C_ROPE)
    ones = jnp.ones((seq_s, C_NOPE), F32)
    rope_c = (jnp.concatenate([ones, cos_c[:, :C_NOPE]], axis=1),
              jnp.concatenate([0.0 * ones, sin_c[:, :C_ROPE], 0.0 * ones[:, :LANES - C_NOPE - C_ROPE]], axis=1))
    lb_w = jax.nn.softmax(b_lb.astype(F32), axis=0)
    lb_all = jnp.cumsum(lb_w, axis=0) - lb_w[:1]
    mod_all = _modulation(c, c_ctx, w_ada, b_ada)
    n_even = cache_a_k.shape[1]
    n_odd = cache_c_kv.shape[1]
    ca_k = cache_a_k.reshape(nbs, n_even, past, A_KV * HEAD_DIM)
    ca_v = cache_a_v.reshape(nbs, n_even, past, A_KV * HEAD_DIM)
    cd_k = cache_d_k.reshape(nbs, n_odd, past, D_KV * HEAD_DIM)
    cd_v = cache_d_v.reshape(nbs, n_odd, past, D_KV * HEAD_DIM)
    ak, av, sb, ckv, cpe, dk, dv = [], [], [], [], [], [], []
    for l in range(DEPTH):
        modp = mod_all[l, 0:1]
        mods = mod_all[l, 1:1 + nbs]
        ln = jnp.concatenate([ln_g[l, 0:1], ln_b[l, 0:1], ln_g[l, 1:2], ln_b[l, 1:2]], axis=0)
        tail_w = (_to_bf16(w_out, l), ln, _to_bf16(w_ffn_gate, l), _to_bf16(w_ffn_up, l), _to_bf16(w_ffn_down, l))
        if l % 2 == 0:
            e = l // 2
            xp, xs, (k_new, v_new, s_new) = _even_layer(
                xp, xs, modp, mods, _to_bf16(w_in_ab, e), lb_all[e], a_sink[e], b_gnorm[e], tail_w,
                rope_hd, ca_k, ca_v, state_b, e, nbp, nbs)
            ak.append(k_new)
            av.append(v_new)
            sb.append(s_new)
        else:
            o = l // 2
            w_in = w_in_cd[o]
            lat = C_Q_LORA + C_KV_LORA
            zeros = lambda n: jnp.zeros((D_MODEL, n), F32)
            w_in = jnp.concatenate([w_in[:, :lat], zeros(C_NOPE), w_in[:, lat:lat + C_ROPE],
                                    zeros(LANES - C_NOPE - C_ROPE), w_in[:, lat + C_ROPE:]], axis=1).astype(BF16)
            wq = c_w_q_up[o].reshape(C_Q_LORA, C_HEADS, C_NOPE + C_ROPE)
            wq = jnp.pad(wq, ((0, 0), (0, 0), (0, LANES - C_NOPE - C_ROPE))).reshape(C_Q_LORA, -1).astype(BF16)
            wkv = c_w_kv_up[o].reshape(C_KV_LORA, C_HEADS, C_NOPE + C_V)
            wk = jnp.pad(wkv[:, :, :C_NOPE], ((0, 0), (0, 0), (0, LANES - C_NOPE))).reshape(C_KV_LORA, -1)
            wkv = jnp.concatenate([wk, wkv[:, :, C_NOPE:].reshape(C_KV_LORA, -1)], axis=1).astype(BF16)
            in_w = (w_in, c_q_norm[o].reshape(1, -1), c_kv_norm[o].reshape(1, -1), wq, wkv,
                    jnp.tile(d_q_norm[o], D_HEADS).reshape(1, -1), jnp.tile(d_k_norm[o], D_KV).reshape(1, -1))
            xp, xs, (c_new, pe_new, k_new, v_new) = _odd_layer(
                xp, xs, modp, mods, in_w, tail_w, rope_hd, rope_c, cache_c_kv, cache_c_pe, cd_k, cd_v, o, nbp, nbs)
            ckv.append(c_new)
            cpe.append(pe_new)
            dk.append(k_new)
            dv.append(v_new)
    return (xp.reshape(nbp, seq_p, D_MODEL), xs.reshape(nbs, seq_s, D_MODEL),
            jnp.stack(ak, axis=1), jnp.stack(av, axis=1), jnp.stack(sb, axis=1),
            jnp.stack(ckv, axis=1), jnp.stack(cpe, axis=1), jnp.stack(dk, axis=1), jnp.stack(dv, axis=1))
```

```python
import functools

import numpy as np
import jax
import jax.numpy as jnp
from jax import lax
from jax.experimental import pallas as pl
from jax.experimental.pallas import tpu as pltpu

F32 = jnp.float32
BF16 = jnp.bfloat16

D_MODEL = 1024
DEPTH = 4
GRID_W = 64
HEAD_DIM = 64
ROPE_BASE = 10000.0
MASK_VALUE = -1e30
F_FLOOR = 1e-30
A_HEADS = 8
A_KV = 2
WINDOW = 128
B_HEADS = 4
B_DK = 128
B_DV = 128
C_HEADS = 8
C_Q_LORA = 384
C_KV_LORA = 256
C_NOPE = 64
C_ROPE = 32
C_V = 64
D_HEADS = 8
D_KV = 4
D_FF = 2816
ALPHA = (2 * DEPTH) ** 0.25

LANES = 128
HGRN_GROUP = 128
LOG2E = 1.4426950408889634
SAMPLE_TQ = 512
PROMPT_HGRN_HEADS = 4
SAMPLE_HGRN_HEADS = 2
PROMPT_SUB_A = 4
PROMPT_SUB_D = 2
SAMPLE_QT = 2
BAND_QT = 4
BAND_TQ = 256
VMEM_LIMIT = 56 * 1024 * 1024
NT_DIMS = (((1,), (1,)), ((), ()))
TN_DIMS = (((0,), (0,)), ((), ()))


def _cparams(n_axes):
    return pltpu.CompilerParams(dimension_semantics=("arbitrary",) * n_axes,
                                vmem_limit_bytes=VMEM_LIMIT)


def _silu(x):
    return x * jax.nn.sigmoid(x)


def _dot(a, b):
    return jnp.dot(a, b, preferred_element_type=F32)


def _layer_norm(x, g, b):
    mu = jnp.mean(x, axis=-1, keepdims=True)
    xc = x - mu
    var = jnp.mean(xc * xc, axis=-1, keepdims=True)
    return xc * lax.rsqrt(var + 1e-5) * g + b


def _rms(x, g):
    return x * lax.rsqrt(jnp.mean(x * x, axis=-1, keepdims=True) + 1e-6) * g


def _rms_heads64(x, g):
    outs = []
    for j in range(x.shape[1] // LANES):
        xg = x[:, j * LANES:(j + 1) * LANES]
        x2 = xg * xg
        lo = lax.broadcasted_iota(jnp.int32, xg.shape, 1) < HEAD_DIM
        s_lo = jnp.sum(jnp.where(lo, x2, 0.0), axis=-1, keepdims=True)
        s_hi = jnp.sum(jnp.where(lo, 0.0, x2), axis=-1, keepdims=True)
        ms = jnp.where(lo, s_lo, s_hi) * (1.0 / HEAD_DIM)
        outs.append(xg * lax.rsqrt(ms + 1e-6))
    return jnp.concatenate(outs, axis=1) * g


def _rope(x, cos, sin, half):
    outs = []
    for j in range(x.shape[1] // LANES):
        xg = x[:, j * LANES:(j + 1) * LANES]
        lane = lax.broadcasted_iota(jnp.int32, xg.shape, 1)
        up = pltpu.roll(xg, LANES - half, 1)
        down = pltpu.roll(xg, half, 1)
        rot = jnp.where((lane & (2 * half - 1)) < half, up, down)
        outs.append(xg * cos + rot * sin)
    return outs[0] if len(outs) == 1 else jnp.concatenate(outs, axis=1)


def _rope_tables(n_tokens, rot_dim):
    n_rows = n_tokens // GRID_W
    row = jnp.broadcast_to(jnp.arange(n_rows, dtype=F32)[:, None], (n_rows, GRID_W)).reshape(-1)
    col = jnp.broadcast_to(jnp.arange(GRID_W, dtype=F32)[None, :], (n_rows, GRID_W)).reshape(-1)
    quarter = rot_dim // 4
    inv = ROPE_BASE ** (-jnp.arange(quarter, dtype=F32) / quarter)
    ar = row[:, None] * inv
    ac = col[:, None] * inv
    cos = jnp.concatenate([jnp.cos(ar), jnp.cos(ar), jnp.cos(ac), jnp.cos(ac)], axis=-1)
    sin = jnp.concatenate([-jnp.sin(ar), jnp.sin(ar), -jnp.sin(ac), jnp.sin(ac)], axis=-1)
    reps = LANES // rot_dim
    return jnp.tile(cos, (1, reps)), jnp.tile(sin, (1, reps))


def _cast_kernel(*refs):
    n = len(refs) // 2
    for x_ref, o_ref in zip(refs[:n], refs[n:]):
        o_ref[...] = x_ref[...].astype(BF16)


def _to_bf16(ws, layer):
    n_blocks = 4
    in_specs, out_specs, out_shape = [], [], []
    for w in ws:
        _, rows, cols = w.shape
        in_specs.append(pl.BlockSpec((None, rows // n_blocks, cols), lambda i: (layer, i, 0)))
        out_specs.append(pl.BlockSpec((rows // n_blocks, cols), lambda i: (i, 0)))
        out_shape.append(jax.ShapeDtypeStruct((rows, cols), BF16))
    return pl.pallas_call(
        _cast_kernel,
        grid=(n_blocks,),
        in_specs=in_specs,
        out_specs=out_specs,
        out_shape=out_shape,
        compiler_params=_cparams(1),
        name="to_bf16",
    )(*ws)


def _mod_kernel(c_ref, w_ref, b_ref, o_ref):
    a = _silu(c_ref[...]).astype(BF16)
    o_ref[0] = _dot(a, w_ref[0].astype(BF16)) + b_ref[0]


def _modulation(c, c_ctx, w_ada, b_ada):
    nb = c.shape[0]
    cond = jnp.zeros((16, D_MODEL), F32).at[0].set(c_ctx).at[1:1 + nb].set(c)
    tn = 1536
    out = pl.pallas_call(
        _mod_kernel,
        grid=(DEPTH, 6 * D_MODEL // tn),
        in_specs=[pl.BlockSpec((16, D_MODEL), lambda l, j: (0, 0)),
                  pl.BlockSpec((1, D_MODEL, tn), lambda l, j: (l, 0, j)),
                  pl.BlockSpec((1, 1, tn), lambda l, j: (l, 0, j))],
        out_specs=pl.BlockSpec((1, 16, tn), lambda l, j: (l, 0, j)),
        out_shape=jax.ShapeDtypeStruct((DEPTH, 16, 6 * D_MODEL), F32),
        compiler_params=_cparams(2),
        name="modulation",
    )(cond, w_ada, b_ada.reshape(DEPTH, 1, 6 * D_MODEL))
    return out.reshape(DEPTH, 16, 6, D_MODEL)


def _modulate(x_ref, mod_ref, shift_row):
    m = mod_ref[0]
    return (x_ref[...] * (1.0 + m[shift_row + 1:shift_row + 2]) + m[shift_row:shift_row + 1]).astype(BF16)


def _inproj_even_kernel(rope, *refs):
    if rope:
        x_ref, mod_ref, w_ref, lb_ref, cos_ref, sin_ref = refs[:6]
        outs = refs[6:]
    else:
        x_ref, mod_ref, w_ref, lb_ref = refs[:4]
        outs = refs[4:]
    qa_ref, ka_ref, va_ref, qb_ref, ib_ref, kf_ref, lf_ref, kb_ref, lbw_ref, gb_ref = outs
    h = _modulate(x_ref, mod_ref, 0)

    def proj(a, b):
        return _dot(h, w_ref[:, a:b])

    qa = proj(0, 512)
    ka = proj(512, 640)
    if rope:
        cos = cos_ref[...]
        sin = sin_ref[...]
        qa = _rope(qa, cos, sin, HEAD_DIM // 4)
        ka = _rope(ka, cos, sin, HEAD_DIM // 4)
    qa_ref[...] = qa.astype(BF16)
    ka_ref[...] = ka
    va_ref[...] = proj(640, 768)
    qb_ref[...] = _silu(proj(768, 1280))
    ib_ref[...] = proj(1280, 1792).astype(BF16)
    for d, (k_ref, l_ref) in enumerate(((kf_ref, lf_ref), (kb_ref, lbw_ref))):
        lb = lb_ref[d:d + 1, :]
        f = lb + (1.0 - lb) * jax.nn.sigmoid(proj(1792 + 512 * d, 2304 + 512 * d))
        l_ref[...] = jnp.log(jnp.maximum(f, F_FLOOR))
        k_ref[...] = 1.0 - f
    gb_ref[...] = _silu(proj(2816, 3328))


def _inproj_even(x, mod, w_bf, lb, rope_tabs, rows_per_cond, tm=256):
    m_rows = x.shape[0]
    rope = rope_tabs is not None
    widths = (512, 128, 128, 512, 512, 512, 512, 512, 512, 512)
    dtypes = (BF16, F32, F32, F32, BF16, F32, F32, F32, F32, F32)
    in_specs = [pl.BlockSpec((tm, D_MODEL), lambda i: (i, 0)),
                pl.BlockSpec((1, 6, D_MODEL), lambda i: (i * tm // rows_per_cond, 0, 0)),
                pl.BlockSpec(w_bf.shape, lambda i: (0, 0)),
                pl.BlockSpec(lb.shape, lambda i: (0, 0))]
    args = [x, mod, w_bf, lb]
    if rope:
        nblk = rope_tabs[0].shape[0] // tm
        in_specs += [pl.BlockSpec((tm, LANES), lambda i: (i % nblk, 0))] * 2
        args += list(rope_tabs)
    return pl.pallas_call(
        functools.partial(_inproj_even_kernel, rope),
        grid=(m_rows // tm,),
        in_specs=in_specs,
        out_specs=[pl.BlockSpec((tm, w), lambda i: (i, 0)) for w in widths],
        out_shape=[jax.ShapeDtypeStruct((m_rows, w), dt) for w, dt in zip(widths, dtypes)],
        compiler_params=_cparams(1),
        name="inproj_even",
    )(*args)


def _inproj_odd_kernel(rope, *refs):
    if rope:
        (x_ref, mod_ref, w_ref, gcq_ref, gckv_ref, wq_ref, wkv_ref, gdq_ref, gdk_ref,
         cosd_ref, sind_ref, cosc_ref, sinc_ref) = refs[:13]
        outs = refs[13:]
    else:
        x_ref, mod_ref, w_ref, gcq_ref, gckv_ref, wq_ref, wkv_ref, gdq_ref, gdk_ref = refs[:9]
        outs = refs[9:]
    qc_ref, ckv_ref, kc_ref, vc_ref, pe_ref, qd_ref, kd_ref, vd_ref = outs
    h = _modulate(x_ref, mod_ref, 0)

    def proj(a, b):
        return _dot(h, w_ref[:, a:b])

    cq = _rms(proj(0, 384), gcq_ref[...]).astype(BF16)
    qc = _dot(cq, wq_ref[...])
    ckv = _rms(proj(384, 640), gckv_ref[...])
    ckv_ref[...] = ckv
    ckv_bf = ckv.astype(BF16)
    kn = _dot(ckv_bf, wkv_ref[:, 0:C_HEADS * LANES])
    vc_ref[...] = _dot(ckv_bf, wkv_ref[:, C_HEADS * LANES:]).astype(BF16)
    pe = proj(640, 768)
    qd = _rms_heads64(proj(768, 1280), gdq_ref[...])
    kd = _rms_heads64(proj(1280, 1536), gdk_ref[...])
    if rope:
        qc = _rope(qc, cosc_ref[...], sinc_ref[...], C_ROPE // 4)
        pe = _rope(pe, cosc_ref[...], sinc_ref[...], C_ROPE // 4)
        qd = _rope(qd, cosd_ref[...], sind_ref[...], HEAD_DIM // 4)
        kd = _rope(kd, cosd_ref[...], sind_ref[...], HEAD_DIM // 4)
    qc_ref[...] = qc.astype(BF16)
    pe_ref[...] = pe
    kc_ref[...] = jnp.concatenate([kn[:, j * LANES:(j + 1) * LANES] + pe for j in range(C_HEADS)],
                                  axis=1).astype(BF16)
    qd_ref[...] = qd.astype(BF16)
    kd_ref[...] = kd
    vd_ref[...] = proj(1536, 1792)


def _inproj_odd(x, mod, w_bf, gcq, gckv, wq_bf, wkv_bf, gdq, gdk, rope_tabs, rows_per_cond, tm=512):
    m_rows = x.shape[0]
    rope = rope_tabs is not None
    outs = ((C_HEADS * LANES, BF16), (C_KV_LORA, F32), (C_HEADS * LANES, BF16), (C_HEADS * C_V, BF16),
            (LANES, F32), (D_HEADS * HEAD_DIM, BF16), (D_KV * HEAD_DIM, F32), (D_KV * HEAD_DIM, F32))
    consts = [w_bf, gcq, gckv, wq_bf, wkv_bf, gdq, gdk]
    in_specs = [pl.BlockSpec((tm, D_MODEL), lambda i: (i, 0)),
                pl.BlockSpec((1, 6, D_MODEL), lambda i: (i * tm // rows_per_cond, 0, 0))]
    in_specs += [pl.BlockSpec(a.shape, lambda i: (0, 0)) for a in consts]
    args = [x, mod] + consts
    if rope:
        nblk = rope_tabs[0].shape[0] // tm
        in_specs += [pl.BlockSpec((tm, LANES), lambda i: (i % nblk, 0))] * 4
        args += list(rope_tabs)
    return pl.pallas_call(
        functools.partial(_inproj_odd_kernel, rope),
        grid=(m_rows // tm,),
        in_specs=in_specs,
        out_specs=[pl.BlockSpec((tm, w), lambda i: (i, 0)) for w, _ in outs],
        out_shape=[jax.ShapeDtypeStruct((m_rows, w), dt) for w, dt in outs],
        compiler_params=_cparams(1),
        name="inproj_odd",
    )(*args)


def _kvup_kernel(x_ref, pe_ref, w_ref, kc_ref, vc_ref):
    x = x_ref[...].astype(BF16)
    kn = _dot(x, w_ref[:, 0:C_HEADS * LANES])
    pe = pe_ref[...]
    kc_ref[...] = jnp.concatenate([kn[:, j * LANES:(j + 1) * LANES] + pe for j in range(C_HEADS)],
                                  axis=1).astype(BF16)
    vc_ref[...] = _dot(x, w_ref[:, C_HEADS * LANES:]).astype(BF16)


def _kvup(lat, pe_slab, wkv_bf, tm=512):
    m_rows = lat.shape[0]
    return pl.pallas_call(
        _kvup_kernel,
        grid=(m_rows // tm,),
        in_specs=[pl.BlockSpec((tm, C_KV_LORA), lambda i: (i, 0)),
                  pl.BlockSpec((tm, LANES), lambda i: (i, 0)),
                  pl.BlockSpec(wkv_bf.shape, lambda i: (0, 0))],
        out_specs=[pl.BlockSpec((tm, C_HEADS * LANES), lambda i: (i, 0)),
                   pl.BlockSpec((tm, C_HEADS * C_V), lambda i: (i, 0))],
        out_shape=[jax.ShapeDtypeStruct((m_rows, C_HEADS * LANES), BF16),
                   jax.ShapeDtypeStruct((m_rows, C_HEADS * C_V), BF16)],
        compiler_params=_cparams(1),
        name="kv_up_ctx",
    )(lat, pe_slab, wkv_bf)


def _attn_kernel(tq, n_heads, group, seq_k, past, band, has_sink, wide, score_mul, n_sub, n_qt, *refs):
    q_ref, k_ref, v_ref = refs[:3]
    idx = 3
    if past:
        kc_ref, vc_ref = refs[idx:idx + 2]
        idx += 2
    if has_sink:
        sink_ref = refs[idx]
        idx += 1
    o_ref, kpad, vaug = refs[idx:idx + 3]
    d = HEAD_DIM
    n_kv = n_heads // group
    n_slots = n_heads if wide else 2 * n_kv
    total = seq_k + past
    qi = pl.program_id(1)

    def _prepare_keys_values():
        for sub in range(n_sub):
            parts = [(k_ref, v_ref, sub * seq_k, 0, seq_k)]
            if past:
                parts.append((kc_ref, vc_ref, 0, seq_k, past))
            base = sub * n_slots
            for ks_ref, vs_ref, src0, r0, n in parts:
                src = slice(src0, src0 + n)
                rows = slice(r0, r0 + n)
                lo = lax.broadcasted_iota(jnp.int32, (n, LANES), 1) < d
                for j in range(n_kv * d // LANES):
                    cols = slice(j * LANES, (j + 1) * LANES)
                    vs = vs_ref[src, cols].astype(F32)
                    if wide:
                        vaug[base + 2 * j, rows, :] = jnp.where(lo, vs, 1.0).astype(BF16)
                        vaug[base + 2 * j + 1, rows, :] = jnp.where(lo, 1.0, vs).astype(BF16)
                        continue
                    vr = pltpu.roll(vs, d, 1)
                    ks = ks_ref[src, cols]
                    kr = pltpu.roll(ks, d, 1)
                    for g, (k_even, k_odd, v_even, v_odd) in ((2 * j, (ks, kr, vs, vr)), (2 * j + 1, (kr, ks, vr, vs))):
                        kpad[base + 2 * g, rows, :] = jnp.where(lo, k_even, 0.0).astype(BF16)
                        kpad[base + 2 * g + 1, rows, :] = jnp.where(lo, 0.0, k_odd).astype(BF16)
                        vaug[base + 2 * g, rows, :] = jnp.where(lo, v_even, 1.0).astype(BF16)
                        vaug[base + 2 * g + 1, rows, :] = jnp.where(lo, 1.0, v_odd).astype(BF16)
                if wide:
                    for h in range(n_heads):
                        kpad[base + h, rows, :] = ks_ref[src, h * LANES:(h + 1) * LANES]

    if seq_k == tq * n_qt:
        _prepare_keys_values()
    else:
        pl.when(qi == 0)(_prepare_keys_values)

    def key_window(tile):
        if not band:
            return [slice(0, total)], None
        kb = tq + 2 * WINDOW
        start = pl.multiple_of(jnp.clip(tile * tq - WINDOW, 0, seq_k - kb), LANES)
        diff = ((tile * tq - start) + lax.broadcasted_iota(jnp.int32, (tq, kb), 0)
                - lax.broadcasted_iota(jnp.int32, (tq, kb), 1))
        return [pl.ds(start, kb), pl.ds(seq_k, past)], jnp.abs(diff) <= WINDOW

    def nt(a, b):
        return lax.dot_general(a, b, NT_DIMS, preferred_element_type=F32)

    lo_out = lax.broadcasted_iota(jnp.int32, (tq, LANES), 1) < d
    for unit in range(n_sub * n_qt):
        base = (unit // n_qt) * n_slots
        q_rows = slice(unit * tq, (unit + 1) * tq)
        key_rows, bmask = key_window(qi * n_qt + unit % n_qt)
        q_slabs = {}
        results = {}
        for g in range(n_kv):
            for parity in (0, 1):
                heads = [h for h in range(g * group, (g + 1) * group) if h % 2 == parity]
                if not heads:
                    continue
                probs = [[] for _ in key_rows]
                maxes = []
                for h in heads:
                    slot = base + (h if wide else 2 * g + parity)
                    qs = h if wide else h // 2
                    if qs not in q_slabs:
                        q = q_ref[q_rows, qs * LANES:(qs + 1) * LANES]
                        q_slabs[qs] = q if wide else (q * d ** -0.5).astype(BF16)
                    scores = [nt(q_slabs[qs], kpad[slot, r, :]) for r in key_rows]
                    if score_mul is not None:
                        scores = [s * score_mul for s in scores]
                    if band:
                        scores[0] = jnp.where(bmask, scores[0], MASK_VALUE)
                    m = jnp.max(scores[0], axis=-1, keepdims=True)
                    for s in scores[1:]:
                        m = jnp.maximum(m, jnp.max(s, axis=-1, keepdims=True))
                    if has_sink:
                        m = jnp.maximum(m, sink_ref[h])
                    maxes.append(m)
                    for part, s in zip(probs, scores):
                        e = jnp.exp(s - m) if score_mul is None else jnp.exp2(s - m)
                        part.append(e.astype(BF16))
                vslot = base + (heads[0] if wide else 2 * g + parity)
                o_aug = None
                for part, r in zip(probs, key_rows):
                    p = part[0] if len(part) == 1 else jnp.concatenate(part, axis=0)
                    o = _dot(p, vaug[vslot, r, :])
                    o_aug = o if o_aug is None else o_aug + o
                den = pltpu.roll(o_aug, d, 1)
                for i, h in enumerate(heads):
                    rows = slice(i * tq, (i + 1) * tq)
                    den_h = den[rows]
                    if has_sink:
                        den_h = den_h + jnp.exp(sink_ref[h] - maxes[i])
                    results[h] = o_aug[rows] / den_h
        for j in range(n_heads // 2):
            o_ref[q_rows, j * LANES:(j + 1) * LANES] = jnp.where(lo_out, results[2 * j], results[2 * j + 1])


def _attention(q, k, v, n_batch, n_heads, group, tq, *, band=False, ctx=None, sink=None, wide=False,
               score_mul=None, n_sub=1, n_qt=1):
    seq = q.shape[0] // n_batch
    nq = seq // (tq * n_qt)
    n_kv = n_heads // group
    assert n_sub == 1 or (n_qt == 1 and nq == 1 and ctx is None and n_batch % n_sub == 0)
    in_specs = [pl.BlockSpec((n_sub * n_qt * tq, q.shape[1]), lambda b, i: (b * nq + i, 0)),
                pl.BlockSpec((n_sub * seq, k.shape[1]), lambda b, i: (b, 0)),
                pl.BlockSpec((n_sub * seq, v.shape[1]), lambda b, i: (b, 0))]
    args = [q, k, v]
    past = 0
    if ctx is not None:
        kc, vc, kc_spec, vc_spec, past = ctx
        in_specs += [kc_spec, vc_spec]
        args += [kc, vc]
    if sink is not None:
        in_specs.append(pl.BlockSpec(memory_space=pltpu.SMEM))
        args.append(sink)
    n_slots = n_sub * (n_heads if wide else 2 * n_kv)
    return pl.pallas_call(
        functools.partial(_attn_kernel, tq, n_heads, group, seq, past, band, sink is not None, wide, score_mul,
                          n_sub, n_qt),
        grid=(n_batch // n_sub, nq),
        in_specs=in_specs,
        out_specs=pl.BlockSpec((n_sub * n_qt * tq, n_heads * HEAD_DIM), lambda b, i: (b * nq + i, 0)),
        out_shape=jax.ShapeDtypeStruct((q.shape[0], n_heads * HEAD_DIM), F32),
        scratch_shapes=[pltpu.VMEM((n_slots, seq + past, LANES), BF16),
                        pltpu.VMEM((n_slots, seq + past, LANES), BF16)],
        compiler_params=_cparams(2),
        name="mla_attention" if wide else "gqa_attention",
    )(*args)


def _hgrn_levels():
    i = np.arange(HGRN_GROUP)
    t, s = i[:, None], i[None, :]
    level = np.where(t != s, np.floor(np.log2(np.maximum(t ^ s, 1))) + 1, 0).astype(np.int32)
    return (jnp.asarray(np.where(s <= t, level, -1), jnp.int32),
            jnp.asarray(np.where(s >= t, level, -1), jnp.int32))


def _scan_rows(x, rev):
    n = x.shape[0]
    row = lax.broadcasted_iota(jnp.int32, x.shape, 0)
    step = 1
    while step < n:
        if step < 8:
            if rev:
                x = x + jnp.where(row < n - step, pltpu.roll(x, n - step, 0), 0.0)
            else:
                x = x + jnp.where(row >= step, pltpu.roll(x, step, 0), 0.0)
        elif rev:
            x = jnp.concatenate([x[:n - step] + x[step:], x[n - step:]], axis=0)
        else:
            x = jnp.concatenate([x[:step], x[step:] + x[:n - step]], axis=0)
        step *= 2
    return x


def _level_refs(c, level, rev):
    n = c.shape[0]
    half = 1 << (level - 1)
    bs = 2 * half
    off = half if rev else half - 1
    if bs >= 16:
        return jnp.concatenate([jnp.broadcast_to(c[i * bs + off:i * bs + off + 1, :], (bs, LANES))
                                for i in range(n // bs)], axis=0)
    c3 = c.reshape(n // 8, 8, LANES)
    sub = lax.broadcasted_iota(jnp.int32, c3.shape, 1)
    out = None
    for j in reversed(range(8 // bs)):
        b = jnp.broadcast_to(c3[:, j * bs + off:j * bs + off + 1, :], c3.shape)
        out = b if out is None else jnp.where(sub < (j + 1) * bs, b, out)
    return out.reshape(n, LANES)


def _neg_abs(x):
    return lax.bitcast_convert_type(lax.bitcast_convert_type(x, jnp.uint32) | jnp.uint32(0x80000000), F32)


def _hgrn_kernel(seq, n_h, has_s0, want_state, *refs):
    q_ref, kf_ref, lf_ref, kb_ref, lbw_ref, v_ref, gb_ref, gn_ref, lvf_ref, lvb_ref = refs[:10]
    idx = 10
    if has_s0:
        s0_ref = refs[idx]
        idx += 1
    o_ref = refs[idx]
    idx += 1
    if want_state:
        sfin_ref = refs[idx]
        idx += 1
    of_scr, ob_scr, stf, stb = refs[idx:idx + 4]
    n_groups = seq // HGRN_GROUP
    scale = B_DK ** -0.5
    n_levels = HGRN_GROUP.bit_length() - 1
    chains = [(hh, d) + spec for hh in range(n_h)
              for d, spec in enumerate(((kf_ref, lf_ref, lvf_ref, of_scr, stf, False),
                                        (kb_ref, lbw_ref, lvb_ref, ob_scr, stb, True)))]

    for hh, d, _, _, _, _, st, _ in chains:
        st[hh] = s0_ref[d, hh].T if has_s0 else jnp.zeros((B_DV, B_DK), F32)

    def nt(a, b):
        return lax.dot_general(a.astype(BF16), b.astype(BF16), NT_DIMS, preferred_element_type=F32)

    def group_step(i, carry):
        work = []
        for hh, d, k_ref, l_ref, lv_ref, o_scr, st, rev in chains:
            g = (n_groups - 1 - i) if rev else i
            rows = pl.ds(pl.multiple_of(g * HGRN_GROUP, HGRN_GROUP), HGRN_GROUP)
            cols = slice(hh * LANES, (hh + 1) * LANES)
            logf = l_ref[rows, cols] * LOG2E
            c = _scan_rows(logf, rev)
            work.append(dict(rows=rows, cols=cols, hh=hh, c=c, logf=logf, q=q_ref[rows, cols] * scale,
                             k=k_ref[rows, cols], v=v_ref[rows, cols].astype(BF16), lv=lv_ref, o=o_scr, st=st,
                             rev=rev))
        for w in work:
            w["qb"] = w["q"].astype(BF16)
            w["kb"] = w["k"].astype(BF16)
            w["attn"] = jnp.where(w["lv"][...] == 0, nt(w["qb"], w["kb"]), 0.0)
        odd = (lax.broadcasted_iota(jnp.int32, (HGRN_GROUP, LANES), 0) & 1) == 1
        for level in range(1, n_levels + 1):
            for w in work:
                if level == 1:
                    arg = jnp.where(odd, 0.0, w["logf"]) if w["rev"] else jnp.where(odd, w["logf"], 0.0)
                else:
                    arg = _neg_abs(w["c"] - _level_refs(w["c"], level, w["rev"]))
                e = jnp.exp2(arg).astype(BF16)
                w["attn"] = jnp.where(w["lv"][...] == level, nt(w["qb"] * e, w["kb"] * e), w["attn"])
        for w in work:
            c, st, hh = w["c"], w["st"], w["hh"]
            tot = c[0:1] if w["rev"] else c[HGRN_GROUP - 1:HGRN_GROUP]
            s_t = st[hh]
            w["o"][w["rows"], w["cols"]] = _dot(w["attn"].astype(BF16), w["v"]) + nt(w["q"] * jnp.exp2(c), s_t)
            kt = (w["k"] * jnp.exp2(tot - c)).astype(BF16)
            st[hh] = s_t * jnp.exp2(tot) + lax.dot_general(w["v"], kt, TN_DIMS, preferred_element_type=F32)
        return carry

    lax.fori_loop(0, n_groups, group_step, 0, unroll=True)
    for hh in range(n_h):
        cols = slice(hh * LANES, (hh + 1) * LANES)
        if want_state:
            sfin_ref[0, hh] = stf[hh].T
            sfin_ref[1, hh] = stb[hh].T
        o_ref[:, cols] = _rms(of_scr[:, cols] + ob_scr[:, cols], gn_ref[...]) * gb_ref[:, cols]


def _hgrn(q, kf, lf, kb, lbw, v, gb, gnorm, n_batch, s0=None, want_state=False, n_h=1):
    seq = q.shape[0] // n_batch
    tok = pl.BlockSpec((seq, n_h * LANES), lambda b, h: (b, h))
    const = pl.BlockSpec((HGRN_GROUP, HGRN_GROUP), lambda b, h: (0, 0))
    in_specs = [tok] * 7 + [pl.BlockSpec((1, LANES), lambda b, h: (0, 0))] + [const] * 2
    args = [q, kf, lf, kb, lbw, v, gb, gnorm.reshape(1, LANES)] + list(_hgrn_levels())
    if s0 is not None:
        state, layer = s0
        in_specs.append(pl.BlockSpec((None, None, 2, n_h, B_DK, B_DV), lambda b, h: (b, layer, 0, h, 0, 0)))
        args.append(state)
    out_specs = [tok]
    out_shape = [jax.ShapeDtypeStruct(q.shape, F32)]
    if want_state:
        out_specs.append(pl.BlockSpec((None, 2, n_h, B_DK, B_DV), lambda b, h: (b, 0, h, 0, 0)))
        out_shape.append(jax.ShapeDtypeStruct((n_batch, 2, B_HEADS, B_DK, B_DV), F32))
    res = pl.pallas_call(
        functools.partial(_hgrn_kernel, seq, n_h, s0 is not None, want_state),
        grid=(n_batch, B_HEADS // n_h),
        in_specs=in_specs,
        out_specs=out_specs,
        out_shape=out_shape,
        scratch_shapes=[pltpu.VMEM((seq, n_h * LANES), F32),
                        pltpu.VMEM((seq, n_h * LANES), F32),
                        pltpu.VMEM((n_h, B_DV, B_DK), F32),
                        pltpu.VMEM((n_h, B_DV, B_DK), F32)],
        compiler_params=_cparams(2),
        name="hgrn2",
    )(*args)
    return res if want_state else (res[0], None)


def _outffn_kernel(o1_ref, o2_ref, x_ref, mod_ref, wo_ref, ln_ref, wg_ref, wu_ref, wd_ref, out_ref):
    m = mod_ref[0]
    half = o1_ref.shape[1]
    y = (_dot(o1_ref[...].astype(BF16), wo_ref[0:half, :])
         + _dot(o2_ref[...].astype(BF16), wo_ref[half:2 * half, :]))
    x1 = _layer_norm(ALPHA * x_ref[...] + m[2:3] * y, ln_ref[0:1, :], ln_ref[1:2, :])
    h = (x1 * (1.0 + m[4:5]) + m[3:4]).astype(BF16)
    acc = None
    for a in range(0, D_FF, 512):
        b = min(a + 512, D_FF)
        act = (_silu(_dot(h, wg_ref[:, a:b])) * _dot(h, wu_ref[:, a:b])).astype(BF16)
        part = _dot(act, wd_ref[a:b, :])
        acc = part if acc is None else acc + part
    out_ref[...] = _layer_norm(ALPHA * x1 + m[5:6] * acc, ln_ref[2:3, :], ln_ref[3:4, :])


def _outffn(o1, o2, x, mod, wo_bf, ln, wg_bf, wu_bf, wd_bf, rows_per_cond, tm=512):
    m_rows = x.shape[0]
    half = o1.shape[1]
    consts = [wo_bf, ln, wg_bf, wu_bf, wd_bf]
    in_specs = [pl.BlockSpec((tm, half), lambda i: (i, 0)),
                pl.BlockSpec((tm, half), lambda i: (i, 0)),
                pl.BlockSpec((tm, D_MODEL), lambda i: (i, 0)),
                pl.BlockSpec((1, 6, D_MODEL), lambda i: (i * tm // rows_per_cond, 0, 0))]
    in_specs += [pl.BlockSpec(a.shape, lambda i: (0, 0)) for a in consts]
    return pl.pallas_call(
        _outffn_kernel,
        grid=(m_rows // tm,),
        in_specs=in_specs,
        out_specs=pl.BlockSpec((tm, D_MODEL), lambda i: (i, 0)),
        out_shape=jax.ShapeDtypeStruct((m_rows, D_MODEL), F32),
        compiler_params=_cparams(1),
        name="outproj_ffn",
    )(o1, o2, x, mod, *consts)


def _cache_spec(past, width, layer):
    return pl.BlockSpec((None, None, past, width), lambda b, i: (b, layer, 0, 0))


def _even_layer(xp, xs, modp, mods, w_in_bf, lb, sink, gnorm, tail_w, rope_hd,
                cache_k, cache_v, state, e, nbp, nbs):
    seq_p = xp.shape[0] // nbp
    seq_s = xs.shape[0] // nbs
    past = cache_k.shape[2]
    pp = _inproj_even(xp, modp, w_in_bf, lb, None, xp.shape[0])
    ps = _inproj_even(xs, mods, w_in_bf, lb, rope_hd, seq_s)
    qa, ka, va, qb, ib, kf, lf, kb, lbw, gb = pp
    oa_p = _attention(qa, ka, va, nbp, A_HEADS, A_HEADS // A_KV, seq_p, sink=sink, n_sub=PROMPT_SUB_A)
    ob_p, s_new = _hgrn(qb, kf, lf, kb, lbw, ib, gb, gnorm, nbp, want_state=True, n_h=PROMPT_HGRN_HEADS)
    xp = _outffn(oa_p, ob_p, xp, modp, *tail_w, xp.shape[0])
    new = (ka.reshape(nbp, seq_p, A_KV, HEAD_DIM), va.reshape(nbp, seq_p, A_KV, HEAD_DIM), s_new)
    qa, ka, va, qb, ib, kf, lf, kb, lbw, gb = ps
    spec = _cache_spec(past, A_KV * HEAD_DIM, e)
    oa_s = _attention(qa, ka, va, nbs, A_HEADS, A_HEADS // A_KV, BAND_TQ, band=True,
                      ctx=(cache_k, cache_v, spec, spec, past), sink=sink, n_qt=BAND_QT)
    ob_s, _ = _hgrn(qb, kf, lf, kb, lbw, ib, gb, gnorm, nbs, s0=(state, e), n_h=SAMPLE_HGRN_HEADS)
    xs = _outffn(oa_s, ob_s, xs, mods, *tail_w, seq_s)
    return xp, xs, new


def _odd_layer(xp, xs, modp, mods, in_w, tail_w, rope_hd, rope_c,
               cache_ckv, cache_pe, cache_k, cache_v, o, nbp, nbs):
    seq_p = xp.shape[0] // nbp
    seq_s = xs.shape[0] // nbs
    past = cache_ckv.shape[2]
    wkv_bf = in_w[4]
    mla_mul = (C_NOPE + C_ROPE) ** -0.5 * LOG2E
    qc, ckv, kc, vc, pe, qd, kd, vd = _inproj_odd(xp, modp, *in_w, None, xp.shape[0])
    oc_p = _attention(qc, kc, vc, nbp, C_HEADS, 1, seq_p, wide=True, score_mul=mla_mul)
    od_p = _attention(qd, kd, vd, nbp, D_HEADS, D_HEADS // D_KV, seq_p, n_sub=PROMPT_SUB_D)
    xp = _outffn(oc_p, od_p, xp, modp, *tail_w, xp.shape[0])
    new = (ckv.reshape(nbp, seq_p, C_KV_LORA), pe[:, C_NOPE:C_NOPE + C_ROPE].reshape(nbp, seq_p, C_ROPE),
           kd.reshape(nbp, seq_p, D_KV, HEAD_DIM), vd.reshape(nbp, seq_p, D_KV, HEAD_DIM))
    qc, ckv, kc, vc, pe, qd, kd, vd = _inproj_odd(xs, mods, *in_w, rope_hd + rope_c, seq_s)
    pe_ctx = jnp.pad(cache_pe[:, o].reshape(nbs * past, C_ROPE), ((0, 0), (C_NOPE, LANES - C_NOPE - C_ROPE)))
    kcc, vcc = _kvup(cache_ckv[:, o].reshape(nbs * past, C_KV_LORA), pe_ctx, wkv_bf)
    oc_s = _attention(qc, kc, vc, nbs, C_HEADS, 1, SAMPLE_TQ, wide=True, score_mul=mla_mul,
                      ctx=(kcc, vcc, pl.BlockSpec((past, C_HEADS * LANES), lambda b, i: (b, 0)),
                           pl.BlockSpec((past, C_HEADS * C_V), lambda b, i: (b, 0)), past), n_qt=SAMPLE_QT)
    spec = _cache_spec(past, D_KV * HEAD_DIM, o)
    od_s = _attention(qd, kd, vd, nbs, D_HEADS, D_HEADS // D_KV, SAMPLE_TQ, ctx=(cache_k, cache_v, spec, spec, past),
                      n_qt=SAMPLE_QT)
    xs = _outffn(oc_s, od_s, xs, mods, *tail_w, seq_s)
    return xp, xs, new


def kernel(x_prompt, x_sample, cache_a_k, cache_a_v, state_b, cache_c_kv, cache_c_pe, cache_d_k, cache_d_v, c, c_ctx, w_ada, b_ada, ln_g, ln_b, w_in_ab, a_sink, b_lb, b_gnorm, w_in_cd, c_q_norm, c_kv_norm, c_w_q_up, c_w_kv_up, d_q_norm, d_k_norm, w_out, w_ffn_gate, w_ffn_up, w_ffn_down):
    nbp, seq_p, _ = x_prompt.shape
    nbs, seq_s, _ = x_sample.shape
    past = cache_a_k.shape[2]
    xp = x_prompt.reshape(nbp * seq_p, D_MODEL)
    xs = x_sample.reshape(nbs * seq_s, D_MODEL)
    rope_hd = _rope_tables(seq_s, HEAD_DIM)
    cos_c, sin_c = _rope_tables(seq_s, C_ROPE)
    ones = jnp.ones((seq_s, C_NOPE), F32)
    rope_c = (jnp.concatenate([ones, cos_c[:, :C_NOPE]], axis=1),
              jnp.concatenate([0.0 * ones, sin_c[:, :C_ROPE], 0.0 * ones[:, :LANES - C_NOPE - C_ROPE]], axis=1))
    lb_w = jax.nn.softmax(b_lb.astype(F32), axis=0)
    lb_all = jnp.cumsum(lb_w, axis=0) - lb_w[:1]
    mod_all = _modulation(c, c_ctx, w_ada, b_ada)
    n_even = cache_a_k.shape[1]
    n_odd = cache_c_kv.shape[1]
    ca_k = cache_a_k.reshape(nbs, n_even, past, A_KV * HEAD_DIM)
    ca_v = cache_a_v.reshape(nbs, n_even, past, A_KV * HEAD_DIM)
    cd_k = cache_d_k.reshape(nbs, n_odd, past, D_KV * HEAD_DIM)
    cd_v = cache_d_v.reshape(nbs, n_odd, past, D_KV * HEAD_DIM)
    ak, av, sb, ckv, cpe, dk, dv = [], [], [], [], [], [], []
    for l in range(DEPTH):
        modp = mod_all[l, 0:1]
        mods = mod_all[l, 1:1 + nbs]
        ln = jnp.concatenate([ln_g[l, 0:1], ln_b[l, 0:1], ln_g[l, 1:2], ln_b[l, 1:2]], axis=0)
        wo, wg, wu, wd = _to_bf16([w_out, w_ffn_gate, w_ffn_up, w_ffn_down], l)
        tail_w = (wo, ln, wg, wu, wd)
        if l % 2 == 0:
            e = l // 2
            xp, xs, (k_new, v_new, s_new) = _even_layer(
                xp, xs, modp, mods, _to_bf16([w_in_ab], e)[0], lb_all[e], a_sink[e], b_gnorm[e], tail_w,
                rope_hd, ca_k, ca_v, state_b, e, nbp, nbs)
            ak.append(k_new)
            av.append(v_new)
            sb.append(s_new)
        else:
            o = l // 2
            w_in = w_in_cd[o]
            lat = C_Q_LORA + C_KV_LORA
            zeros = lambda n: jnp.zeros((D_MODEL, n), F32)
            w_in = jnp.concatenate([w_in[:, :lat], zeros(C_NOPE), w_in[:, lat:lat + C_ROPE],
                                    zeros(LANES - C_NOPE - C_ROPE), w_in[:, lat + C_ROPE:]], axis=1).astype(BF16)
            wq = c_w_q_up[o].reshape(C_Q_LORA, C_HEADS, C_NOPE + C_ROPE)
            wq = jnp.pad(wq, ((0, 0), (0, 0), (0, LANES - C_NOPE - C_ROPE))).reshape(C_Q_LORA, -1).astype(BF16)
            wkv = c_w_kv_up[o].reshape(C_KV_LORA, C_HEADS, C_NOPE + C_V)
            wk = jnp.pad(wkv[:, :, :C_NOPE], ((0, 0), (0, 0), (0, LANES - C_NOPE))).reshape(C_KV_LORA, -1)
            wkv = jnp.concatenate([wk, wkv[:, :, C_NOPE:].reshape(C_KV_LORA, -1)], axis=1).astype(BF16)
            in_w = (w_in, c_q_norm[o].reshape(1, -1), c_kv_norm[o].reshape(1, -1), wq, wkv,
                    jnp.tile(d_q_norm[o], D_HEADS).reshape(1, -1), jnp.tile(d_k_norm[o], D_KV).reshape(1, -1))
            xp, xs, (c_new, pe_new, k_new, v_new) = _odd_layer(
                xp, xs, modp, mods, in_w, tail_w, rope_hd, rope_c, cache_c_kv, cache_c_pe, cd_k, cd_v, o, nbp, nbs)
            ckv.append(c_new)
            cpe.append(pe_new)
            dk.append(k_new)
            dv.append(v_new)
    return (xp.reshape(nbp, seq_p, D_MODEL), xs.reshape(nbs, seq_s, D_MODEL),
            jnp.stack(ak, axis=1), jnp.stack(av, axis=1), jnp.stack(sb, axis=1),
            jnp.stack(ckv, axis=1), jnp.stack(cpe, axis=1), jnp.stack(dk, axis=1), jnp.stack(dv, axis=1))
```

```python
import functools

import numpy as np
import jax
import jax.numpy as jnp
from jax import lax
from jax.experimental import pallas as pl
from jax.experimental.pallas import tpu as pltpu

F32 = jnp.float32
BF16 = jnp.bfloat16

D_MODEL = 1024
DEPTH = 4
GRID_W = 64
HEAD_DIM = 64
ROPE_BASE = 10000.0
MASK_VALUE = -1e30
F_FLOOR = 1e-30
A_HEADS = 8
A_KV = 2
WINDOW = 128
B_HEADS = 4
B_DK = 128
B_DV = 128
C_HEADS = 8
C_Q_LORA = 384
C_KV_LORA = 256
C_NOPE = 64
C_ROPE = 32
C_V = 64
D_HEADS = 8
D_KV = 4
D_FF = 2816
ALPHA = (2 * DEPTH) ** 0.25

LANES = 128
HGRN_GROUP = 128
LOG2E = 1.4426950408889634
SAMPLE_TQ = 512
PROMPT_HGRN_HEADS = 4
SAMPLE_HGRN_HEADS = 2
PROMPT_SUB_A = 4
PROMPT_SUB_D = 2
SAMPLE_QT = 2
BAND_QT = 4
BAND_TQ = 256
VMEM_LIMIT = 56 * 1024 * 1024
NT_DIMS = (((1,), (1,)), ((), ()))
TN_DIMS = (((0,), (0,)), ((), ()))


def _cparams(n_axes):
    return pltpu.CompilerParams(dimension_semantics=("arbitrary",) * n_axes,
                                vmem_limit_bytes=VMEM_LIMIT)


def _silu(x):
    return x * jax.nn.sigmoid(x)


def _dot(a, b):
    return jnp.dot(a, b, preferred_element_type=F32)


def _layer_norm(x, g, b):
    mu = jnp.mean(x, axis=-1, keepdims=True)
    xc = x - mu
    var = jnp.mean(xc * xc, axis=-1, keepdims=True)
    return xc * lax.rsqrt(var + 1e-5) * g + b


def _rms(x, g):
    return x * lax.rsqrt(jnp.mean(x * x, axis=-1, keepdims=True) + 1e-6) * g


def _rms_heads64(x, g):
    outs = []
    for j in range(x.shape[1] // LANES):
        xg = x[:, j * LANES:(j + 1) * LANES]
        x2 = xg * xg
        lo = lax.broadcasted_iota(jnp.int32, xg.shape, 1) < HEAD_DIM
        s_lo = jnp.sum(jnp.where(lo, x2, 0.0), axis=-1, keepdims=True)
        s_hi = jnp.sum(jnp.where(lo, 0.0, x2), axis=-1, keepdims=True)
        ms = jnp.where(lo, s_lo, s_hi) * (1.0 / HEAD_DIM)
        outs.append(xg * lax.rsqrt(ms + 1e-6))
    return jnp.concatenate(outs, axis=1) * g


def _rope(x, cos, sin, half):
    outs = []
    for j in range(x.shape[1] // LANES):
        xg = x[:, j * LANES:(j + 1) * LANES]
        lane = lax.broadcasted_iota(jnp.int32, xg.shape, 1)
        up = pltpu.roll(xg, LANES - half, 1)
        down = pltpu.roll(xg, half, 1)
        rot = jnp.where((lane & (2 * half - 1)) < half, up, down)
        outs.append(xg * cos + rot * sin)
    return outs[0] if len(outs) == 1 else jnp.concatenate(outs, axis=1)


def _rope_tables(n_tokens, rot_dim):
    n_rows = n_tokens // GRID_W
    row = jnp.broadcast_to(jnp.arange(n_rows, dtype=F32)[:, None], (n_rows, GRID_W)).reshape(-1)
    col = jnp.broadcast_to(jnp.arange(GRID_W, dtype=F32)[None, :], (n_rows, GRID_W)).reshape(-1)
    quarter = rot_dim // 4
    inv = ROPE_BASE ** (-jnp.arange(quarter, dtype=F32) / quarter)
    ar = row[:, None] * inv
    ac = col[:, None] * inv
    cos = jnp.concatenate([jnp.cos(ar), jnp.cos(ar), jnp.cos(ac), jnp.cos(ac)], axis=-1)
    sin = jnp.concatenate([-jnp.sin(ar), jnp.sin(ar), -jnp.sin(ac), jnp.sin(ac)], axis=-1)
    reps = LANES // rot_dim
    return jnp.tile(cos, (1, reps)), jnp.tile(sin, (1, reps))


def _cast_kernel(*refs):
    n = len(refs) // 2
    for x_ref, o_ref in zip(refs[:n], refs[n:]):
        o_ref[...] = x_ref[...].astype(BF16)


def _to_bf16(ws, layer):
    n_blocks = 4
    in_specs, out_specs, out_shape = [], [], []
    for w in ws:
        _, rows, cols = w.shape
        in_specs.append(pl.BlockSpec((None, rows // n_blocks, cols), lambda i: (layer, i, 0)))
        out_specs.append(pl.BlockSpec((rows // n_blocks, cols), lambda i: (i, 0)))
        out_shape.append(jax.ShapeDtypeStruct((rows, cols), BF16))
    return pl.pallas_call(
        _cast_kernel,
        grid=(n_blocks,),
        in_specs=in_specs,
        out_specs=out_specs,
        out_shape=out_shape,
        compiler_params=_cparams(1),
        name="to_bf16",
    )(*ws)


def _mod_kernel(c_ref, w_ref, b_ref, o_ref):
    a = _silu(c_ref[...]).astype(BF16)
    o_ref[0] = _dot(a, w_ref[0].astype(BF16)) + b_ref[0]


def _modulation(c, c_ctx, w_ada, b_ada):
    nb = c.shape[0]
    cond = jnp.zeros((16, D_MODEL), F32).at[0].set(c_ctx).at[1:1 + nb].set(c)
    tn = 1536
    out = pl.pallas_call(
        _mod_kernel,
        grid=(DEPTH, 6 * D_MODEL // tn),
        in_specs=[pl.BlockSpec((16, D_MODEL), lambda l, j: (0, 0)),
                  pl.BlockSpec((1, D_MODEL, tn), lambda l, j: (l, 0, j)),
                  pl.BlockSpec((1, 1, tn), lambda l, j: (l, 0, j))],
        out_specs=pl.BlockSpec((1, 16, tn), lambda l, j: (l, 0, j)),
        out_shape=jax.ShapeDtypeStruct((DEPTH, 16, 6 * D_MODEL), F32),
        compiler_params=_cparams(2),
        name="modulation",
    )(cond, w_ada, b_ada.reshape(DEPTH, 1, 6 * D_MODEL))
    return out.reshape(DEPTH, 16, 6, D_MODEL)


def _modulate(x_ref, mod_ref, shift_row):
    m = mod_ref[0]
    return (x_ref[...] * (1.0 + m[shift_row + 1:shift_row + 2]) + m[shift_row:shift_row + 1]).astype(BF16)


def _inproj_even_kernel(rope, *refs):
    if rope:
        x_ref, mod_ref, w_ref, lb_ref, cos_ref, sin_ref = refs[:6]
        outs = refs[6:]
    else:
        x_ref, mod_ref, w_ref, lb_ref = refs[:4]
        outs = refs[4:]
    qa_ref, ka_ref, va_ref, qb_ref, ib_ref, kf_ref, lf_ref, kb_ref, lbw_ref, gb_ref = outs
    h = _modulate(x_ref, mod_ref, 0)

    def proj(a, b):
        return _dot(h, w_ref[:, a:b])

    for d, (k_ref, l_ref) in enumerate(((kf_ref, lf_ref), (kb_ref, lbw_ref))):
        lb = lb_ref[d:d + 1, :]
        f = lb + (1.0 - lb) * jax.nn.sigmoid(proj(1792 + 512 * d, 2304 + 512 * d))
        l_ref[...] = jnp.log(jnp.maximum(f, F_FLOOR))
        k_ref[...] = 1.0 - f
    qb_ref[...] = _silu(proj(768, 1280))
    gb_ref[...] = _silu(proj(2816, 3328))
    qa = proj(0, 512)
    ka = proj(512, 640)
    if rope:
        cos = cos_ref[...]
        sin = sin_ref[...]
        qa = _rope(qa, cos, sin, HEAD_DIM // 4)
        ka = _rope(ka, cos, sin, HEAD_DIM // 4)
    qa_ref[...] = qa.astype(BF16)
    ka_ref[...] = ka
    va_ref[...] = proj(640, 768)
    ib_ref[...] = proj(1280, 1792).astype(BF16)


def _inproj_even(x, mod, w_bf, lb, rope_tabs, rows_per_cond, tm=256):
    m_rows = x.shape[0]
    rope = rope_tabs is not None
    widths = (512, 128, 128, 512, 512, 512, 512, 512, 512, 512)
    dtypes = (BF16, F32, F32, F32, BF16, F32, F32, F32, F32, F32)
    in_specs = [pl.BlockSpec((tm, D_MODEL), lambda i: (i, 0)),
                pl.BlockSpec((1, 6, D_MODEL), lambda i: (i * tm // rows_per_cond, 0, 0)),
                pl.BlockSpec(w_bf.shape, lambda i: (0, 0)),
                pl.BlockSpec(lb.shape, lambda i: (0, 0))]
    args = [x, mod, w_bf, lb]
    if rope:
        nblk = rope_tabs[0].shape[0] // tm
        in_specs += [pl.BlockSpec((tm, LANES), lambda i: (i % nblk, 0))] * 2
        args += list(rope_tabs)
    return pl.pallas_call(
        functools.partial(_inproj_even_kernel, rope),
        grid=(m_rows // tm,),
        in_specs=in_specs,
        out_specs=[pl.BlockSpec((tm, w), lambda i: (i, 0)) for w in widths],
        out_shape=[jax.ShapeDtypeStruct((m_rows, w), dt) for w, dt in zip(widths, dtypes)],
        compiler_params=_cparams(1),
        name="inproj_even",
    )(*args)


def _inproj_odd_kernel(rope, *refs):
    if rope:
        (x_ref, mod_ref, w_ref, gcq_ref, gckv_ref, wq_ref, wkv_ref, gdq_ref, gdk_ref,
         cosd_ref, sind_ref, cosc_ref, sinc_ref) = refs[:13]
        outs = refs[13:]
    else:
        x_ref, mod_ref, w_ref, gcq_ref, gckv_ref, wq_ref, wkv_ref, gdq_ref, gdk_ref = refs[:9]
        outs = refs[9:]
    qc_ref, ckv_ref, kc_ref, vc_ref, pe_ref, qd_ref, kd_ref, vd_ref = outs
    h = _modulate(x_ref, mod_ref, 0)

    def proj(a, b):
        return _dot(h, w_ref[:, a:b])

    qd = _rms_heads64(proj(768, 1280), gdq_ref[...])
    kd = _rms_heads64(proj(1280, 1536), gdk_ref[...])
    if rope:
        qd = _rope(qd, cosd_ref[...], sind_ref[...], HEAD_DIM // 4)
        kd = _rope(kd, cosd_ref[...], sind_ref[...], HEAD_DIM // 4)
    qd_ref[...] = qd.astype(BF16)
    kd_ref[...] = kd
    cq = _rms(proj(0, 384), gcq_ref[...]).astype(BF16)
    qc = _dot(cq, wq_ref[...])
    pe = proj(640, 768)
    if rope:
        qc = _rope(qc, cosc_ref[...], sinc_ref[...], C_ROPE // 4)
        pe = _rope(pe, cosc_ref[...], sinc_ref[...], C_ROPE // 4)
    qc_ref[...] = qc.astype(BF16)
    pe_ref[...] = pe
    ckv = _rms(proj(384, 640), gckv_ref[...])
    ckv_ref[...] = ckv
    ckv_bf = ckv.astype(BF16)
    kn = _dot(ckv_bf, wkv_ref[:, 0:C_HEADS * LANES])
    kc_ref[...] = jnp.concatenate([kn[:, j * LANES:(j + 1) * LANES] + pe for j in range(C_HEADS)],
                                  axis=1).astype(BF16)
    vc_ref[...] = _dot(ckv_bf, wkv_ref[:, C_HEADS * LANES:]).astype(BF16)
    vd_ref[...] = proj(1536, 1792)


def _inproj_odd(x, mod, w_bf, gcq, gckv, wq_bf, wkv_bf, gdq, gdk, rope_tabs, rows_per_cond, tm=512):
    m_rows = x.shape[0]
    rope = rope_tabs is not None
    outs = ((C_HEADS * LANES, BF16), (C_KV_LORA, F32), (C_HEADS * LANES, BF16), (C_HEADS * C_V, BF16),
            (LANES, F32), (D_HEADS * HEAD_DIM, BF16), (D_KV * HEAD_DIM, F32), (D_KV * HEAD_DIM, F32))
    consts = [w_bf, gcq, gckv, wq_bf, wkv_bf, gdq, gdk]
    in_specs = [pl.BlockSpec((tm, D_MODEL), lambda i: (i, 0)),
                pl.BlockSpec((1, 6, D_MODEL), lambda i: (i * tm // rows_per_cond, 0, 0))]
    in_specs += [pl.BlockSpec(a.shape, lambda i: (0, 0)) for a in consts]
    args = [x, mod] + consts
    if rope:
        nblk = rope_tabs[0].shape[0] // tm
        in_specs += [pl.BlockSpec((tm, LANES), lambda i: (i % nblk, 0))] * 4
        args += list(rope_tabs)
    return pl.pallas_call(
        functools.partial(_inproj_odd_kernel, rope),
        grid=(m_rows // tm,),
        in_specs=in_specs,
        out_specs=[pl.BlockSpec((tm, w), lambda i: (i, 0)) for w, _ in outs],
        out_shape=[jax.ShapeDtypeStruct((m_rows, w), dt) for w, dt in outs],
        compiler_params=_cparams(1),
        name="inproj_odd",
    )(*args)


def _kvup_kernel(x_ref, pe_ref, w_ref, kc_ref, vc_ref):
    x = x_ref[...].astype(BF16)
    kn = _dot(x, w_ref[:, 0:C_HEADS * LANES])
    pe = pe_ref[...]
    kc_ref[...] = jnp.concatenate([kn[:, j * LANES:(j + 1) * LANES] + pe for j in range(C_HEADS)],
                                  axis=1).astype(BF16)
    vc_ref[...] = _dot(x, w_ref[:, C_HEADS * LANES:]).astype(BF16)


def _kvup(lat, pe_slab, wkv_bf, tm=512):
    m_rows = lat.shape[0]
    return pl.pallas_call(
        _kvup_kernel,
        grid=(m_rows // tm,),
        in_specs=[pl.BlockSpec((tm, C_KV_LORA), lambda i: (i, 0)),
                  pl.BlockSpec((tm, LANES), lambda i: (i, 0)),
                  pl.BlockSpec(wkv_bf.shape, lambda i: (0, 0))],
        out_specs=[pl.BlockSpec((tm, C_HEADS * LANES), lambda i: (i, 0)),
                   pl.BlockSpec((tm, C_HEADS * C_V), lambda i: (i, 0))],
        out_shape=[jax.ShapeDtypeStruct((m_rows, C_HEADS * LANES), BF16),
                   jax.ShapeDtypeStruct((m_rows, C_HEADS * C_V), BF16)],
        compiler_params=_cparams(1),
        name="kv_up_ctx",
    )(lat, pe_slab, wkv_bf)


def _attn_kernel(tq, n_heads, group, seq_k, past, band, has_sink, wide, score_mul, n_sub, n_qt, *refs):
    q_ref, k_ref, v_ref = refs[:3]
    idx = 3
    if past:
        kc_ref, vc_ref = refs[idx:idx + 2]
        idx += 2
    if has_sink:
        sink_ref = refs[idx]
        idx += 1
    o_ref, kpad, vaug = refs[idx:idx + 3]
    d = HEAD_DIM
    n_kv = n_heads // group
    n_slots = n_heads if wide else 2 * n_kv
    total = seq_k + past
    qi = pl.program_id(1)

    def _prepare_keys_values():
        for sub in range(n_sub):
            parts = [(k_ref, v_ref, sub * seq_k, 0, seq_k)]
            if past:
                parts.append((kc_ref, vc_ref, 0, seq_k, past))
            base = sub * n_slots
            for ks_ref, vs_ref, src0, r0, n in parts:
                src = slice(src0, src0 + n)
                rows = slice(r0, r0 + n)
                lo = lax.broadcasted_iota(jnp.int32, (n, LANES), 1) < d
                for j in range(n_kv * d // LANES):
                    cols = slice(j * LANES, (j + 1) * LANES)
                    vs = vs_ref[src, cols].astype(F32)
                    if wide:
                        vaug[base + 2 * j, rows, :] = jnp.where(lo, vs, 1.0).astype(BF16)
                        vaug[base + 2 * j + 1, rows, :] = jnp.where(lo, 1.0, vs).astype(BF16)
                        continue
                    vr = pltpu.roll(vs, d, 1)
                    ks = ks_ref[src, cols]
                    kr = pltpu.roll(ks, d, 1)
                    for g, (k_even, k_odd, v_even, v_odd) in ((2 * j, (ks, kr, vs, vr)), (2 * j + 1, (kr, ks, vr, vs))):
                        kpad[base + 2 * g, rows, :] = jnp.where(lo, k_even, 0.0).astype(BF16)
                        kpad[base + 2 * g + 1, rows, :] = jnp.where(lo, 0.0, k_odd).astype(BF16)
                        vaug[base + 2 * g, rows, :] = jnp.where(lo, v_even, 1.0).astype(BF16)
                        vaug[base + 2 * g + 1, rows, :] = jnp.where(lo, 1.0, v_odd).astype(BF16)
                if wide:
                    for h in range(n_heads):
                        kpad[base + h, rows, :] = ks_ref[src, h * LANES:(h + 1) * LANES]

    if seq_k == tq * n_qt:
        _prepare_keys_values()
    else:
        pl.when(qi == 0)(_prepare_keys_values)

    def key_window(tile):
        if not band:
            return [slice(0, total)], None
        kb = tq + 2 * WINDOW
        start = pl.multiple_of(jnp.clip(tile * tq - WINDOW, 0, seq_k - kb), LANES)
        diff = ((tile * tq - start) + lax.broadcasted_iota(jnp.int32, (tq, kb), 0)
                - lax.broadcasted_iota(jnp.int32, (tq, kb), 1))
        return [pl.ds(start, kb), pl.ds(seq_k, past)], jnp.abs(diff) <= WINDOW

    def nt(a, b):
        return lax.dot_general(a, b, NT_DIMS, preferred_element_type=F32)

    lo_out = lax.broadcasted_iota(jnp.int32, (tq, LANES), 1) < d
    for unit in range(n_sub * n_qt):
        base = (unit // n_qt) * n_slots
        q_rows = slice(unit * tq, (unit + 1) * tq)
        key_rows, bmask = key_window(qi * n_qt + unit % n_qt)
        q_slabs = {}
        results = {}
        for g in range(n_kv):
            for parity in (0, 1):
                heads = [h for h in range(g * group, (g + 1) * group) if h % 2 == parity]
                if not heads:
                    continue
                probs = [[] for _ in key_rows]
                maxes = []
                for h in heads:
                    slot = base + (h if wide else 2 * g + parity)
                    qs = h if wide else h // 2
                    if qs not in q_slabs:
                        q = q_ref[q_rows, qs * LANES:(qs + 1) * LANES]
                        q_slabs[qs] = q if wide else (q * d ** -0.5).astype(BF16)
                    scores = [nt(q_slabs[qs], kpad[slot, r, :]) for r in key_rows]
                    if score_mul is not None:
                        scores = [s * score_mul for s in scores]
                    if band:
                        scores[0] = jnp.where(bmask, scores[0], MASK_VALUE)
                    m = jnp.max(scores[0], axis=-1, keepdims=True)
                    for s in scores[1:]:
                        m = jnp.maximum(m, jnp.max(s, axis=-1, keepdims=True))
                    if has_sink:
                        m = jnp.maximum(m, sink_ref[h])
                    maxes.append(m)
                    for part, s in zip(probs, scores):
                        e = jnp.exp(s - m) if score_mul is None else jnp.exp2(s - m)
                        part.append(e.astype(BF16))
                vslot = base + (heads[0] if wide else 2 * g + parity)
                o_aug = None
                for part, r in zip(probs, key_rows):
                    p = part[0] if len(part) == 1 else jnp.concatenate(part, axis=0)
                    o = _dot(p, vaug[vslot, r, :])
                    o_aug = o if o_aug is None else o_aug + o
                den = pltpu.roll(o_aug, d, 1)
                for i, h in enumerate(heads):
                    rows = slice(i * tq, (i + 1) * tq)
                    den_h = den[rows]
                    if has_sink:
                        den_h = den_h + jnp.exp(sink_ref[h] - maxes[i])
                    results[h] = o_aug[rows] / den_h
        for j in range(n_heads // 2):
            o_ref[q_rows, j * LANES:(j + 1) * LANES] = jnp.where(lo_out, results[2 * j], results[2 * j + 1])


def _attention(q, k, v, n_batch, n_heads, group, tq, *, band=False, ctx=None, sink=None, wide=False,
               score_mul=None, n_sub=1, n_qt=1):
    seq = q.shape[0] // n_batch
    nq = seq // (tq * n_qt)
    n_kv = n_heads // group
    assert n_sub == 1 or (n_qt == 1 and nq == 1 and ctx is None and n_batch % n_sub == 0)
    in_specs = [pl.BlockSpec((n_sub * n_qt * tq, q.shape[1]), lambda b, i: (b * nq + i, 0)),
                pl.BlockSpec((n_sub * seq, k.shape[1]), lambda b, i: (b, 0)),
                pl.BlockSpec((n_sub * seq, v.shape[1]), lambda b, i: (b, 0))]
    args = [q, k, v]
    past = 0
    if ctx is not None:
        kc, vc, kc_spec, vc_spec, past = ctx
        in_specs += [kc_spec, vc_spec]
        args += [kc, vc]
    if sink is not None:
        in_specs.append(pl.BlockSpec(memory_space=pltpu.SMEM))
        args.append(sink)
    n_slots = n_sub * (n_heads if wide else 2 * n_kv)
    return pl.pallas_call(
        functools.partial(_attn_kernel, tq, n_heads, group, seq, past, band, sink is not None, wide, score_mul,
                          n_sub, n_qt),
        grid=(n_batch // n_sub, nq),
        in_specs=in_specs,
        out_specs=pl.BlockSpec((n_sub * n_qt * tq, n_heads * HEAD_DIM), lambda b, i: (b * nq + i, 0)),
        out_shape=jax.ShapeDtypeStruct((q.shape[0], n_heads * HEAD_DIM), F32),
        scratch_shapes=[pltpu.VMEM((n_slots, seq + past, LANES), BF16),
                        pltpu.VMEM((n_slots, seq + past, LANES), BF16)],
        compiler_params=_cparams(2),
        name="mla_attention" if wide else "gqa_attention",
    )(*args)


def _hgrn_levels():
    i = np.arange(HGRN_GROUP)
    t, s = i[:, None], i[None, :]
    level = np.where(t != s, np.floor(np.log2(np.maximum(t ^ s, 1))) + 1, 0).astype(np.int32)
    return (jnp.asarray(np.where(s <= t, level, -1), jnp.int32),
            jnp.asarray(np.where(s >= t, level, -1), jnp.int32))


def _scan_rows(x, rev):
    n = x.shape[0]
    row = lax.broadcasted_iota(jnp.int32, x.shape, 0)
    step = 1
    while step < n:
        if step < 8:
            if rev:
                x = x + jnp.where(row < n - step, pltpu.roll(x, n - step, 0), 0.0)
            else:
                x = x + jnp.where(row >= step, pltpu.roll(x, step, 0), 0.0)
        elif rev:
            x = jnp.concatenate([x[:n - step] + x[step:], x[n - step:]], axis=0)
        else:
            x = jnp.concatenate([x[:step], x[step:] + x[:n - step]], axis=0)
        step *= 2
    return x


def _level_refs(c, level, rev):
    n = c.shape[0]
    half = 1 << (level - 1)
    bs = 2 * half
    off = half if rev else half - 1
    if bs >= 16:
        return jnp.concatenate([jnp.broadcast_to(c[i * bs + off:i * bs + off + 1, :], (bs, LANES))
                                for i in range(n // bs)], axis=0)
    c3 = c.reshape(n // 8, 8, LANES)
    sub = lax.broadcasted_iota(jnp.int32, c3.shape, 1)
    out = None
    for j in reversed(range(8 // bs)):
        b = jnp.broadcast_to(c3[:, j * bs + off:j * bs + off + 1, :], c3.shape)
        out = b if out is None else jnp.where(sub < (j + 1) * bs, b, out)
    return out.reshape(n, LANES)


def _neg_abs(x):
    return lax.bitcast_convert_type(lax.bitcast_convert_type(x, jnp.uint32) | jnp.uint32(0x80000000), F32)


def _hgrn_kernel(seq, n_h, has_s0, want_state, *refs):
    q_ref, kf_ref, lf_ref, kb_ref, lbw_ref, v_ref, gb_ref, gn_ref, lvf_ref, lvb_ref = refs[:10]
    idx = 10
    if has_s0:
        s0_ref = refs[idx]
        idx += 1
    o_ref = refs[idx]
    idx += 1
    if want_state:
        sfin_ref = refs[idx]
        idx += 1
    of_scr, ob_scr, stf, stb = refs[idx:idx + 4]
    n_groups = seq // HGRN_GROUP
    scale = B_DK ** -0.5
    n_levels = HGRN_GROUP.bit_length() - 1
    chains = [(hh, d) + spec for hh in range(n_h)
              for d, spec in enumerate(((kf_ref, lf_ref, lvf_ref, of_scr, stf, False),
                                        (kb_ref, lbw_ref, lvb_ref, ob_scr, stb, True)))]

    for hh, d, _, _, _, _, st, _ in chains:
        st[hh] = s0_ref[d, hh].T if has_s0 else jnp.zeros((B_DV, B_DK), F32)

    def nt(a, b):
        return lax.dot_general(a.astype(BF16), b.astype(BF16), NT_DIMS, preferred_element_type=F32)

    def group_step(i, carry):
        work = []
        for hh, d, k_ref, l_ref, lv_ref, o_scr, st, rev in chains:
            g = (n_groups - 1 - i) if rev else i
            rows = pl.ds(pl.multiple_of(g * HGRN_GROUP, HGRN_GROUP), HGRN_GROUP)
            cols = slice(hh * LANES, (hh + 1) * LANES)
            logf = l_ref[rows, cols] * LOG2E
            c = _scan_rows(logf, rev)
            work.append(dict(rows=rows, cols=cols, hh=hh, c=c, logf=logf, q=q_ref[rows, cols] * scale,
                             k=k_ref[rows, cols], v=v_ref[rows, cols].astype(BF16), lv=lv_ref, o=o_scr, st=st,
                             rev=rev))
        for w in work:
            w["qb"] = w["q"].astype(BF16)
            w["kb"] = w["k"].astype(BF16)
            w["attn"] = jnp.where(w["lv"][...] == 0, nt(w["qb"], w["kb"]), 0.0)
        odd = (lax.broadcasted_iota(jnp.int32, (HGRN_GROUP, LANES), 0) & 1) == 1
        for level in range(1, n_levels + 1):
            for w in work:
                if level == 1:
                    arg = jnp.where(odd, 0.0, w["logf"]) if w["rev"] else jnp.where(odd, w["logf"], 0.0)
                else:
                    arg = _neg_abs(w["c"] - _level_refs(w["c"], level, w["rev"]))
                e = jnp.exp2(arg).astype(BF16)
                w["attn"] = jnp.where(w["lv"][...] == level, nt(w["qb"] * e, w["kb"] * e), w["attn"])
        for w in work:
            c, st, hh = w["c"], w["st"], w["hh"]
            tot = c[0:1] if w["rev"] else c[HGRN_GROUP - 1:HGRN_GROUP]
            s_t = st[hh]
            w["o"][w["rows"], w["cols"]] = _dot(w["attn"].astype(BF16), w["v"]) + nt(w["q"] * jnp.exp2(c), s_t)
            kt = (w["k"] * jnp.exp2(tot - c)).astype(BF16)
            st[hh] = s_t * jnp.exp2(tot) + lax.dot_general(w["v"], kt, TN_DIMS, preferred_element_type=F32)
        return carry

    lax.fori_loop(0, n_groups, group_step, 0, unroll=True)
    for hh in range(n_h):
        cols = slice(hh * LANES, (hh + 1) * LANES)
        if want_state:
            sfin_ref[0, hh] = stf[hh].T
            sfin_ref[1, hh] = stb[hh].T
        o_ref[:, cols] = _rms(of_scr[:, cols] + ob_scr[:, cols], gn_ref[...]) * gb_ref[:, cols]


def _hgrn(q, kf, lf, kb, lbw, v, gb, gnorm, n_batch, s0=None, want_state=False, n_h=1):
    seq = q.shape[0] // n_batch
    tok = pl.BlockSpec((seq, n_h * LANES), lambda b, h: (b, h))
    const = pl.BlockSpec((HGRN_GROUP, HGRN_GROUP), lambda b, h: (0, 0))
    in_specs = [tok] * 7 + [pl.BlockSpec((1, LANES), lambda b, h: (0, 0))] + [const] * 2
    args = [q, kf, lf, kb, lbw, v, gb, gnorm.reshape(1, LANES)] + list(_hgrn_levels())
    if s0 is not None:
        state, layer = s0
        in_specs.append(pl.BlockSpec((None, None, 2, n_h, B_DK, B_DV), lambda b, h: (b, layer, 0, h, 0, 0)))
        args.append(state)
    out_specs = [tok]
    out_shape = [jax.ShapeDtypeStruct(q.shape, F32)]
    if want_state:
        out_specs.append(pl.BlockSpec((None, 2, n_h, B_DK, B_DV), lambda b, h: (b, 0, h, 0, 0)))
        out_shape.append(jax.ShapeDtypeStruct((n_batch, 2, B_HEADS, B_DK, B_DV), F32))
    res = pl.pallas_call(
        functools.partial(_hgrn_kernel, seq, n_h, s0 is not None, want_state),
        grid=(n_batch, B_HEADS // n_h),
        in_specs=in_specs,
        out_specs=out_specs,
        out_shape=out_shape,
        scratch_shapes=[pltpu.VMEM((seq, n_h * LANES), F32),
                        pltpu.VMEM((seq, n_h * LANES), F32),
                        pltpu.VMEM((n_h, B_DV, B_DK), F32),
                        pltpu.VMEM((n_h, B_DV, B_DK), F32)],
        compiler_params=_cparams(2),
        name="hgrn2",
    )(*args)
    return res if want_state else (res[0], None)


def _outffn_kernel(o1_ref, o2_ref, x_ref, mod_ref, wo_ref, ln_ref, wg_ref, wu_ref, wd_ref, out_ref):
    m = mod_ref[0]
    half = o1_ref.shape[1]
    y = (_dot(o1_ref[...].astype(BF16), wo_ref[0:half, :])
         + _dot(o2_ref[...].astype(BF16), wo_ref[half:2 * half, :]))
    x1 = _layer_norm(ALPHA * x_ref[...] + m[2:3] * y, ln_ref[0:1, :], ln_ref[1:2, :])
    h = (x1 * (1.0 + m[4:5]) + m[3:4]).astype(BF16)
    acc = None
    for a in range(0, D_FF, 512):
        b = min(a + 512, D_FF)
        act = (_silu(_dot(h, wg_ref[:, a:b])) * _dot(h, wu_ref[:, a:b])).astype(BF16)
        part = _dot(act, wd_ref[a:b, :])
        acc = part if acc is None else acc + part
    out_ref[...] = _layer_norm(ALPHA * x1 + m[5:6] * acc, ln_ref[2:3, :], ln_ref[3:4, :])


def _outffn(o1, o2, x, mod, wo_bf, ln, wg_bf, wu_bf, wd_bf, rows_per_cond, tm=512):
    m_rows = x.shape[0]
    half = o1.shape[1]
    consts = [wo_bf, ln, wg_bf, wu_bf, wd_bf]
    in_specs = [pl.BlockSpec((tm, half), lambda i: (i, 0)),
                pl.BlockSpec((tm, half), lambda i: (i, 0)),
                pl.BlockSpec((tm, D_MODEL), lambda i: (i, 0)),
                pl.BlockSpec((1, 6, D_MODEL), lambda i: (i * tm // rows_per_cond, 0, 0))]
    in_specs += [pl.BlockSpec(a.shape, lambda i: (0, 0)) for a in consts]
    return pl.pallas_call(
        _outffn_kernel,
        grid=(m_rows // tm,),
        in_specs=in_specs,
        out_specs=pl.BlockSpec((tm, D_MODEL), lambda i: (i, 0)),
        out_shape=jax.ShapeDtypeStruct((m_rows, D_MODEL), F32),
        compiler_params=_cparams(1),
        name="outproj_ffn",
    )(o1, o2, x, mod, *consts)


def _cache_spec(past, width, layer):
    return pl.BlockSpec((None, None, past, width), lambda b, i: (b, layer, 0, 0))


def _even_layer(xp, xs, modp, mods, w_in_bf, lb, sink, gnorm, tail_w, rope_hd,
                cache_k, cache_v, state, e, nbp, nbs):
    seq_p = xp.shape[0] // nbp
    seq_s = xs.shape[0] // nbs
    past = cache_k.shape[2]
    pp = _inproj_even(xp, modp, w_in_bf, lb, None, xp.shape[0])
    ps = _inproj_even(xs, mods, w_in_bf, lb, rope_hd, seq_s)
    qa, ka, va, qb, ib, kf, lf, kb, lbw, gb = pp
    oa_p = _attention(qa, ka, va, nbp, A_HEADS, A_HEADS // A_KV, seq_p, sink=sink, n_sub=PROMPT_SUB_A)
    ob_p, s_new = _hgrn(qb, kf, lf, kb, lbw, ib, gb, gnorm, nbp, want_state=True, n_h=PROMPT_HGRN_HEADS)
    xp = _outffn(oa_p, ob_p, xp, modp, *tail_w, xp.shape[0])
    new = (ka.reshape(nbp, seq_p, A_KV, HEAD_DIM), va.reshape(nbp, seq_p, A_KV, HEAD_DIM), s_new)
    qa, ka, va, qb, ib, kf, lf, kb, lbw, gb = ps
    spec = _cache_spec(past, A_KV * HEAD_DIM, e)
    oa_s = _attention(qa, ka, va, nbs, A_HEADS, A_HEADS // A_KV, BAND_TQ, band=True,
                      ctx=(cache_k, cache_v, spec, spec, past), sink=sink, n_qt=BAND_QT)
    ob_s, _ = _hgrn(qb, kf, lf, kb, lbw, ib, gb, gnorm, nbs, s0=(state, e), n_h=SAMPLE_HGRN_HEADS)
    xs = _outffn(oa_s, ob_s, xs, mods, *tail_w, seq_s)
    return xp, xs, new


def _odd_layer(xp, xs, modp, mods, in_w, tail_w, rope_hd, rope_c,
               cache_ckv, cache_pe, cache_k, cache_v, o, nbp, nbs):
    seq_p = xp.shape[0] // nbp
    seq_s = xs.shape[0] // nbs
    past = cache_ckv.shape[2]
    wkv_bf = in_w[4]
    mla_mul = (C_NOPE + C_ROPE) ** -0.5 * LOG2E
    qc, ckv, kc, vc, pe, qd, kd, vd = _inproj_odd(xp, modp, *in_w, None, xp.shape[0])
    oc_p = _attention(qc, kc, vc, nbp, C_HEADS, 1, seq_p, wide=True, score_mul=mla_mul)
    od_p = _attention(qd, kd, vd, nbp, D_HEADS, D_HEADS // D_KV, seq_p, n_sub=PROMPT_SUB_D)
    xp = _outffn(oc_p, od_p, xp, modp, *tail_w, xp.shape[0])
    new = (ckv.reshape(nbp, seq_p, C_KV_LORA), pe[:, C_NOPE:C_NOPE + C_ROPE].reshape(nbp, seq_p, C_ROPE),
           kd.reshape(nbp, seq_p, D_KV, HEAD_DIM), vd.reshape(nbp, seq_p, D_KV, HEAD_DIM))
    qc, ckv, kc, vc, pe, qd, kd, vd = _inproj_odd(xs, mods, *in_w, rope_hd + rope_c, seq_s)
    pe_ctx = jnp.pad(cache_pe[:, o].reshape(nbs * past, C_ROPE), ((0, 0), (C_NOPE, LANES - C_NOPE - C_ROPE)))
    kcc, vcc = _kvup(cache_ckv[:, o].reshape(nbs * past, C_KV_LORA), pe_ctx, wkv_bf)
    oc_s = _attention(qc, kc, vc, nbs, C_HEADS, 1, SAMPLE_TQ, wide=True, score_mul=mla_mul,
                      ctx=(kcc, vcc, pl.BlockSpec((past, C_HEADS * LANES), lambda b, i: (b, 0)),
                           pl.BlockSpec((past, C_HEADS * C_V), lambda b, i: (b, 0)), past), n_qt=SAMPLE_QT)
    spec = _cache_spec(past, D_KV * HEAD_DIM, o)
    od_s = _attention(qd, kd, vd, nbs, D_HEADS, D_HEADS // D_KV, SAMPLE_TQ, ctx=(cache_k, cache_v, spec, spec, past),
                      n_qt=SAMPLE_QT)
    xs = _outffn(oc_s, od_s, xs, mods, *tail_w, seq_s)
    return xp, xs, new


def kernel(x_prompt, x_sample, cache_a_k, cache_a_v, state_b, cache_c_kv, cache_c_pe, cache_d_k, cache_d_v, c, c_ctx, w_ada, b_ada, ln_g, ln_b, w_in_ab, a_sink, b_lb, b_gnorm, w_in_cd, c_q_norm, c_kv_norm, c_w_q_up, c_w_kv_up, d_q_norm, d_k_norm, w_out, w_ffn_gate, w_ffn_up, w_ffn_down):
    nbp, seq_p, _ = x_prompt.shape
    nbs, seq_s, _ = x_sample.shape
    past = cache_a_k.shape[2]
    xp = x_prompt.reshape(nbp * seq_p, D_MODEL)
    xs = x_sample.reshape(nbs * seq_s, D_MODEL)
    rope_hd = _rope_tables(seq_s, HEAD_DIM)
    cos_c, sin_c = _rope_tables(seq_s, C_ROPE)
    ones = jnp.ones((seq_s, C_NOPE), F32)
    rope_c = (jnp.concatenate([ones, cos_c[:, :C_NOPE]], axis=1),
              jnp.concatenate([0.0 * ones, sin_c[:, :C_ROPE], 0.0 * ones[:, :LANES - C_NOPE - C_ROPE]], axis=1))
    lb_w = jax.nn.softmax(b_lb.astype(F32), axis=0)
    lb_all = jnp.cumsum(lb_w, axis=0) - lb_w[:1]
    mod_all = _modulation(c, c_ctx, w_ada, b_ada)
    n_even = cache_a_k.shape[1]
    n_odd = cache_c_kv.shape[1]
    ca_k = cache_a_k.reshape(nbs, n_even, past, A_KV * HEAD_DIM)
    ca_v = cache_a_v.reshape(nbs, n_even, past, A_KV * HEAD_DIM)
    cd_k = cache_d_k.reshape(nbs, n_odd, past, D_KV * HEAD_DIM)
    cd_v = cache_d_v.reshape(nbs, n_odd, past, D_KV * HEAD_DIM)
    ak, av, sb, ckv, cpe, dk, dv = [], [], [], [], [], [], []
    for l in range(DEPTH):
        modp = mod_all[l, 0:1]
        mods = mod_all[l, 1:1 + nbs]
        ln = jnp.concatenate([ln_g[l, 0:1], ln_b[l, 0:1], ln_g[l, 1:2], ln_b[l, 1:2]], axis=0)
        wo, wg, wu, wd = _to_bf16([w_out, w_ffn_gate, w_ffn_up, w_ffn_down], l)
        tail_w = (wo, ln, wg, wu, wd)
        if l % 2 == 0:
            e = l // 2
            xp, xs, (k_new, v_new, s_new) = _even_layer(
                xp, xs, modp, mods, _to_bf16([w_in_ab], e)[0], lb_all[e], a_sink[e], b_gnorm[e], tail_w,
                rope_hd, ca_k, ca_v, state_b, e, nbp, nbs)
            ak.append(k_new)
            av.append(v_new)
            sb.append(s_new)
        else:
            o = l // 2
            w_in = w_in_cd[o]
            lat = C_Q_LORA + C_KV_LORA
            zeros = lambda n: jnp.zeros((D_MODEL, n), F32)
            w_in = jnp.concatenate([w_in[:, :lat], zeros(C_NOPE), w_in[:, lat:lat + C_ROPE],
                                    zeros(LANES - C_NOPE - C_ROPE), w_in[:, lat + C_ROPE:]], axis=1).astype(BF16)
            wq = c_w_q_up[o].reshape(C_Q_LORA, C_HEADS, C_NOPE + C_ROPE)
            wq = jnp.pad(wq, ((0, 0), (0, 0), (0, LANES - C_NOPE - C_ROPE))).reshape(C_Q_LORA, -1).astype(BF16)
            wkv = c_w_kv_up[o].reshape(C_KV_LORA, C_HEADS, C_NOPE + C_V)
            wk = jnp.pad(wkv[:, :, :C_NOPE], ((0, 0), (0, 0), (0, LANES - C_NOPE))).reshape(C_KV_LORA, -1)
            wkv = jnp.concatenate([wk, wkv[:, :, C_NOPE:].reshape(C_KV_LORA, -1)], axis=1).astype(BF16)
            in_w = (w_in, c_q_norm[o].reshape(1, -1), c_kv_norm[o].reshape(1, -1), wq, wkv,
                    jnp.tile(d_q_norm[o], D_HEADS).reshape(1, -1), jnp.tile(d_k_norm[o], D_KV).reshape(1, -1))
            xp, xs, (c_new, pe_new, k_new, v_new) = _odd_layer(
                xp, xs, modp, mods, in_w, tail_w, rope_hd, rope_c, cache_c_kv, cache_c_pe, cd_k, cd_v, o, nbp, nbs)
            ckv.append(c_new)
            cpe.append(pe_new)
            dk.append(k_new)
            dv.append(v_new)
    return (xp.reshape(nbp, seq_p, D_MODEL), xs.reshape(nbs, seq_s, D_MODEL),
            jnp.stack(ak, axis=1), jnp.stack(av, axis=1), jnp.stack(sb, axis=1),
            jnp.stack(ckv, axis=1), jnp.stack(cpe, axis=1), jnp.stack(dk, axis=1), jnp.stack(dv, axis=1))
```

```python
import functools

import numpy as np
import jax
import jax.numpy as jnp
from jax import lax
from jax.experimental import pallas as pl
from jax.experimental.pallas import tpu as pltpu

F32 = jnp.float32
BF16 = jnp.bfloat16

D_MODEL = 1024
DEPTH = 4
GRID_W = 64
HEAD_DIM = 64
ROPE_BASE = 10000.0
MASK_VALUE = -1e30
F_FLOOR = 1e-30
A_HEADS = 8
A_KV = 2
WINDOW = 128
B_HEADS = 4
B_DK = 128
B_DV = 128
C_HEADS = 8
C_Q_LORA = 384
C_KV_LORA = 256
C_NOPE = 64
C_ROPE = 32
C_V = 64
D_HEADS = 8
D_KV = 4
D_FF = 2816
ALPHA = (2 * DEPTH) ** 0.25

LANES = 128
HGRN_GROUP = 128
LOG2E = 1.4426950408889634
PROMPT_HGRN_HEADS = 4
SAMPLE_HGRN_HEADS = 2
PROMPT_SUB_A = 4
PROMPT_SUB_D = 2
SAMPLE_TQ = 512
SAMPLE_QT = 2
BAND_TQ = 256
BAND_QT = 4
VMEM_LIMIT = 56 * 1024 * 1024
NT_DIMS = (((1,), (1,)), ((), ()))
TN_DIMS = (((0,), (0,)), ((), ()))


def _cparams(n_axes):
    return pltpu.CompilerParams(dimension_semantics=("arbitrary",) * n_axes,
                                vmem_limit_bytes=VMEM_LIMIT)


def _silu(x):
    return x * jax.nn.sigmoid(x)


def _dot(a, b):
    return jnp.dot(a, b, preferred_element_type=F32)


def _layer_norm(x, g, b):
    mu = jnp.mean(x, axis=-1, keepdims=True)
    xc = x - mu
    var = jnp.mean(xc * xc, axis=-1, keepdims=True)
    return xc * lax.rsqrt(var + 1e-5) * g + b


def _rms(x, g):
    return x * lax.rsqrt(jnp.mean(x * x, axis=-1, keepdims=True) + 1e-6) * g


def _rms_heads64(x, g):
    outs = []
    for j in range(x.shape[1] // LANES):
        xg = x[:, j * LANES:(j + 1) * LANES]
        x2 = xg * xg
        lo = lax.broadcasted_iota(jnp.int32, xg.shape, 1) < HEAD_DIM
        s_lo = jnp.sum(jnp.where(lo, x2, 0.0), axis=-1, keepdims=True)
        s_hi = jnp.sum(jnp.where(lo, 0.0, x2), axis=-1, keepdims=True)
        ms = jnp.where(lo, s_lo, s_hi) * (1.0 / HEAD_DIM)
        outs.append(xg * lax.rsqrt(ms + 1e-6))
    return jnp.concatenate(outs, axis=1) * g


def _rope(x, cos, sin, half):
    outs = []
    for j in range(x.shape[1] // LANES):
        xg = x[:, j * LANES:(j + 1) * LANES]
        lane = lax.broadcasted_iota(jnp.int32, xg.shape, 1)
        up = pltpu.roll(xg, LANES - half, 1)
        down = pltpu.roll(xg, half, 1)
        rot = jnp.where((lane & (2 * half - 1)) < half, up, down)
        outs.append(xg * cos + rot * sin)
    return outs[0] if len(outs) == 1 else jnp.concatenate(outs, axis=1)


def _rope_tables(n_tokens, rot_dim):
    n_rows = n_tokens // GRID_W
    row = jnp.broadcast_to(jnp.arange(n_rows, dtype=F32)[:, None], (n_rows, GRID_W)).reshape(-1)
    col = jnp.broadcast_to(jnp.arange(GRID_W, dtype=F32)[None, :], (n_rows, GRID_W)).reshape(-1)
    quarter = rot_dim // 4
    inv = ROPE_BASE ** (-jnp.arange(quarter, dtype=F32) / quarter)
    ar = row[:, None] * inv
    ac = col[:, None] * inv
    cos = jnp.concatenate([jnp.cos(ar), jnp.cos(ar), jnp.cos(ac), jnp.cos(ac)], axis=-1)
    sin = jnp.concatenate([-jnp.sin(ar), jnp.sin(ar), -jnp.sin(ac), jnp.sin(ac)], axis=-1)
    reps = LANES // rot_dim
    return jnp.tile(cos, (1, reps)), jnp.tile(sin, (1, reps))


def _cast_kernel(*refs):
    n = len(refs) // 2
    for x_ref, o_ref in zip(refs[:n], refs[n:]):
        o_ref[...] = x_ref[...].astype(BF16)


def _to_bf16(ws, layer):
    n_blocks = 4
    in_specs, out_specs, out_shape = [], [], []
    for w in ws:
        _, rows, cols = w.shape
        in_specs.append(pl.BlockSpec((None, rows // n_blocks, cols), lambda i: (layer, i, 0)))
        out_specs.append(pl.BlockSpec((rows // n_blocks, cols), lambda i: (i, 0)))
        out_shape.append(jax.ShapeDtypeStruct((rows, cols), BF16))
    return pl.pallas_call(
        _cast_kernel,
        grid=(n_blocks,),
        in_specs=in_specs,
        out_specs=out_specs,
        out_shape=out_shape,
        compiler_params=_cparams(1),
        name="to_bf16",
    )(*ws)


def _mod_kernel(c_ref, w_ref, b_ref, o_ref):
    a = _silu(c_ref[...]).astype(BF16)
    o_ref[0] = _dot(a, w_ref[0].astype(BF16)) + b_ref[0]


def _modulation(c, c_ctx, w_ada, b_ada):
    nb = c.shape[0]
    cond = jnp.zeros((16, D_MODEL), F32).at[0].set(c_ctx).at[1:1 + nb].set(c)
    tn = 1536
    out = pl.pallas_call(
        _mod_kernel,
        grid=(DEPTH, 6 * D_MODEL // tn),
        in_specs=[pl.BlockSpec((16, D_MODEL), lambda l, j: (0, 0)),
                  pl.BlockSpec((1, D_MODEL, tn), lambda l, j: (l, 0, j)),
                  pl.BlockSpec((1, 1, tn), lambda l, j: (l, 0, j))],
        out_specs=pl.BlockSpec((1, 16, tn), lambda l, j: (l, 0, j)),
        out_shape=jax.ShapeDtypeStruct((DEPTH, 16, 6 * D_MODEL), F32),
        compiler_params=_cparams(2),
        name="modulation",
    )(cond, w_ada, b_ada.reshape(DEPTH, 1, 6 * D_MODEL))
    return out.reshape(DEPTH, 16, 6, D_MODEL)


def _modulate(x_ref, mod_ref, shift_row):
    m = mod_ref[0]
    return (x_ref[...] * (1.0 + m[shift_row + 1:shift_row + 2]) + m[shift_row:shift_row + 1]).astype(BF16)


def _inproj_even_kernel(rope, *refs):
    if rope:
        x_ref, mod_ref, w_ref, lb_ref, cos_ref, sin_ref = refs[:6]
        outs = refs[6:]
    else:
        x_ref, mod_ref, w_ref, lb_ref = refs[:4]
        outs = refs[4:]
    qa_ref, ka_ref, va_ref, qb_ref, ib_ref, kf_ref, lf_ref, kb_ref, lbw_ref, gb_ref = outs
    h = _modulate(x_ref, mod_ref, 0)

    def proj(a, b):
        return _dot(h, w_ref[:, a:b])

    for d, (k_ref, l_ref) in enumerate(((kf_ref, lf_ref), (kb_ref, lbw_ref))):
        lb = lb_ref[d:d + 1, :]
        f = lb + (1.0 - lb) * jax.nn.sigmoid(proj(1792 + 512 * d, 2304 + 512 * d))
        l_ref[...] = jnp.log(jnp.maximum(f, F_FLOOR))
        k_ref[...] = 1.0 - f
    qb_ref[...] = _silu(proj(768, 1280))
    gb_ref[...] = _silu(proj(2816, 3328))
    qa = proj(0, 512)
    ka = proj(512, 640)
    if rope:
        cos = cos_ref[...]
        sin = sin_ref[...]
        qa = _rope(qa, cos, sin, HEAD_DIM // 4)
        ka = _rope(ka, cos, sin, HEAD_DIM // 4)
    qa_ref[...] = qa.astype(BF16)
    ka_ref[...] = ka
    va_ref[...] = proj(640, 768)
    ib_ref[...] = proj(1280, 1792).astype(BF16)


def _inproj_even(x, mod, w_bf, lb, rope_tabs, rows_per_cond, tm=256):
    m_rows = x.shape[0]
    rope = rope_tabs is not None
    widths = (512, 128, 128, 512, 512, 512, 512, 512, 512, 512)
    dtypes = (BF16, F32, F32, F32, BF16, F32, F32, F32, F32, F32)
    in_specs = [pl.BlockSpec((tm, D_MODEL), lambda i: (i, 0)),
                pl.BlockSpec((1, 6, D_MODEL), lambda i: (i * tm // rows_per_cond, 0, 0)),
                pl.BlockSpec(w_bf.shape, lambda i: (0, 0)),
                pl.BlockSpec(lb.shape, lambda i: (0, 0))]
    args = [x, mod, w_bf, lb]
    if rope:
        nblk = rope_tabs[0].shape[0] // tm
        in_specs += [pl.BlockSpec((tm, LANES), lambda i: (i % nblk, 0))] * 2
        args += list(rope_tabs)
    return pl.pallas_call(
        functools.partial(_inproj_even_kernel, rope),
        grid=(m_rows // tm,),
        in_specs=in_specs,
        out_specs=[pl.BlockSpec((tm, w), lambda i: (i, 0)) for w in widths],
        out_shape=[jax.ShapeDtypeStruct((m_rows, w), dt) for w, dt in zip(widths, dtypes)],
        compiler_params=_cparams(1),
        name="inproj_even",
    )(*args)


def _inproj_odd_kernel(rope, *refs):
    if rope:
        (x_ref, mod_ref, w_ref, gcq_ref, gckv_ref, wq_ref, wkv_ref, gdq_ref, gdk_ref,
         cosd_ref, sind_ref, cosc_ref, sinc_ref) = refs[:13]
        outs = refs[13:]
    else:
        x_ref, mod_ref, w_ref, gcq_ref, gckv_ref, wq_ref, wkv_ref, gdq_ref, gdk_ref = refs[:9]
        outs = refs[9:]
    qc_ref, ckv_ref, kc_ref, vc_ref, pe_ref, qd_ref, kd_ref, vd_ref = outs
    h = _modulate(x_ref, mod_ref, 0)

    def proj(a, b):
        return _dot(h, w_ref[:, a:b])

    qd = _rms_heads64(proj(768, 1280), gdq_ref[...])
    kd = _rms_heads64(proj(1280, 1536), gdk_ref[...])
    if rope:
        qd = _rope(qd, cosd_ref[...], sind_ref[...], HEAD_DIM // 4)
        kd = _rope(kd, cosd_ref[...], sind_ref[...], HEAD_DIM // 4)
    qd_ref[...] = qd.astype(BF16)
    kd_ref[...] = kd
    cq = _rms(proj(0, 384), gcq_ref[...]).astype(BF16)
    qc = _dot(cq, wq_ref[...])
    pe = proj(640, 768)
    if rope:
        qc = _rope(qc, cosc_ref[...], sinc_ref[...], C_ROPE // 4)
        pe = _rope(pe, cosc_ref[...], sinc_ref[...], C_ROPE // 4)
    qc_ref[...] = qc.astype(BF16)
    pe_ref[...] = pe
    ckv = _rms(proj(384, 640), gckv_ref[...])
    ckv_ref[...] = ckv
    ckv_bf = ckv.astype(BF16)
    kn = _dot(ckv_bf, wkv_ref[:, 0:C_HEADS * LANES])
    kc_ref[...] = jnp.concatenate([kn[:, j * LANES:(j + 1) * LANES] + pe for j in range(C_HEADS)],
                                  axis=1).astype(BF16)
    vc_ref[...] = _dot(ckv_bf, wkv_ref[:, C_HEADS * LANES:]).astype(BF16)
    vd_ref[...] = proj(1536, 1792)


def _inproj_odd(x, mod, w_bf, gcq, gckv, wq_bf, wkv_bf, gdq, gdk, rope_tabs, rows_per_cond, tm=512):
    m_rows = x.shape[0]
    rope = rope_tabs is not None
    outs = ((C_HEADS * LANES, BF16), (C_KV_LORA, F32), (C_HEADS * LANES, BF16), (C_HEADS * C_V, BF16),
            (LANES, F32), (D_HEADS * HEAD_DIM, BF16), (D_KV * HEAD_DIM, F32), (D_KV * HEAD_DIM, F32))
    consts = [w_bf, gcq, gckv, wq_bf, wkv_bf, gdq, gdk]
    in_specs = [pl.BlockSpec((tm, D_MODEL), lambda i: (i, 0)),
                pl.BlockSpec((1, 6, D_MODEL), lambda i: (i * tm // rows_per_cond, 0, 0))]
    in_specs += [pl.BlockSpec(a.shape, lambda i: (0, 0)) for a in consts]
    args = [x, mod] + consts
    if rope:
        nblk = rope_tabs[0].shape[0] // tm
        in_specs += [pl.BlockSpec((tm, LANES), lambda i: (i % nblk, 0))] * 4
        args += list(rope_tabs)
    return pl.pallas_call(
        functools.partial(_inproj_odd_kernel, rope),
        grid=(m_rows // tm,),
        in_specs=in_specs,
        out_specs=[pl.BlockSpec((tm, w), lambda i: (i, 0)) for w, _ in outs],
        out_shape=[jax.ShapeDtypeStruct((m_rows, w), dt) for w, dt in outs],
        compiler_params=_cparams(1),
        name="inproj_odd",
    )(*args)


def _kvup_kernel(x_ref, pe_ref, w_ref, kc_ref, vc_ref):
    x = x_ref[...].astype(BF16)
    kn = _dot(x, w_ref[:, 0:C_HEADS * LANES])
    pe = pe_ref[...]
    kc_ref[...] = jnp.concatenate([kn[:, j * LANES:(j + 1) * LANES] + pe for j in range(C_HEADS)],
                                  axis=1).astype(BF16)
    vc_ref[...] = _dot(x, w_ref[:, C_HEADS * LANES:]).astype(BF16)


def _kvup(lat, pe_slab, wkv_bf, tm=512):
    m_rows = lat.shape[0]
    return pl.pallas_call(
        _kvup_kernel,
        grid=(m_rows // tm,),
        in_specs=[pl.BlockSpec((tm, C_KV_LORA), lambda i: (i, 0)),
                  pl.BlockSpec((tm, LANES), lambda i: (i, 0)),
                  pl.BlockSpec(wkv_bf.shape, lambda i: (0, 0))],
        out_specs=[pl.BlockSpec((tm, C_HEADS * LANES), lambda i: (i, 0)),
                   pl.BlockSpec((tm, C_HEADS * C_V), lambda i: (i, 0))],
        out_shape=[jax.ShapeDtypeStruct((m_rows, C_HEADS * LANES), BF16),
                   jax.ShapeDtypeStruct((m_rows, C_HEADS * C_V), BF16)],
        compiler_params=_cparams(1),
        name="kv_up_ctx",
    )(lat, pe_slab, wkv_bf)


def _attn_kernel(tq, n_heads, group, seq_k, past, band, has_sink, wide, score_mul, n_sub, n_qt, *refs):
    q_ref, k_ref, v_ref = refs[:3]
    idx = 3
    if past:
        kc_ref, vc_ref = refs[idx:idx + 2]
        idx += 2
    if has_sink:
        sink_ref = refs[idx]
        idx += 1
    o_ref, kpad, vaug = refs[idx:idx + 3]
    d = HEAD_DIM
    n_kv = n_heads // group
    n_slots = n_heads if wide else 2 * n_kv
    total = seq_k + past
    qi = pl.program_id(1)

    def _prepare_keys_values():
        for sub in range(n_sub):
            parts = [(k_ref, v_ref, sub * seq_k, 0, seq_k)]
            if past:
                parts.append((kc_ref, vc_ref, 0, seq_k, past))
            base = sub * n_slots
            for ks_ref, vs_ref, src0, r0, n in parts:
                src = slice(src0, src0 + n)
                rows = slice(r0, r0 + n)
                lo = lax.broadcasted_iota(jnp.int32, (n, LANES), 1) < d
                for j in range(n_kv * d // LANES):
                    cols = slice(j * LANES, (j + 1) * LANES)
                    vs = vs_ref[src, cols].astype(F32)
                    if wide:
                        vaug[base + 2 * j, rows, :] = jnp.where(lo, vs, 1.0).astype(BF16)
                        vaug[base + 2 * j + 1, rows, :] = jnp.where(lo, 1.0, vs).astype(BF16)
                        continue
                    vr = pltpu.roll(vs, d, 1)
                    ks = ks_ref[src, cols]
                    kr = pltpu.roll(ks, d, 1)
                    for g, (k_even, k_odd, v_even, v_odd) in ((2 * j, (ks, kr, vs, vr)), (2 * j + 1, (kr, ks, vr, vs))):
                        kpad[base + 2 * g, rows, :] = jnp.where(lo, k_even, 0.0).astype(BF16)
                        kpad[base + 2 * g + 1, rows, :] = jnp.where(lo, 0.0, k_odd).astype(BF16)
                        vaug[base + 2 * g, rows, :] = jnp.where(lo, v_even, 1.0).astype(BF16)
                        vaug[base + 2 * g + 1, rows, :] = jnp.where(lo, 1.0, v_odd).astype(BF16)
                if wide:
                    for h in range(n_heads):
                        kpad[base + h, rows, :] = ks_ref[src, h * LANES:(h + 1) * LANES]

    if seq_k == tq * n_qt:
        _prepare_keys_values()
    else:
        pl.when(qi == 0)(_prepare_keys_values)

    def key_window(tile):
        if not band:
            return [slice(0, total)], None
        kb = tq + 2 * WINDOW
        start = pl.multiple_of(jnp.clip(tile * tq - WINDOW, 0, seq_k - kb), LANES)
        diff = ((tile * tq - start) + lax.broadcasted_iota(jnp.int32, (tq, kb), 0)
                - lax.broadcasted_iota(jnp.int32, (tq, kb), 1))
        return [pl.ds(start, kb), pl.ds(seq_k, past)], jnp.abs(diff) <= WINDOW

    def nt(a, b):
        return lax.dot_general(a, b, NT_DIMS, preferred_element_type=F32)

    lo_out = lax.broadcasted_iota(jnp.int32, (tq, LANES), 1) < d
    for unit in range(n_sub * n_qt):
        base = (unit // n_qt) * n_slots
        q_rows = slice(unit * tq, (unit + 1) * tq)
        key_rows, bmask = key_window(qi * n_qt + unit % n_qt)
        q_slabs = {}
        results = {}
        for g in range(n_kv):
            for parity in (0, 1):
                heads = [h for h in range(g * group, (g + 1) * group) if h % 2 == parity]
                if not heads:
                    continue
                probs = [[] for _ in key_rows]
                maxes = []
                for h in heads:
                    slot = base + (h if wide else 2 * g + parity)
                    qs = h if wide else h // 2
                    if qs not in q_slabs:
                        q = q_ref[q_rows, qs * LANES:(qs + 1) * LANES]
                        q_slabs[qs] = q if wide else (q * d ** -0.5).astype(BF16)
                    scores = [nt(q_slabs[qs], kpad[slot, r, :]) for r in key_rows]
                    if score_mul is not None:
                        scores = [s * score_mul for s in scores]
                    if band:
                        scores[0] = jnp.where(bmask, scores[0], MASK_VALUE)
                    m = jnp.max(scores[0], axis=-1, keepdims=True)
                    for s in scores[1:]:
                        m = jnp.maximum(m, jnp.max(s, axis=-1, keepdims=True))
                    if has_sink:
                        m = jnp.maximum(m, sink_ref[h])
                    maxes.append(m)
                    for part, s in zip(probs, scores):
                        e = jnp.exp(s - m) if score_mul is None else jnp.exp2(s - m)
                        part.append(e.astype(BF16))
                vslot = base + (heads[0] if wide else 2 * g + parity)
                o_aug = None
                for part, r in zip(probs, key_rows):
                    p = part[0] if len(part) == 1 else jnp.concatenate(part, axis=0)
                    o = _dot(p, vaug[vslot, r, :])
                    o_aug = o if o_aug is None else o_aug + o
                den = pltpu.roll(o_aug, d, 1)
                for i, h in enumerate(heads):
                    rows = slice(i * tq, (i + 1) * tq)
                    den_h = den[rows]
                    if has_sink:
                        den_h = den_h + jnp.exp(sink_ref[h] - maxes[i])
                    results[h] = o_aug[rows] / den_h
        for j in range(n_heads // 2):
            o_ref[q_rows, j * LANES:(j + 1) * LANES] = jnp.where(lo_out, results[2 * j], results[2 * j + 1])


def _attention(q, k, v, n_batch, n_heads, group, tq, *, band=False, ctx=None, sink=None, wide=False,
               score_mul=None, n_sub=1, n_qt=1):
    seq = q.shape[0] // n_batch
    nq = seq // (tq * n_qt)
    n_kv = n_heads // group
    assert n_sub == 1 or (n_qt == 1 and nq == 1 and ctx is None and n_batch % n_sub == 0)
    in_specs = [pl.BlockSpec((n_sub * n_qt * tq, q.shape[1]), lambda b, i: (b * nq + i, 0)),
                pl.BlockSpec((n_sub * seq, k.shape[1]), lambda b, i: (b, 0)),
                pl.BlockSpec((n_sub * seq, v.shape[1]), lambda b, i: (b, 0))]
    args = [q, k, v]
    past = 0
    if ctx is not None:
        kc, vc, kc_spec, vc_spec, past = ctx
        in_specs += [kc_spec, vc_spec]
        args += [kc, vc]
    if sink is not None:
        in_specs.append(pl.BlockSpec(memory_space=pltpu.SMEM))
        args.append(sink)
    n_slots = n_sub * (n_heads if wide else 2 * n_kv)
    return pl.pallas_call(
        functools.partial(_attn_kernel, tq, n_heads, group, seq, past, band, sink is not None, wide, score_mul,
                          n_sub, n_qt),
        grid=(n_batch // n_sub, nq),
        in_specs=in_specs,
        out_specs=pl.BlockSpec((n_sub * n_qt * tq, n_heads * HEAD_DIM), lambda b, i: (b * nq + i, 0)),
        out_shape=jax.ShapeDtypeStruct((q.shape[0], n_heads * HEAD_DIM), F32),
        scratch_shapes=[pltpu.VMEM((n_slots, seq + past, LANES), BF16),
                        pltpu.VMEM((n_slots, seq + past, LANES), BF16)],
        compiler_params=_cparams(2),
        name="mla_attention" if wide else "gqa_attention",
    )(*args)


def _hgrn_levels():
    i = np.arange(HGRN_GROUP)
    t, s = i[:, None], i[None, :]
    level = np.where(t != s, np.floor(np.log2(np.maximum(t ^ s, 1))) + 1, 0).astype(np.int32)
    return (jnp.asarray(np.where(s <= t, level, -1), jnp.int32),
            jnp.asarray(np.where(s >= t, level, -1), jnp.int32))


def _scan_rows(x, rev):
    n = x.shape[0]
    row = lax.broadcasted_iota(jnp.int32, x.shape, 0)
    step = 1
    while step < n:
        if step < 8:
            if rev:
                x = x + jnp.where(row < n - step, pltpu.roll(x, n - step, 0), 0.0)
            else:
                x = x + jnp.where(row >= step, pltpu.roll(x, step, 0), 0.0)
        elif rev:
            x = jnp.concatenate([x[:n - step] + x[step:], x[n - step:]], axis=0)
        else:
            x = jnp.concatenate([x[:step], x[step:] + x[:n - step]], axis=0)
        step *= 2
    return x


def _level_refs(c, level, rev):
    n = c.shape[0]
    half = 1 << (level - 1)
    bs = 2 * half
    off = half if rev else half - 1
    if bs >= 16:
        return jnp.concatenate([jnp.broadcast_to(c[i * bs + off:i * bs + off + 1, :], (bs, LANES))
                                for i in range(n // bs)], axis=0)
    c3 = c.reshape(n // 8, 8, LANES)
    sub = lax.broadcasted_iota(jnp.int32, c3.shape, 1)
    out = None
    for j in reversed(range(8 // bs)):
        b = jnp.broadcast_to(c3[:, j * bs + off:j * bs + off + 1, :], c3.shape)
        out = b if out is None else jnp.where(sub < (j + 1) * bs, b, out)
    return out.reshape(n, LANES)


def _neg_abs(x):
    return lax.bitcast_convert_type(lax.bitcast_convert_type(x, jnp.uint32) | jnp.uint32(0x80000000), F32)


def _hgrn_kernel(seq, n_h, has_s0, want_state, *refs):
    q_ref, kf_ref, lf_ref, kb_ref, lbw_ref, v_ref, gb_ref, gn_ref, lvf_ref, lvb_ref = refs[:10]
    idx = 10
    if has_s0:
        s0_ref = refs[idx]
        idx += 1
    o_ref = refs[idx]
    idx += 1
    if want_state:
        sfin_ref = refs[idx]
        idx += 1
    of_scr, ob_scr, stf, stb = refs[idx:idx + 4]
    n_groups = seq // HGRN_GROUP
    scale = B_DK ** -0.5
    n_levels = HGRN_GROUP.bit_length() - 1
    chains = [(hh, d) + spec for hh in range(n_h)
              for d, spec in enumerate(((kf_ref, lf_ref, lvf_ref, of_scr, stf, False),
                                        (kb_ref, lbw_ref, lvb_ref, ob_scr, stb, True)))]

    for hh, d, _, _, _, _, st, _ in chains:
        st[hh] = s0_ref[d, hh].T if has_s0 else jnp.zeros((B_DV, B_DK), F32)

    def nt(a, b):
        return lax.dot_general(a.astype(BF16), b.astype(BF16), NT_DIMS, preferred_element_type=F32)

    def group_step(i, carry):
        work = []
        for hh, d, k_ref, l_ref, lv_ref, o_scr, st, rev in chains:
            g = (n_groups - 1 - i) if rev else i
            rows = pl.ds(pl.multiple_of(g * HGRN_GROUP, HGRN_GROUP), HGRN_GROUP)
            cols = slice(hh * LANES, (hh + 1) * LANES)
            logf = l_ref[rows, cols] * LOG2E
            c = _scan_rows(logf, rev)
            work.append(dict(rows=rows, cols=cols, hh=hh, c=c, logf=logf, q=q_ref[rows, cols] * scale,
                             k=k_ref[rows, cols], v=v_ref[rows, cols].astype(BF16), lv=lv_ref, o=o_scr, st=st,
                             rev=rev))
        for w in work:
            w["qb"] = w["q"].astype(BF16)
            w["kb"] = w["k"].astype(BF16)
            w["attn"] = jnp.where(w["lv"][...] == 0, nt(w["qb"], w["kb"]), 0.0)
        odd = (lax.broadcasted_iota(jnp.int32, (HGRN_GROUP, LANES), 0) & 1) == 1
        for level in range(1, n_levels + 1):
            for w in work:
                if level == 1:
                    arg = jnp.where(odd, 0.0, w["logf"]) if w["rev"] else jnp.where(odd, w["logf"], 0.0)
                else:
                    arg = _neg_abs(w["c"] - _level_refs(w["c"], level, w["rev"]))
                e = jnp.exp2(arg).astype(BF16)
                w["attn"] = jnp.where(w["lv"][...] == level, nt(w["qb"] * e, w["kb"] * e), w["attn"])
        for w in work:
            c, st, hh = w["c"], w["st"], w["hh"]
            tot = c[0:1] if w["rev"] else c[HGRN_GROUP - 1:HGRN_GROUP]
            s_t = st[hh]
            w["o"][w["rows"], w["cols"]] = _dot(w["attn"].astype(BF16), w["v"]) + nt(w["q"] * jnp.exp2(c), s_t)
            kt = (w["k"] * jnp.exp2(tot - c)).astype(BF16)
            st[hh] = s_t * jnp.exp2(tot) + lax.dot_general(w["v"], kt, TN_DIMS, preferred_element_type=F32)
        return carry

    lax.fori_loop(0, n_groups, group_step, 0, unroll=True)
    for hh in range(n_h):
        cols = slice(hh * LANES, (hh + 1) * LANES)
        if want_state:
            sfin_ref[0, hh] = stf[hh].T
            sfin_ref[1, hh] = stb[hh].T
        o_ref[:, cols] = _rms(of_scr[:, cols] + ob_scr[:, cols], gn_ref[...]) * gb_ref[:, cols]


def _hgrn(q, kf, lf, kb, lbw, v, gb, gnorm, n_batch, s0=None, want_state=False, n_h=1):
    seq = q.shape[0] // n_batch
    tok = pl.BlockSpec((seq, n_h * LANES), lambda b, h: (b, h))
    const = pl.BlockSpec((HGRN_GROUP, HGRN_GROUP), lambda b, h: (0, 0))
    in_specs = [tok] * 7 + [pl.BlockSpec((1, LANES), lambda b, h: (0, 0))] + [const] * 2
    args = [q, kf, lf, kb, lbw, v, gb, gnorm.reshape(1, LANES)] + list(_hgrn_levels())
    if s0 is not None:
        state, layer = s0
        in_specs.append(pl.BlockSpec((None, None, 2, n_h, B_DK, B_DV), lambda b, h: (b, layer, 0, h, 0, 0)))
        args.append(state)
    out_specs = [tok]
    out_shape = [jax.ShapeDtypeStruct(q.shape, F32)]
    if want_state:
        out_specs.append(pl.BlockSpec((None, 2, n_h, B_DK, B_DV), lambda b, h: (b, 0, h, 0, 0)))
        out_shape.append(jax.ShapeDtypeStruct((n_batch, 2, B_HEADS, B_DK, B_DV), F32))
    res = pl.pallas_call(
        functools.partial(_hgrn_kernel, seq, n_h, s0 is not None, want_state),
        grid=(n_batch, B_HEADS // n_h),
        in_specs=in_specs,
        out_specs=out_specs,
        out_shape=out_shape,
        scratch_shapes=[pltpu.VMEM((seq, n_h * LANES), F32),
                        pltpu.VMEM((seq, n_h * LANES), F32),
                        pltpu.VMEM((n_h, B_DV, B_DK), F32),
                        pltpu.VMEM((n_h, B_DV, B_DK), F32)],
        compiler_params=_cparams(2),
        name="hgrn2",
    )(*args)
    return res if want_state else (res[0], None)


def _outffn_kernel(o1_ref, o2_ref, x_ref, mod_ref, wo_ref, ln_ref, wg_ref, wu_ref, wd_ref, out_ref):
    m = mod_ref[0]
    half = o1_ref.shape[1]
    y = (_dot(o1_ref[...].astype(BF16), wo_ref[0:half, :])
         + _dot(o2_ref[...].astype(BF16), wo_ref[half:2 * half, :]))
    x1 = _layer_norm(ALPHA * x_ref[...] + m[2:3] * y, ln_ref[0:1, :], ln_ref[1:2, :])
    h = (x1 * (1.0 + m[4:5]) + m[3:4]).astype(BF16)
    acc = None
    for a in range(0, D_FF, 512):
        b = min(a + 512, D_FF)
        act = (_silu(_dot(h, wg_ref[:, a:b])) * _dot(h, wu_ref[:, a:b])).astype(BF16)
        part = _dot(act, wd_ref[a:b, :])
        acc = part if acc is None else acc + part
    out_ref[...] = _layer_norm(ALPHA * x1 + m[5:6] * acc, ln_ref[2:3, :], ln_ref[3:4, :])


def _outffn(o1, o2, x, mod, wo_bf, ln, wg_bf, wu_bf, wd_bf, rows_per_cond, tm=512):
    m_rows = x.shape[0]
    half = o1.shape[1]
    consts = [wo_bf, ln, wg_bf, wu_bf, wd_bf]
    in_specs = [pl.BlockSpec((tm, half), lambda i: (i, 0)),
                pl.BlockSpec((tm, half), lambda i: (i, 0)),
                pl.BlockSpec((tm, D_MODEL), lambda i: (i, 0)),
                pl.BlockSpec((1, 6, D_MODEL), lambda i: (i * tm // rows_per_cond, 0, 0))]
    in_specs += [pl.BlockSpec(a.shape, lambda i: (0, 0)) for a in consts]
    return pl.pallas_call(
        _outffn_kernel,
        grid=(m_rows // tm,),
        in_specs=in_specs,
        out_specs=pl.BlockSpec((tm, D_MODEL), lambda i: (i, 0)),
        out_shape=jax.ShapeDtypeStruct((m_rows, D_MODEL), F32),
        compiler_params=_cparams(1),
        name="outproj_ffn",
    )(o1, o2, x, mod, *consts)


def _cache_spec(past, width, layer):
    return pl.BlockSpec((None, None, past, width), lambda b, i: (b, layer, 0, 0))


def _even_layer(xp, xs, modp, mods, w_in_bf, lb, sink, gnorm, tail_w, rope_hd,
                cache_k, cache_v, state, e, nbp, nbs):
    seq_p = xp.shape[0] // nbp
    seq_s = xs.shape[0] // nbs
    past = cache_k.shape[2]
    pp = _inproj_even(xp, modp, w_in_bf, lb, None, xp.shape[0])
    ps = _inproj_even(xs, mods, w_in_bf, lb, rope_hd, seq_s)
    qa, ka, va, qb, ib, kf, lf, kb, lbw, gb = pp
    oa_p = _attention(qa, ka, va, nbp, A_HEADS, A_HEADS // A_KV, seq_p, sink=sink, n_sub=PROMPT_SUB_A)
    ob_p, s_new = _hgrn(qb, kf, lf, kb, lbw, ib, gb, gnorm, nbp, want_state=True, n_h=PROMPT_HGRN_HEADS)
    xp = _outffn(oa_p, ob_p, xp, modp, *tail_w, xp.shape[0])
    new = (ka.reshape(nbp, seq_p, A_KV, HEAD_DIM), va.reshape(nbp, seq_p, A_KV, HEAD_DIM), s_new)
    qa, ka, va, qb, ib, kf, lf, kb, lbw, gb = ps
    spec = _cache_spec(past, A_KV * HEAD_DIM, e)
    oa_s = _attention(qa, ka, va, nbs, A_HEADS, A_HEADS // A_KV, BAND_TQ, band=True,
                      ctx=(cache_k, cache_v, spec, spec, past), sink=sink, n_qt=BAND_QT)
    ob_s, _ = _hgrn(qb, kf, lf, kb, lbw, ib, gb, gnorm, nbs, s0=(state, e), n_h=SAMPLE_HGRN_HEADS)
    xs = _outffn(oa_s, ob_s, xs, mods, *tail_w, seq_s)
    return xp, xs, new


def _odd_layer(xp, xs, modp, mods, in_w, tail_w, rope_hd, rope_c,
               cache_ckv, cache_pe, cache_k, cache_v, o, nbp, nbs):
    seq_p = xp.shape[0] // nbp
    seq_s = xs.shape[0] // nbs
    past = cache_ckv.shape[2]
    wkv_bf = in_w[4]
    mla_mul = (C_NOPE + C_ROPE) ** -0.5 * LOG2E
    qc, ckv, kc, vc, pe, qd, kd, vd = _inproj_odd(xp, modp, *in_w, None, xp.shape[0])
    oc_p = _attention(qc, kc, vc, nbp, C_HEADS, 1, seq_p, wide=True, score_mul=mla_mul)
    od_p = _attention(qd, kd, vd, nbp, D_HEADS, D_HEADS // D_KV, seq_p, n_sub=PROMPT_SUB_D)
    xp = _outffn(oc_p, od_p, xp, modp, *tail_w, xp.shape[0])
    new = (ckv.reshape(nbp, seq_p, C_KV_LORA), pe[:, C_NOPE:C_NOPE + C_ROPE].reshape(nbp, seq_p, C_ROPE),
           kd.reshape(nbp, seq_p, D_KV, HEAD_DIM), vd.reshape(nbp, seq_p, D_KV, HEAD_DIM))
    qc, ckv, kc, vc, pe, qd, kd, vd = _inproj_odd(xs, mods, *in_w, rope_hd + rope_c, seq_s)
    pe_ctx = jnp.pad(cache_pe[:, o].reshape(nbs * past, C_ROPE), ((0, 0), (C_NOPE, LANES - C_NOPE - C_ROPE)))
    kcc, vcc = _kvup(cache_ckv[:, o].reshape(nbs * past, C_KV_LORA), pe_ctx, wkv_bf)
    oc_s = _attention(qc, kc, vc, nbs, C_HEADS, 1, SAMPLE_TQ, wide=True, score_mul=mla_mul,
                      ctx=(kcc, vcc, pl.BlockSpec((past, C_HEADS * LANES), lambda b, i: (b, 0)),
                           pl.BlockSpec((past, C_HEADS * C_V), lambda b, i: (b, 0)), past), n_qt=SAMPLE_QT)
    spec = _cache_spec(past, D_KV * HEAD_DIM, o)
    od_s = _attention(qd, kd, vd, nbs, D_HEADS, D_HEADS // D_KV, SAMPLE_TQ, ctx=(cache_k, cache_v, spec, spec, past),
                      n_qt=SAMPLE_QT)
    xs = _outffn(oc_s, od_s, xs, mods, *tail_w, seq_s)
    return xp, xs, new


def kernel(x_prompt, x_sample, cache_a_k, cache_a_v, state_b, cache_c_kv, cache_c_pe, cache_d_k, cache_d_v, c, c_ctx, w_ada, b_ada, ln_g, ln_b, w_in_ab, a_sink, b_lb, b_gnorm, w_in_cd, c_q_norm, c_kv_norm, c_w_q_up, c_w_kv_up, d_q_norm, d_k_norm, w_out, w_ffn_gate, w_ffn_up, w_ffn_down):
    nbp, seq_p, _ = x_prompt.shape
    nbs, seq_s, _ = x_sample.shape
    past = cache_a_k.shape[2]
    xp = x_prompt.reshape(nbp * seq_p, D_MODEL)
    xs = x_sample.reshape(nbs * seq_s, D_MODEL)
    rope_hd = _rope_tables(seq_s, HEAD_DIM)
    cos_c, sin_c = _rope_tables(seq_s, C_ROPE)
    ones = jnp.ones((seq_s, C_NOPE), F32)
    rope_c = (jnp.concatenate([ones, cos_c[:, :C_NOPE]], axis=1),
              jnp.concatenate([0.0 * ones, sin_c[:, :C_ROPE], 0.0 * ones[:, :LANES - C_NOPE - C_ROPE]], axis=1))
    lb_w = jax.nn.softmax(b_lb.astype(F32), axis=0)
    lb_all = jnp.cumsum(lb_w, axis=0) - lb_w[:1]
    mod_all = _modulation(c, c_ctx, w_ada, b_ada)
    n_even = cache_a_k.shape[1]
    n_odd = cache_c_kv.shape[1]
    ca_k = cache_a_k.reshape(nbs, n_even, past, A_KV * HEAD_DIM)
    ca_v = cache_a_v.reshape(nbs, n_even, past, A_KV * HEAD_DIM)
    cd_k = cache_d_k.reshape(nbs, n_odd, past, D_KV * HEAD_DIM)
    cd_v = cache_d_v.reshape(nbs, n_odd, past, D_KV * HEAD_DIM)
    ak, av, sb, ckv, cpe, dk, dv = [], [], [], [], [], [], []
    for l in range(DEPTH):
        modp = mod_all[l, 0:1]
        mods = mod_all[l, 1:1 + nbs]
        ln = jnp.concatenate([ln_g[l, 0:1], ln_b[l, 0:1], ln_g[l, 1:2], ln_b[l, 1:2]], axis=0)
        wo, wg, wu, wd = _to_bf16([w_out, w_ffn_gate, w_ffn_up, w_ffn_down], l)
        tail_w = (wo, ln, wg, wu, wd)
        if l % 2 == 0:
            e = l // 2
            xp, xs, (k_new, v_new, s_new) = _even_layer(
                xp, xs, modp, mods, _to_bf16([w_in_ab], e)[0], lb_all[e], a_sink[e], b_gnorm[e], tail_w,
                rope_hd, ca_k, ca_v, state_b, e, nbp, nbs)
            ak.append(k_new)
            av.append(v_new)
            sb.append(s_new)
        else:
            o = l // 2
            w_in = w_in_cd[o]
            lat = C_Q_LORA + C_KV_LORA
            zeros = lambda n: jnp.zeros((D_MODEL, n), F32)
            w_in = jnp.concatenate([w_in[:, :lat], zeros(C_NOPE), w_in[:, lat:lat + C_ROPE],
                                    zeros(LANES - C_NOPE - C_ROPE), w_in[:, lat + C_ROPE:]], axis=1).astype(BF16)
            wq = c_w_q_up[o].reshape(C_Q_LORA, C_HEADS, C_NOPE + C_ROPE)
            wq = jnp.pad(wq, ((0, 0), (0, 0), (0, LANES - C_NOPE - C_ROPE))).reshape(C_Q_LORA, -1).astype(BF16)
            wkv = c_w_kv_up[o].reshape(C_KV_LORA, C_HEADS, C_NOPE + C_V)
            wk = jnp.pad(wkv[:, :, :C_NOPE], ((0, 0), (0, 0), (0, LANES - C_NOPE))).reshape(C_KV_LORA, -1)
            wkv = jnp.concatenate([wk, wkv[:, :, C_NOPE:].reshape(C_KV_LORA, -1)], axis=1).astype(BF16)
            in_w = (w_in, c_q_norm[o].reshape(1, -1), c_kv_norm[o].reshape(1, -1), wq, wkv,
                    jnp.tile(d_q_norm[o], D_HEADS).reshape(1, -1), jnp.tile(d_k_norm[o], D_KV).reshape(1, -1))
            xp, xs, (c_new, pe_new, k_new, v_new) = _odd_layer(
                xp, xs, modp, mods, in_w, tail_w, rope_hd, rope_c, cache_c_kv, cache_c_pe, cd_k, cd_v, o, nbp, nbs)
            ckv.append(c_new)
            cpe.append(pe_new)
            dk.append(k_new)
            dv.append(v_new)
    return (xp.reshape(nbp, seq_p, D_MODEL), xs.reshape(nbs, seq_s, D_MODEL),
            jnp.stack(ak, axis=1), jnp.stack(av, axis=1), jnp.stack(sb, axis=1),
            jnp.stack(ckv, axis=1), jnp.stack(cpe, axis=1), jnp.stack(dk, axis=1), jnp.stack(dv, axis=1))
```

```python
import functools

import numpy as np
import jax
import jax.numpy as jnp
from jax import lax
from jax.experimental import pallas as pl
from jax.experimental.pallas import tpu as pltpu

F32 = jnp.float32
BF16 = jnp.bfloat16

D_MODEL = 1024
DEPTH = 4
GRID_W = 64
HEAD_DIM = 64
ROPE_BASE = 10000.0
MASK_VALUE = -1e30
F_FLOOR = 1e-30
A_HEADS = 8
A_KV = 2
WINDOW = 128
B_HEADS = 4
B_DK = 128
B_DV = 128
C_HEADS = 8
C_Q_LORA = 384
C_KV_LORA = 256
C_NOPE = 64
C_ROPE = 32
C_V = 64
D_HEADS = 8
D_KV = 4
D_FF = 2816
ALPHA = (2 * DEPTH) ** 0.25

LANES = 128
HGRN_GROUP = 128
LOG2E = 1.4426950408889634
PROMPT_HGRN_HEADS = 4
SAMPLE_HGRN_HEADS = 2
PROMPT_SUB_A = 4
PROMPT_SUB_D = 2
SAMPLE_TQ = 512
SAMPLE_QT = 2
BAND_TQ = 256
BAND_QT = 4
VMEM_LIMIT = 56 * 1024 * 1024
NT_DIMS = (((1,), (1,)), ((), ()))
TN_DIMS = (((0,), (0,)), ((), ()))


def _cparams(n_axes):
    return pltpu.CompilerParams(dimension_semantics=("arbitrary",) * n_axes,
                                vmem_limit_bytes=VMEM_LIMIT)


def _silu(x):
    return x * jax.nn.sigmoid(x)


def _dot(a, b):
    return jnp.dot(a, b, preferred_element_type=F32)


def _layer_norm(x, g, b):
    mu = jnp.mean(x, axis=-1, keepdims=True)
    xc = x - mu
    var = jnp.mean(xc * xc, axis=-1, keepdims=True)
    return xc * lax.rsqrt(var + 1e-5) * g + b


def _rms(x, g):
    return x * lax.rsqrt(jnp.mean(x * x, axis=-1, keepdims=True) + 1e-6) * g


def _rms_heads64(x, g):
    outs = []
    for j in range(x.shape[1] // LANES):
        xg = x[:, j * LANES:(j + 1) * LANES]
        x2 = xg * xg
        lo = lax.broadcasted_iota(jnp.int32, xg.shape, 1) < HEAD_DIM
        s_lo = jnp.sum(jnp.where(lo, x2, 0.0), axis=-1, keepdims=True)
        s_hi = jnp.sum(jnp.where(lo, 0.0, x2), axis=-1, keepdims=True)
        ms = jnp.where(lo, s_lo, s_hi) * (1.0 / HEAD_DIM)
        outs.append(xg * lax.rsqrt(ms + 1e-6))
    return jnp.concatenate(outs, axis=1) * g


def _rope(x, cos, sin, half):
    outs = []
    for j in range(x.shape[1] // LANES):
        xg = x[:, j * LANES:(j + 1) * LANES]
        lane = lax.broadcasted_iota(jnp.int32, xg.shape, 1)
        up = pltpu.roll(xg, LANES - half, 1)
        down = pltpu.roll(xg, half, 1)
        rot = jnp.where((lane & (2 * half - 1)) < half, up, down)
        outs.append(xg * cos + rot * sin)
    return outs[0] if len(outs) == 1 else jnp.concatenate(outs, axis=1)


def _rope_tables(n_tokens, rot_dim):
    n_rows = n_tokens // GRID_W
    row = jnp.broadcast_to(jnp.arange(n_rows, dtype=F32)[:, None], (n_rows, GRID_W)).reshape(-1)
    col = jnp.broadcast_to(jnp.arange(GRID_W, dtype=F32)[None, :], (n_rows, GRID_W)).reshape(-1)
    quarter = rot_dim // 4
    inv = ROPE_BASE ** (-jnp.arange(quarter, dtype=F32) / quarter)
    ar = row[:, None] * inv
    ac = col[:, None] * inv
    cos = jnp.concatenate([jnp.cos(ar), jnp.cos(ar), jnp.cos(ac), jnp.cos(ac)], axis=-1)
    sin = jnp.concatenate([-jnp.sin(ar), jnp.sin(ar), -jnp.sin(ac), jnp.sin(ac)], axis=-1)
    reps = LANES // rot_dim
    return jnp.tile(cos, (1, reps)), jnp.tile(sin, (1, reps))


def _cast_kernel(*refs):
    n = len(refs) // 2
    for x_ref, o_ref in zip(refs[:n], refs[n:]):
        o_ref[...] = x_ref[...].astype(BF16)


def _to_bf16(ws, layer):
    n_blocks = 4
    in_specs, out_specs, out_shape = [], [], []
    for w in ws:
        _, rows, cols = w.shape
        in_specs.append(pl.BlockSpec((None, rows // n_blocks, cols), lambda i: (layer, i, 0)))
        out_specs.append(pl.BlockSpec((rows // n_blocks, cols), lambda i: (i, 0)))
        out_shape.append(jax.ShapeDtypeStruct((rows, cols), BF16))
    return pl.pallas_call(
        _cast_kernel,
        grid=(n_blocks,),
        in_specs=in_specs,
        out_specs=out_specs,
        out_shape=out_shape,
        compiler_params=_cparams(1),
        name="to_bf16",
    )(*ws)


def _mod_kernel(c_ref, w_ref, b_ref, o_ref):
    a = _silu(c_ref[...]).astype(BF16)
    o_ref[0] = _dot(a, w_ref[0].astype(BF16)) + b_ref[0]


def _modulation(c, c_ctx, w_ada, b_ada):
    nb = c.shape[0]
    cond = jnp.zeros((16, D_MODEL), F32).at[0].set(c_ctx).at[1:1 + nb].set(c)
    tn = 1536
    out = pl.pallas_call(
        _mod_kernel,
        grid=(DEPTH, 6 * D_MODEL // tn),
        in_specs=[pl.BlockSpec((16, D_MODEL), lambda l, j: (0, 0)),
                  pl.BlockSpec((1, D_MODEL, tn), lambda l, j: (l, 0, j)),
                  pl.BlockSpec((1, 1, tn), lambda l, j: (l, 0, j))],
        out_specs=pl.BlockSpec((1, 16, tn), lambda l, j: (l, 0, j)),
        out_shape=jax.ShapeDtypeStruct((DEPTH, 16, 6 * D_MODEL), F32),
        compiler_params=_cparams(2),
        name="modulation",
    )(cond, w_ada, b_ada.reshape(DEPTH, 1, 6 * D_MODEL))
    return out.reshape(DEPTH, 16, 6, D_MODEL)


def _modulate(x_ref, mod_ref, shift_row):
    m = mod_ref[0]
    return (x_ref[...] * (1.0 + m[shift_row + 1:shift_row + 2]) + m[shift_row:shift_row + 1]).astype(BF16)


def _inproj_even_kernel(rope, *refs):
    if rope:
        x_ref, mod_ref, w_ref, lb_ref, cos_ref, sin_ref = refs[:6]
        outs = refs[6:]
    else:
        x_ref, mod_ref, w_ref, lb_ref = refs[:4]
        outs = refs[4:]
    qa_ref, ka_ref, va_ref, qb_ref, ib_ref, kf_ref, lf_ref, kb_ref, lbw_ref, gb_ref = outs
    h = _modulate(x_ref, mod_ref, 0)

    def proj(a, b):
        return _dot(h, w_ref[:, a:b])

    for d, (k_ref, l_ref) in enumerate(((kf_ref, lf_ref), (kb_ref, lbw_ref))):
        lb = lb_ref[d:d + 1, :]
        f = lb + (1.0 - lb) * jax.nn.sigmoid(proj(1792 + 512 * d, 2304 + 512 * d))
        l_ref[...] = jnp.log(jnp.maximum(f, F_FLOOR))
        k_ref[...] = 1.0 - f
    qb_ref[...] = _silu(proj(768, 1280))
    gb_ref[...] = _silu(proj(2816, 3328))
    qa = proj(0, 512)
    ka = proj(512, 640)
    if rope:
        cos = cos_ref[...]
        sin = sin_ref[...]
        qa = _rope(qa, cos, sin, HEAD_DIM // 4)
        ka = _rope(ka, cos, sin, HEAD_DIM // 4)
    qa_ref[...] = qa.astype(BF16)
    ka_ref[...] = ka
    va_ref[...] = proj(640, 768)
    ib_ref[...] = proj(1280, 1792).astype(BF16)


def _inproj_even(x, mod, w_bf, lb, rope_tabs, rows_per_cond, tm=256):
    m_rows = x.shape[0]
    rope = rope_tabs is not None
    widths = (512, 128, 128, 512, 512, 512, 512, 512, 512, 512)
    dtypes = (BF16, F32, F32, F32, BF16, F32, F32, F32, F32, F32)
    in_specs = [pl.BlockSpec((tm, D_MODEL), lambda i: (i, 0)),
                pl.BlockSpec((1, 6, D_MODEL), lambda i: (i * tm // rows_per_cond, 0, 0)),
                pl.BlockSpec(w_bf.shape, lambda i: (0, 0)),
                pl.BlockSpec(lb.shape, lambda i: (0, 0))]
    args = [x, mod, w_bf, lb]
    if rope:
        nblk = rope_tabs[0].shape[0] // tm
        in_specs += [pl.BlockSpec((tm, LANES), lambda i: (i % nblk, 0))] * 2
        args += list(rope_tabs)
    return pl.pallas_call(
        functools.partial(_inproj_even_kernel, rope),
        grid=(m_rows // tm,),
        in_specs=in_specs,
        out_specs=[pl.BlockSpec((tm, w), lambda i: (i, 0)) for w in widths],
        out_shape=[jax.ShapeDtypeStruct((m_rows, w), dt) for w, dt in zip(widths, dtypes)],
        compiler_params=_cparams(1),
        name="inproj_even",
    )(*args)


def _inproj_odd_kernel(rope, *refs):
    if rope:
        (x_ref, mod_ref, w_ref, gcq_ref, gckv_ref, wq_ref, wkv_ref, gdq_ref, gdk_ref,
         cosd_ref, sind_ref, cosc_ref, sinc_ref) = refs[:13]
        outs = refs[13:]
    else:
        x_ref, mod_ref, w_ref, gcq_ref, gckv_ref, wq_ref, wkv_ref, gdq_ref, gdk_ref = refs[:9]
        outs = refs[9:]
    qc_ref, ckv_ref, kc_ref, vc_ref, pe_ref, qd_ref, kd_ref, vd_ref = outs
    h = _modulate(x_ref, mod_ref, 0)

    def proj(a, b):
        return _dot(h, w_ref[:, a:b])

    qd = _rms_heads64(proj(768, 1280), gdq_ref[...])
    kd = _rms_heads64(proj(1280, 1536), gdk_ref[...])
    if rope:
        qd = _rope(qd, cosd_ref[...], sind_ref[...], HEAD_DIM // 4)
        kd = _rope(kd, cosd_ref[...], sind_ref[...], HEAD_DIM // 4)
    qd_ref[...] = qd.astype(BF16)
    kd_ref[...] = kd
    cq = _rms(proj(0, 384), gcq_ref[...]).astype(BF16)
    qc = _dot(cq, wq_ref[...])
    pe = proj(640, 768)
    if rope:
        qc = _rope(qc, cosc_ref[...], sinc_ref[...], C_ROPE // 4)
        pe = _rope(pe, cosc_ref[...], sinc_ref[...], C_ROPE // 4)
    qc_ref[...] = qc.astype(BF16)
    pe_ref[...] = pe
    ckv = _rms(proj(384, 640), gckv_ref[...])
    ckv_ref[...] = ckv
    ckv_bf = ckv.astype(BF16)
    kn = _dot(ckv_bf, wkv_ref[:, 0:C_HEADS * LANES])
    kc_ref[...] = jnp.concatenate([kn[:, j * LANES:(j + 1) * LANES] + pe for j in range(C_HEADS)],
                                  axis=1).astype(BF16)
    vc_ref[...] = _dot(ckv_bf, wkv_ref[:, C_HEADS * LANES:]).astype(BF16)
    vd_ref[...] = proj(1536, 1792)


def _inproj_odd(x, mod, w_bf, gcq, gckv, wq_bf, wkv_bf, gdq, gdk, rope_tabs, rows_per_cond, tm=512):
    m_rows = x.shape[0]
    rope = rope_tabs is not None
    outs = ((C_HEADS * LANES, BF16), (C_KV_LORA, F32), (C_HEADS * LANES, BF16), (C_HEADS * C_V, BF16),
            (LANES, F32), (D_HEADS * HEAD_DIM, BF16), (D_KV * HEAD_DIM, F32), (D_KV * HEAD_DIM, F32))
    consts = [w_bf, gcq, gckv, wq_bf, wkv_bf, gdq, gdk]
    in_specs = [pl.BlockSpec((tm, D_MODEL), lambda i: (i, 0)),
                pl.BlockSpec((1, 6, D_MODEL), lambda i: (i * tm // rows_per_cond, 0, 0))]
    in_specs += [pl.BlockSpec(a.shape, lambda i: (0, 0)) for a in consts]
    args = [x, mod] + consts
    if rope:
        nblk = rope_tabs[0].shape[0] // tm
        in_specs += [pl.BlockSpec((tm, LANES), lambda i: (i % nblk, 0))] * 4
        args += list(rope_tabs)
    return pl.pallas_call(
        functools.partial(_inproj_odd_kernel, rope),
        grid=(m_rows // tm,),
        in_specs=in_specs,
        out_specs=[pl.BlockSpec((tm, w), lambda i: (i, 0)) for w, _ in outs],
        out_shape=[jax.ShapeDtypeStruct((m_rows, w), dt) for w, dt in outs],
        compiler_params=_cparams(1),
        name="inproj_odd",
    )(*args)


def _kvup_kernel(x_ref, pe_ref, w_ref, kc_ref, vc_ref):
    x = x_ref[...].astype(BF16)
    kn = _dot(x, w_ref[:, 0:C_HEADS * LANES])
    pe = pe_ref[...]
    kc_ref[...] = jnp.concatenate([kn[:, j * LANES:(j + 1) * LANES] + pe for j in range(C_HEADS)],
                                  axis=1).astype(BF16)
    vc_ref[...] = _dot(x, w_ref[:, C_HEADS * LANES:]).astype(BF16)


def _kvup(lat, pe_slab, wkv_bf, tm=512):
    m_rows = lat.shape[0]
    return pl.pallas_call(
        _kvup_kernel,
        grid=(m_rows // tm,),
        in_specs=[pl.BlockSpec((tm, C_KV_LORA), lambda i: (i, 0)),
                  pl.BlockSpec((tm, LANES), lambda i: (i, 0)),
                  pl.BlockSpec(wkv_bf.shape, lambda i: (0, 0))],
        out_specs=[pl.BlockSpec((tm, C_HEADS * LANES), lambda i: (i, 0)),
                   pl.BlockSpec((tm, C_HEADS * C_V), lambda i: (i, 0))],
        out_shape=[jax.ShapeDtypeStruct((m_rows, C_HEADS * LANES), BF16),
                   jax.ShapeDtypeStruct((m_rows, C_HEADS * C_V), BF16)],
        compiler_params=_cparams(1),
        name="kv_up_ctx",
    )(lat, pe_slab, wkv_bf)


def _attn_kernel(tq, n_heads, group, seq_k, past, band, has_sink, wide, score_mul, n_sub, n_qt, *refs):
    q_ref, k_ref, v_ref = refs[:3]
    idx = 3
    if past:
        kc_ref, vc_ref = refs[idx:idx + 2]
        idx += 2
    if has_sink:
        sink_ref = refs[idx]
        idx += 1
    o_ref, kpad, vaug = refs[idx:idx + 3]
    d = HEAD_DIM
    n_kv = n_heads // group
    n_slots = n_heads if wide else 2 * n_kv
    total = seq_k + past
    qi = pl.program_id(1)

    def _prepare_keys_values():
        for sub in range(n_sub):
            parts = [(k_ref, v_ref, sub * seq_k, 0, seq_k)]
            if past:
                parts.append((kc_ref, vc_ref, 0, seq_k, past))
            base = sub * n_slots
            for ks_ref, vs_ref, src0, r0, n in parts:
                src = slice(src0, src0 + n)
                rows = slice(r0, r0 + n)
                lo = lax.broadcasted_iota(jnp.int32, (n, LANES), 1) < d
                for j in range(n_kv * d // LANES):
                    cols = slice(j * LANES, (j + 1) * LANES)
                    vs = vs_ref[src, cols].astype(F32)
                    if wide:
                        vaug[base + 2 * j, rows, :] = jnp.where(lo, vs, 1.0).astype(BF16)
                        vaug[base + 2 * j + 1, rows, :] = jnp.where(lo, 1.0, vs).astype(BF16)
                        continue
                    vr = pltpu.roll(vs, d, 1)
                    ks = ks_ref[src, cols]
                    kr = pltpu.roll(ks, d, 1)
                    for g, (k_even, k_odd, v_even, v_odd) in ((2 * j, (ks, kr, vs, vr)), (2 * j + 1, (kr, ks, vr, vs))):
                        kpad[base + 2 * g, rows, :] = jnp.where(lo, k_even, 0.0).astype(BF16)
                        kpad[base + 2 * g + 1, rows, :] = jnp.where(lo, 0.0, k_odd).astype(BF16)
                        vaug[base + 2 * g, rows, :] = jnp.where(lo, v_even, 1.0).astype(BF16)
                        vaug[base + 2 * g + 1, rows, :] = jnp.where(lo, 1.0, v_odd).astype(BF16)
                if wide:
                    for h in range(n_heads):
                        kpad[base + h, rows, :] = ks_ref[src, h * LANES:(h + 1) * LANES]

    if seq_k == tq * n_qt:
        _prepare_keys_values()
    else:
        pl.when(qi == 0)(_prepare_keys_values)

    def key_window(tile):
        if not band:
            return [slice(0, total)], None
        kb = tq + 2 * WINDOW
        start = pl.multiple_of(jnp.clip(tile * tq - WINDOW, 0, seq_k - kb), LANES)
        diff = ((tile * tq - start) + lax.broadcasted_iota(jnp.int32, (tq, kb), 0)
                - lax.broadcasted_iota(jnp.int32, (tq, kb), 1))
        return [pl.ds(start, kb), pl.ds(seq_k, past)], jnp.abs(diff) <= WINDOW

    def nt(a, b):
        return lax.dot_general(a, b, NT_DIMS, preferred_element_type=F32)

    lo_out = lax.broadcasted_iota(jnp.int32, (tq, LANES), 1) < d
    for unit in range(n_sub * n_qt):
        base = (unit // n_qt) * n_slots
        q_rows = slice(unit * tq, (unit + 1) * tq)
        key_rows, bmask = key_window(qi * n_qt + unit % n_qt)
        q_slabs = {}
        results = {}
        for g in range(n_kv):
            for parity in (0, 1):
                heads = [h for h in range(g * group, (g + 1) * group) if h % 2 == parity]
                if not heads:
                    continue
                probs = [[] for _ in key_rows]
                maxes = []
                for h in heads:
                    slot = base + (h if wide else 2 * g + parity)
                    qs = h if wide else h // 2
                    if qs not in q_slabs:
                        q = q_ref[q_rows, qs * LANES:(qs + 1) * LANES]
                        q_slabs[qs] = q if wide else (q * d ** -0.5).astype(BF16)
                    scores = [nt(q_slabs[qs], kpad[slot, r, :]) for r in key_rows]
                    if score_mul is not None:
                        scores = [s * score_mul for s in scores]
                    if band:
                        scores[0] = jnp.where(bmask, scores[0], MASK_VALUE)
                    m = jnp.max(scores[0], axis=-1, keepdims=True)
                    for s in scores[1:]:
                        m = jnp.maximum(m, jnp.max(s, axis=-1, keepdims=True))
                    if has_sink:
                        m = jnp.maximum(m, sink_ref[h])
                    maxes.append(m)
                    for part, s in zip(probs, scores):
                        e = jnp.exp(s - m) if score_mul is None else jnp.exp2(s - m)
                        part.append(e.astype(BF16))
                vslot = base + (heads[0] if wide else 2 * g + parity)
                o_aug = None
                for part, r in zip(probs, key_rows):
                    p = part[0] if len(part) == 1 else jnp.concatenate(part, axis=0)
                    o = _dot(p, vaug[vslot, r, :])
                    o_aug = o if o_aug is None else o_aug + o
                den = pltpu.roll(o_aug, d, 1)
                for i, h in enumerate(heads):
                    rows = slice(i * tq, (i + 1) * tq)
                    den_h = den[rows]
                    if has_sink:
                        den_h = den_h + jnp.exp(sink_ref[h] - maxes[i])
                    results[h] = o_aug[rows] / den_h
        for j in range(n_heads // 2):
            o_ref[q_rows, j * LANES:(j + 1) * LANES] = jnp.where(lo_out, results[2 * j], results[2 * j + 1])


def _attention(q, k, v, n_batch, n_heads, group, tq, *, band=False, ctx=None, sink=None, wide=False,
               score_mul=None, n_sub=1, n_qt=1):
    seq = q.shape[0] // n_batch
    nq = seq // (tq * n_qt)
    n_kv = n_heads // group
    assert n_sub == 1 or (n_qt == 1 and nq == 1 and ctx is None and n_batch % n_sub == 0)
    in_specs = [pl.BlockSpec((n_sub * n_qt * tq, q.shape[1]), lambda b, i: (b * nq + i, 0)),
                pl.BlockSpec((n_sub * seq, k.shape[1]), lambda b, i: (b, 0)),
                pl.BlockSpec((n_sub * seq, v.shape[1]), lambda b, i: (b, 0))]
    args = [q, k, v]
    past = 0
    if ctx is not None:
        kc, vc, kc_spec, vc_spec, past = ctx
        in_specs += [kc_spec, vc_spec]
        args += [kc, vc]
    if sink is not None:
        in_specs.append(pl.BlockSpec(memory_space=pltpu.SMEM))
        args.append(sink)
    n_slots = n_sub * (n_heads if wide else 2 * n_kv)
    return pl.pallas_call(
        functools.partial(_attn_kernel, tq, n_heads, group, seq, past, band, sink is not None, wide, score_mul,
                          n_sub, n_qt),
        grid=(n_batch // n_sub, nq),
        in_specs=in_specs,
        out_specs=pl.BlockSpec((n_sub * n_qt * tq, n_heads * HEAD_DIM), lambda b, i: (b * nq + i, 0)),
        out_shape=jax.ShapeDtypeStruct((q.shape[0], n_heads * HEAD_DIM), F32),
        scratch_shapes=[pltpu.VMEM((n_slots, seq + past, LANES), BF16),
                        pltpu.VMEM((n_slots, seq + past, LANES), BF16)],
        compiler_params=_cparams(2),
        name="mla_attention" if wide else "gqa_attention",
    )(*args)


def _hgrn_levels():
    i = np.arange(HGRN_GROUP)
    t, s = i[:, None], i[None, :]
    level = np.where(t != s, np.floor(np.log2(np.maximum(t ^ s, 1))) + 1, 0).astype(np.int32)
    return (jnp.asarray(np.where(s <= t, level, -1), jnp.int32),
            jnp.asarray(np.where(s >= t, level, -1), jnp.int32))


def _scan_rows(x, rev):
    n = x.shape[0]
    row = lax.broadcasted_iota(jnp.int32, x.shape, 0)
    step = 1
    while step < n:
        if step < 8:
            if rev:
                x = x + jnp.where(row < n - step, pltpu.roll(x, n - step, 0), 0.0)
            else:
                x = x + jnp.where(row >= step, pltpu.roll(x, step, 0), 0.0)
        elif rev:
            x = jnp.concatenate([x[:n - step] + x[step:], x[n - step:]], axis=0)
        else:
            x = jnp.concatenate([x[:step], x[step:] + x[:n - step]], axis=0)
        step *= 2
    return x


def _level_refs(c, level, rev):
    n = c.shape[0]
    half = 1 << (level - 1)
    bs = 2 * half
    off = half if rev else half - 1
    if bs >= 16:
        return jnp.concatenate([jnp.broadcast_to(c[i * bs + off:i * bs + off + 1, :], (bs, LANES))
                                for i in range(n // bs)], axis=0)
    c3 = c.reshape(n // 8, 8, LANES)
    sub = lax.broadcasted_iota(jnp.int32, c3.shape, 1)
    out = None
    for j in reversed(range(8 // bs)):
        b = jnp.broadcast_to(c3[:, j * bs + off:j * bs + off + 1, :], c3.shape)
        out = b if out is None else jnp.where(sub < (j + 1) * bs, b, out)
    return out.reshape(n, LANES)


def _neg_abs(x):
    return lax.bitcast_convert_type(lax.bitcast_convert_type(x, jnp.uint32) | jnp.uint32(0x80000000), F32)


def _hgrn_kernel(seq, n_h, has_s0, want_state, *refs):
    q_ref, kf_ref, lf_ref, kb_ref, lbw_ref, v_ref, gb_ref, gn_ref, lvf_ref, lvb_ref = refs[:10]
    idx = 10
    if has_s0:
        s0_ref = refs[idx]
        idx += 1
    if want_state:
        idx += 1
    o_ref = refs[idx]
    idx += 1
    if want_state:
        sfin_ref = refs[idx]
        idx += 1
    of_scr, ob_scr, stf, stb = refs[idx:idx + 4]
    n_groups = seq // HGRN_GROUP
    scale = B_DK ** -0.5
    n_levels = HGRN_GROUP.bit_length() - 1
    chains = [(hh, d) + spec for hh in range(n_h)
              for d, spec in enumerate(((kf_ref, lf_ref, lvf_ref, of_scr, stf, False),
                                        (kb_ref, lbw_ref, lvb_ref, ob_scr, stb, True)))]

    for hh, d, _, _, _, _, st, _ in chains:
        st[hh] = s0_ref[d, hh].T if has_s0 else jnp.zeros((B_DV, B_DK), F32)

    def nt(a, b):
        return lax.dot_general(a.astype(BF16), b.astype(BF16), NT_DIMS, preferred_element_type=F32)

    def group_step(i, carry):
        work = []
        for hh, d, k_ref, l_ref, lv_ref, o_scr, st, rev in chains:
            g = (n_groups - 1 - i) if rev else i
            rows = pl.ds(pl.multiple_of(g * HGRN_GROUP, HGRN_GROUP), HGRN_GROUP)
            cols = slice(hh * LANES, (hh + 1) * LANES)
            logf = l_ref[rows, cols] * LOG2E
            c = _scan_rows(logf, rev)
            work.append(dict(rows=rows, cols=cols, hh=hh, c=c, logf=logf, q=q_ref[rows, cols] * scale,
                             k=k_ref[rows, cols], v=v_ref[rows, cols].astype(BF16), lv=lv_ref, o=o_scr, st=st,
                             rev=rev))
        for w in work:
            w["qb"] = w["q"].astype(BF16)
            w["kb"] = w["k"].astype(BF16)
            w["attn"] = jnp.where(w["lv"][...] == 0, nt(w["qb"], w["kb"]), 0.0)
        odd = (lax.broadcasted_iota(jnp.int32, (HGRN_GROUP, LANES), 0) & 1) == 1
        for level in range(1, n_levels + 1):
            for w in work:
                if level == 1:
                    arg = jnp.where(odd, 0.0, w["logf"]) if w["rev"] else jnp.where(odd, w["logf"], 0.0)
                else:
                    arg = _neg_abs(w["c"] - _level_refs(w["c"], level, w["rev"]))
                e = jnp.exp2(arg).astype(BF16)
                w["attn"] = jnp.where(w["lv"][...] == level, nt(w["qb"] * e, w["kb"] * e), w["attn"])
        for w in work:
            c, st, hh = w["c"], w["st"], w["hh"]
            tot = c[0:1] if w["rev"] else c[HGRN_GROUP - 1:HGRN_GROUP]
            s_t = st[hh]
            w["o"][w["rows"], w["cols"]] = _dot(w["attn"].astype(BF16), w["v"]) + nt(w["q"] * jnp.exp2(c), s_t)
            kt = (w["k"] * jnp.exp2(tot - c)).astype(BF16)
            st[hh] = s_t * jnp.exp2(tot) + lax.dot_general(w["v"], kt, TN_DIMS, preferred_element_type=F32)
        return carry

    lax.fori_loop(0, n_groups, group_step, 0, unroll=True)
    for hh in range(n_h):
        cols = slice(hh * LANES, (hh + 1) * LANES)
        if want_state:
            sfin_ref[0, hh] = stf[hh].T
            sfin_ref[1, hh] = stb[hh].T
        o_ref[:, cols] = _rms(of_scr[:, cols] + ob_scr[:, cols], gn_ref[...]) * gb_ref[:, cols]


def _hgrn(q, kf, lf, kb, lbw, v, gb, gnorm, n_batch, s0=None, state_out=None, n_h=1):
    want_state = state_out is not None
    seq = q.shape[0] // n_batch
    tok = pl.BlockSpec((seq, n_h * LANES), lambda b, h: (b, h))
    const = pl.BlockSpec((HGRN_GROUP, HGRN_GROUP), lambda b, h: (0, 0))
    in_specs = [tok] * 7 + [pl.BlockSpec((1, LANES), lambda b, h: (0, 0))] + [const] * 2
    args = [q, kf, lf, kb, lbw, v, gb, gnorm.reshape(1, LANES)] + list(_hgrn_levels())
    if s0 is not None:
        state, layer = s0
        in_specs.append(pl.BlockSpec((None, None, 2, n_h, B_DK, B_DV), lambda b, h: (b, layer, 0, h, 0, 0)))
        args.append(state)
    out_specs = [tok]
    out_shape = [jax.ShapeDtypeStruct(q.shape, F32)]
    aliases = {}
    if want_state:
        buf, out_layer = state_out
        in_specs.append(pl.BlockSpec(memory_space=pl.ANY))
        args.append(buf)
        aliases = {len(args) - 1: 1}
        out_specs.append(pl.BlockSpec((None, None, 2, n_h, B_DK, B_DV), lambda b, h: (b, out_layer, 0, h, 0, 0)))
        out_shape.append(jax.ShapeDtypeStruct(buf.shape, F32))
    res = pl.pallas_call(
        functools.partial(_hgrn_kernel, seq, n_h, s0 is not None, want_state),
        grid=(n_batch, B_HEADS // n_h),
        in_specs=in_specs,
        out_specs=out_specs,
        out_shape=out_shape,
        input_output_aliases=aliases,
        scratch_shapes=[pltpu.VMEM((seq, n_h * LANES), F32),
                        pltpu.VMEM((seq, n_h * LANES), F32),
                        pltpu.VMEM((n_h, B_DV, B_DK), F32),
                        pltpu.VMEM((n_h, B_DV, B_DK), F32)],
        compiler_params=_cparams(2),
        name="hgrn2",
    )(*args)
    return res if want_state else (res[0], None)


def _outffn_kernel(o1_ref, o2_ref, x_ref, mod_ref, wo_ref, ln_ref, wg_ref, wu_ref, wd_ref, out_ref):
    m = mod_ref[0]
    half = o1_ref.shape[1]
    y = (_dot(o1_ref[...].astype(BF16), wo_ref[0:half, :])
         + _dot(o2_ref[...].astype(BF16), wo_ref[half:2 * half, :]))
    x1 = _layer_norm(ALPHA * x_ref[...] + m[2:3] * y, ln_ref[0:1, :], ln_ref[1:2, :])
    h = (x1 * (1.0 + m[4:5]) + m[3:4]).astype(BF16)
    acc = None
    for a in range(0, D_FF, 512):
        b = min(a + 512, D_FF)
        act = (_silu(_dot(h, wg_ref[:, a:b])) * _dot(h, wu_ref[:, a:b])).astype(BF16)
        part = _dot(act, wd_ref[a:b, :])
        acc = part if acc is None else acc + part
    out_ref[...] = _layer_norm(ALPHA * x1 + m[5:6] * acc, ln_ref[2:3, :], ln_ref[3:4, :])


def _outffn(o1, o2, x, mod, wo_bf, ln, wg_bf, wu_bf, wd_bf, rows_per_cond, tm=512):
    m_rows = x.shape[0]
    half = o1.shape[1]
    consts = [wo_bf, ln, wg_bf, wu_bf, wd_bf]
    in_specs = [pl.BlockSpec((tm, half), lambda i: (i, 0)),
                pl.BlockSpec((tm, half), lambda i: (i, 0)),
                pl.BlockSpec((tm, D_MODEL), lambda i: (i, 0)),
                pl.BlockSpec((1, 6, D_MODEL), lambda i: (i * tm // rows_per_cond, 0, 0))]
    in_specs += [pl.BlockSpec(a.shape, lambda i: (0, 0)) for a in consts]
    return pl.pallas_call(
        _outffn_kernel,
        grid=(m_rows // tm,),
        in_specs=in_specs,
        out_specs=pl.BlockSpec((tm, D_MODEL), lambda i: (i, 0)),
        out_shape=jax.ShapeDtypeStruct((m_rows, D_MODEL), F32),
        compiler_params=_cparams(1),
        name="outproj_ffn",
    )(o1, o2, x, mod, *consts)


def _cache_spec(past, width, layer):
    return pl.BlockSpec((None, None, past, width), lambda b, i: (b, layer, 0, 0))


def _even_layer(xp, xs, modp, mods, w_in_bf, lb, sink, gnorm, tail_w, rope_hd,
                cache_k, cache_v, state, new_state, e, nbp, nbs):
    seq_p = xp.shape[0] // nbp
    seq_s = xs.shape[0] // nbs
    past = cache_k.shape[2]
    pp = _inproj_even(xp, modp, w_in_bf, lb, None, xp.shape[0])
    ps = _inproj_even(xs, mods, w_in_bf, lb, rope_hd, seq_s)
    qa, ka, va, qb, ib, kf, lf, kb, lbw, gb = pp
    oa_p = _attention(qa, ka, va, nbp, A_HEADS, A_HEADS // A_KV, seq_p, sink=sink, n_sub=PROMPT_SUB_A)
    ob_p, s_new = _hgrn(qb, kf, lf, kb, lbw, ib, gb, gnorm, nbp, state_out=(new_state, e), n_h=PROMPT_HGRN_HEADS)
    xp = _outffn(oa_p, ob_p, xp, modp, *tail_w, xp.shape[0])
    new = (ka.reshape(nbp, seq_p, A_KV, HEAD_DIM), va.reshape(nbp, seq_p, A_KV, HEAD_DIM), s_new)
    qa, ka, va, qb, ib, kf, lf, kb, lbw, gb = ps
    spec = _cache_spec(past, A_KV * HEAD_DIM, e)
    oa_s = _attention(qa, ka, va, nbs, A_HEADS, A_HEADS // A_KV, BAND_TQ, band=True,
                      ctx=(cache_k, cache_v, spec, spec, past), sink=sink, n_qt=BAND_QT)
    ob_s, _ = _hgrn(qb, kf, lf, kb, lbw, ib, gb, gnorm, nbs, s0=(state, e), n_h=SAMPLE_HGRN_HEADS)
    xs = _outffn(oa_s, ob_s, xs, mods, *tail_w, seq_s)
    return xp, xs, new


def _odd_layer(xp, xs, modp, mods, in_w, tail_w, rope_hd, rope_c,
               cache_ckv, cache_pe, cache_k, cache_v, o, nbp, nbs):
    seq_p = xp.shape[0] // nbp
    seq_s = xs.shape[0] // nbs
    past = cache_ckv.shape[2]
    wkv_bf = in_w[4]
    mla_mul = (C_NOPE + C_ROPE) ** -0.5 * LOG2E
    qc, ckv, kc, vc, pe, qd, kd, vd = _inproj_odd(xp, modp, *in_w, None, xp.shape[0])
    oc_p = _attention(qc, kc, vc, nbp, C_HEADS, 1, seq_p, wide=True, score_mul=mla_mul)
    od_p = _attention(qd, kd, vd, nbp, D_HEADS, D_HEADS // D_KV, seq_p, n_sub=PROMPT_SUB_D)
    xp = _outffn(oc_p, od_p, xp, modp, *tail_w, xp.shape[0])
    new = (ckv.reshape(nbp, seq_p, C_KV_LORA), pe[:, C_NOPE:C_NOPE + C_ROPE].reshape(nbp, seq_p, C_ROPE),
           kd.reshape(nbp, seq_p, D_KV, HEAD_DIM), vd.reshape(nbp, seq_p, D_KV, HEAD_DIM))
    qc, ckv, kc, vc, pe, qd, kd, vd = _inproj_odd(xs, mods, *in_w, rope_hd + rope_c, seq_s)
    pe_ctx = jnp.pad(cache_pe[:, o].reshape(nbs * past, C_ROPE), ((0, 0), (C_NOPE, LANES - C_NOPE - C_ROPE)))
    kcc, vcc = _kvup(cache_ckv[:, o].reshape(nbs * past, C_KV_LORA), pe_ctx, wkv_bf)
    oc_s = _attention(qc, kc, vc, nbs, C_HEADS, 1, SAMPLE_TQ, wide=True, score_mul=mla_mul,
                      ctx=(kcc, vcc, pl.BlockSpec((past, C_HEADS * LANES), lambda b, i: (b, 0)),
                           pl.BlockSpec((past, C_HEADS * C_V), lambda b, i: (b, 0)), past), n_qt=SAMPLE_QT)
    spec = _cache_spec(past, D_KV * HEAD_DIM, o)
    od_s = _attention(qd, kd, vd, nbs, D_HEADS, D_HEADS // D_KV, SAMPLE_TQ, ctx=(cache_k, cache_v, spec, spec, past),
                      n_qt=SAMPLE_QT)
    xs = _outffn(oc_s, od_s, xs, mods, *tail_w, seq_s)
    return xp, xs, new


def kernel(x_prompt, x_sample, cache_a_k, cache_a_v, state_b, cache_c_kv, cache_c_pe, cache_d_k, cache_d_v, c, c_ctx, w_ada, b_ada, ln_g, ln_b, w_in_ab, a_sink, b_lb, b_gnorm, w_in_cd, c_q_norm, c_kv_norm, c_w_q_up, c_w_kv_up, d_q_norm, d_k_norm, w_out, w_ffn_gate, w_ffn_up, w_ffn_down):
    nbp, seq_p, _ = x_prompt.shape
    nbs, seq_s, _ = x_sample.shape
    past = cache_a_k.shape[2]
    xp = x_prompt.reshape(nbp * seq_p, D_MODEL)
    xs = x_sample.reshape(nbs * seq_s, D_MODEL)
    rope_hd = _rope_tables(seq_s, HEAD_DIM)
    cos_c, sin_c = _rope_tables(seq_s, C_ROPE)
    ones = jnp.ones((seq_s, C_NOPE), F32)
    rope_c = (jnp.concatenate([ones, cos_c[:, :C_NOPE]], axis=1),
              jnp.concatenate([0.0 * ones, sin_c[:, :C_ROPE], 0.0 * ones[:, :LANES - C_NOPE - C_ROPE]], axis=1))
    lb_w = jax.nn.softmax(b_lb.astype(F32), axis=0)
    lb_all = jnp.cumsum(lb_w, axis=0) - lb_w[:1]
    mod_all = _modulation(c, c_ctx, w_ada, b_ada)
    n_even = cache_a_k.shape[1]
    n_odd = cache_c_kv.shape[1]
    ca_k = cache_a_k.reshape(nbs, n_even, past, A_KV * HEAD_DIM)
    ca_v = cache_a_v.reshape(nbs, n_even, past, A_KV * HEAD_DIM)
    cd_k = cache_d_k.reshape(nbs, n_odd, past, D_KV * HEAD_DIM)
    cd_v = cache_d_v.reshape(nbs, n_odd, past, D_KV * HEAD_DIM)
    ak, av, ckv, cpe, dk, dv = [], [], [], [], [], []
    new_state_b = jnp.zeros((nbp, n_even, 2, B_HEADS, B_DK, B_DV), F32)
    for l in range(DEPTH):
        modp = mod_all[l, 0:1]
        mods = mod_all[l, 1:1 + nbs]
        ln = jnp.concatenate([ln_g[l, 0:1], ln_b[l, 0:1], ln_g[l, 1:2], ln_b[l, 1:2]], axis=0)
        wo, wg, wu, wd = _to_bf16([w_out, w_ffn_gate, w_ffn_up, w_ffn_down], l)
        tail_w = (wo, ln, wg, wu, wd)
        if l % 2 == 0:
            e = l // 2
            xp, xs, (k_new, v_new, s_new) = _even_layer(
                xp, xs, modp, mods, _to_bf16([w_in_ab], e)[0], lb_all[e], a_sink[e], b_gnorm[e], tail_w,
                rope_hd, ca_k, ca_v, state_b, new_state_b, e, nbp, nbs)
            new_state_b = s_new
            ak.append(k_new)
            av.append(v_new)
        else:
            o = l // 2
            w_in = w_in_cd[o]
            lat = C_Q_LORA + C_KV_LORA
            zeros = lambda n: jnp.zeros((D_MODEL, n), F32)
            w_in = jnp.concatenate([w_in[:, :lat], zeros(C_NOPE), w_in[:, lat:lat + C_ROPE],
                                    zeros(LANES - C_NOPE - C_ROPE), w_in[:, lat + C_ROPE:]], axis=1).astype(BF16)
            wq = c_w_q_up[o].reshape(C_Q_LORA, C_HEADS, C_NOPE + C_ROPE)
            wq = jnp.pad(wq, ((0, 0), (0, 0), (0, LANES - C_NOPE - C_ROPE))).reshape(C_Q_LORA, -1).astype(BF16)
            wkv = c_w_kv_up[o].reshape(C_KV_LORA, C_HEADS, C_NOPE + C_V)
            wk = jnp.pad(wkv[:, :, :C_NOPE], ((0, 0), (0, 0), (0, LANES - C_NOPE))).reshape(C_KV_LORA, -1)
            wkv = jnp.concatenate([wk, wkv[:, :, C_NOPE:].reshape(C_KV_LORA, -1)], axis=1).astype(BF16)
            in_w = (w_in, c_q_norm[o].reshape(1, -1), c_kv_norm[o].reshape(1, -1), wq, wkv,
                    jnp.tile(d_q_norm[o], D_HEADS).reshape(1, -1), jnp.tile(d_k_norm[o], D_KV).reshape(1, -1))
            xp, xs, (c_new, pe_new, k_new, v_new) = _odd_layer(
                xp, xs, modp, mods, in_w, tail_w, rope_hd, rope_c, cache_c_kv, cache_c_pe, cd_k, cd_v, o, nbp, nbs)
            ckv.append(c_new)
            cpe.append(pe_new)
            dk.append(k_new)
            dv.append(v_new)
    return (xp.reshape(nbp, seq_p, D_MODEL), xs.reshape(nbs, seq_s, D_MODEL),
            jnp.stack(ak, axis=1), jnp.stack(av, axis=1), new_state_b,
            jnp.stack(ckv, axis=1), jnp.stack(cpe, axis=1), jnp.stack(dk, axis=1), jnp.stack(dv, axis=1))
```
